```python
import functools
import jax, jax.numpy as jnp
from jax import lax
import numpy as np

D_MODEL = 1024
BATCH = 2
SEQ = 8192
DEPTH = 1
DEC_BATCH = 32
DEC_SEQ = 4
PAST_LEN = 8192
PAGE_SIZE = 128

N_HEADS = 8
HEAD_DIM = 64
ATT_WIDTH = N_HEADS * HEAD_DIM
CONV_WIDTH = 512
CONV_K = 31
FFN_HIDDEN = -(-8 * D_MODEL // (3 * 256)) * 256
Q_BLOCK = 128
FORGET_BIAS = 7.0
EPS = 1e-6
NEG_INF = -1e30

Q_OFF = 0
K_OFF = Q_OFF + ATT_WIDTH
V_OFF = K_OFF + ATT_WIDTH
F_OFF = V_OFF + ATT_WIDTH
GLU_OFF = F_OFF + N_HEADS
GA_OFF = GLU_OFF + 2 * CONV_WIDTH
GB_OFF = GA_OFF + D_MODEL
N_IN = GB_OFF + D_MODEL

kernel_name = 'fox_conformer_gated_hybrid_step'


def _rmsnorm(x, g):
    xf = x.astype(jnp.float32)
    y = xf * lax.rsqrt(jnp.mean(xf * xf, axis=-1, keepdims=True) + EPS)
    return (y * g.astype(jnp.float32)).astype(x.dtype)


def _layernorm(x, g, b):
    xf = x.astype(jnp.float32)
    mu = jnp.mean(xf, axis=-1, keepdims=True)
    var = jnp.mean(jnp.square(xf - mu), axis=-1, keepdims=True)
    y = (xf - mu) * lax.rsqrt(var + EPS) * g.astype(jnp.float32) + b.astype(jnp.float32)
    return y.astype(x.dtype)


def _fox_prompt(q, k, v, logf):
    B, S = q.shape[0], q.shape[1]
    nb = S // Q_BLOCK
    scale = HEAD_DIM ** -0.5
    kf = k.astype(jnp.float32)
    vf = v.astype(jnp.float32)
    C = lax.cumsum(logf.astype(jnp.float32), axis=1)
    Ck = jnp.transpose(C, (0, 2, 1))
    qb = jnp.swapaxes(q.reshape(B, nb, Q_BLOCK, N_HEADS, HEAD_DIM), 0, 1)
    Cb = jnp.swapaxes(C.reshape(B, nb, Q_BLOCK, N_HEADS), 0, 1)
    kpos = jnp.arange(S)

    def one_block(args):
        qi, Ci, i = args
        s = jnp.einsum('bqhd,bkhd->bhqk', qi.astype(jnp.float32), kf) * scale
        s = s + jnp.transpose(Ci, (0, 2, 1))[..., None] - Ck[:, :, None, :]
        qpos = i * Q_BLOCK + jnp.arange(Q_BLOCK)
        s = jnp.where(kpos[None, :] <= qpos[:, None], s, NEG_INF)
        p = jax.nn.softmax(s, axis=-1)
        return jnp.einsum('bhqk,bkhd->bqhd', p, vf)

    o = lax.map(one_block, (qb, Cb, jnp.arange(nb)))
    return jnp.swapaxes(o, 0, 1).reshape(B, S, N_HEADS, HEAD_DIM).astype(q.dtype)


def _fox_sample(q, k, v, logf, k_past, v_past, logf_past):
    P = k_past.shape[1]
    T = q.shape[1]
    scale = HEAD_DIM ** -0.5
    qf = q.astype(jnp.float32)
    Cn = jnp.transpose(lax.cumsum(logf.astype(jnp.float32), axis=1), (0, 2, 1))
    R = lax.cumsum(logf_past.astype(jnp.float32), axis=1, reverse=True)
    R = jnp.concatenate([R[:, 1:], jnp.zeros_like(R[:, :1])], axis=1)
    R = jnp.transpose(R, (0, 2, 1))
    s_past = jnp.einsum('bthd,bshd->bhts', qf, k_past.astype(jnp.float32)) * scale
    s_past = s_past + Cn[..., None] + R[:, :, None, :]
    s_new = jnp.einsum('bthd,bshd->bhts', qf, k.astype(jnp.float32)) * scale
    s_new = s_new + Cn[..., None] - Cn[:, :, None, :]
    s_new = jnp.where(jnp.tril(jnp.ones((T, T), dtype=bool)), s_new, NEG_INF)
    p = jax.nn.softmax(jnp.concatenate([s_past, s_new], axis=-1), axis=-1)
    o = (jnp.einsum('bhts,bshd->bthd', p[..., :P], v_past.astype(jnp.float32))
         + jnp.einsum('bhts,bshd->bthd', p[..., P:], v.astype(jnp.float32)))
    return o.astype(q.dtype)


def _depthwise_causal_conv(u_pad, w, b):
    y = lax.conv_general_dilated(u_pad, w[:, None, :].astype(u_pad.dtype), window_strides=(1,),
                                 padding='VALID', dimension_numbers=('NWC', 'WIO', 'NWC'),
                                 feature_group_count=CONV_WIDTH)
    return y + b.astype(u_pad.dtype)


def _layer(x, c, lw, attend, conv_left):
    (rms1_g, rms2_g, w_ada, b_ada, w_in, b_in, dw_w, dw_b, ln_g, ln_b,
     w_pa, w_pb, b_pb, w_o, w_ffn_in, w_ffn_out) = lw
    B, L = x.shape[0], x.shape[1]
    mod = jax.nn.silu(c) @ w_ada + b_ada
    sh1, sc1, g1, sh2, sc2, g2 = jnp.split(mod, 6, axis=-1)
    h = _rmsnorm(x, rms1_g) * (1 + sc1[:, None, :]) + sh1[:, None, :]
    z = h @ w_in + b_in
    q = z[..., Q_OFF:K_OFF].reshape(B, L, N_HEADS, HEAD_DIM)
    k = z[..., K_OFF:V_OFF].reshape(B, L, N_HEADS, HEAD_DIM)
    v = z[..., V_OFF:F_OFF].reshape(B, L, N_HEADS, HEAD_DIM)
    logf = jax.nn.log_sigmoid(z[..., F_OFF:GLU_OFF].astype(jnp.float32))
    glu_a, glu_b = jnp.split(z[..., GLU_OFF:GA_OFF], 2, axis=-1)
    gate_a = jax.nn.sigmoid(z[..., GA_OFF:GB_OFF])
    gate_b = jax.nn.sigmoid(z[..., GB_OFF:N_IN])
    o = attend(q, k, v, logf)
    y_a = o.reshape(B, L, ATT_WIDTH) @ w_pa
    u = glu_a * jax.nn.sigmoid(glu_b)
    u_pad = jnp.concatenate([conv_left.astype(u.dtype), u], axis=1)
    y_b = _depthwise_causal_conv(u_pad, dw_w, dw_b)
    y_b = jax.nn.silu(_layernorm(y_b, ln_g, ln_b)) @ w_pb + b_pb
    m = gate_a * y_a + gate_b * y_b
    x = x + g1[:, None, :] * (m @ w_o)
    h2 = _rmsnorm(x, rms2_g) * (1 + sc2[:, None, :]) + sh2[:, None, :]
    f_gate, f_up = jnp.split(h2 @ w_ffn_in, 2, axis=-1)
    x = x + g2[:, None, :] * ((jax.nn.silu(f_gate) * f_up) @ w_ffn_out)
    return x, k, v, logf, u_pad[:, -(CONV_K - 1):]


def setup_inputs(seed: int = 0) -> dict:
    key = jax.random.key(seed)
    ks = jax.random.split(key, 32)
    f32 = jnp.float32
    n_pages = PAST_LEN // PAGE_SIZE
    n_used = DEC_BATCH * n_pages
    n_phys = n_used + (n_used + 3) // 4

    def nrm(k, shape, fan_in, gain=1.0):
        return jax.random.normal(k, shape, f32) * (gain * fan_in ** -0.5)

    def small(k, shape):
        return 0.02 * jax.random.normal(k, shape, f32)

    def gain(k, shape):
        return 1.0 + 0.1 * jax.random.normal(k, shape, f32)

    page_table = jax.random.permutation(ks[0], n_phys)[:n_used].reshape(DEC_BATCH, n_pages).astype(jnp.int32)
    b_in = small(ks[1], (DEPTH, N_IN)).at[:, F_OFF:GLU_OFF].add(FORGET_BIAS)
    return {
        'x_prompt': jax.random.normal(ks[2], (BATCH, SEQ, D_MODEL), f32),
        'x_sample': jax.random.normal(ks[3], (DEC_BATCH, DEC_SEQ, D_MODEL), f32),
        'c_prompt': jax.random.normal(ks[4], (BATCH, D_MODEL), f32),
        'c_sample': jax.random.normal(ks[5], (DEC_BATCH, D_MODEL), f32),
        'cache_k': jax.random.normal(ks[6], (DEPTH, n_phys, PAGE_SIZE, N_HEADS, HEAD_DIM), f32),
        'cache_v': jax.random.normal(ks[7], (DEPTH, n_phys, PAGE_SIZE, N_HEADS, HEAD_DIM), f32),
        'cache_logf': jax.nn.log_sigmoid(FORGET_BIAS + 0.5 * jax.random.normal(ks[8], (DEPTH, n_phys, PAGE_SIZE, N_HEADS), f32)),
        'state_conv': 0.5 * jax.random.normal(ks[9], (DEPTH, DEC_BATCH, CONV_K - 1, CONV_WIDTH), f32),
        'page_table': page_table,
        'rms1_g': gain(ks[10], (DEPTH, D_MODEL)),
        'rms2_g': gain(ks[11], (DEPTH, D_MODEL)),
        'w_ada': nrm(ks[12], (DEPTH, D_MODEL, 6 * D_MODEL), D_MODEL, 0.5),
        'b_ada': small(ks[13], (DEPTH, 6 * D_MODEL)),
        'w_in': nrm(ks[14], (DEPTH, D_MODEL, N_IN), D_MODEL),
        'b_in': b_in,
        'dw_w': nrm(ks[15], (DEPTH, CONV_K, CONV_WIDTH), CONV_K),
        'dw_b': small(ks[16], (DEPTH, CONV_WIDTH)),
        'ln_g': gain(ks[17], (DEPTH, CONV_WIDTH)),
        'ln_b': small(ks[18], (DEPTH, CONV_WIDTH)),
        'w_pa': nrm(ks[19], (DEPTH, ATT_WIDTH, D_MODEL), ATT_WIDTH),
        'w_pb': nrm(ks[20], (DEPTH, CONV_WIDTH, D_MODEL), CONV_WIDTH),
        'b_pb': small(ks[21], (DEPTH, D_MODEL)),
        'w_o': nrm(ks[22], (DEPTH, D_MODEL, D_MODEL), D_MODEL),
        'w_ffn_in': nrm(ks[23], (DEPTH, D_MODEL, 2 * FFN_HIDDEN), D_MODEL),
        'w_ffn_out': nrm(ks[24], (DEPTH, FFN_HIDDEN, D_MODEL), FFN_HIDDEN),
        'final_g': gain(ks[25], (D_MODEL,)),
    }


def reference(x_prompt, x_sample, c_prompt, c_sample, cache_k, cache_v, cache_logf, state_conv,
              page_table, rms1_g, rms2_g, w_ada, b_ada, w_in, b_in, dw_w, dw_b, ln_g, ln_b,
              w_pa, w_pb, b_pb, w_o, w_ffn_in, w_ffn_out, final_g):
    B, S = x_prompt.shape[0], x_prompt.shape[1]
    DB = x_sample.shape[0]
    n_pages = page_table.shape[1]
    P = n_pages * PAGE_SIZE
    xp, xs = x_prompt, x_sample
    kp_l, vp_l, lp_l, cp_l, ks_l, vs_l, ls_l, cs_l = [], [], [], [], [], [], [], []
    for l in range(DEPTH):
        lw = (rms1_g[l], rms2_g[l], w_ada[l], b_ada[l], w_in[l], b_in[l], dw_w[l], dw_b[l],
              ln_g[l], ln_b[l], w_pa[l], w_pb[l], b_pb[l], w_o[l], w_ffn_in[l], w_ffn_out[l])
        zero_left = jnp.zeros((B, CONV_K - 1, CONV_WIDTH), xp.dtype)
        xp, k_p, v_p, lf_p, c_p = _layer(xp, c_prompt, lw, _fox_prompt, zero_left)
        kp_l.append(k_p.reshape(B, S // PAGE_SIZE, PAGE_SIZE, N_HEADS, HEAD_DIM))
        vp_l.append(v_p.reshape(B, S // PAGE_SIZE, PAGE_SIZE, N_HEADS, HEAD_DIM))
        lp_l.append(lf_p.reshape(B, S // PAGE_SIZE, PAGE_SIZE, N_HEADS))
        cp_l.append(c_p)
        k_past = cache_k[l][page_table].reshape(DB, P, N_HEADS, HEAD_DIM)
        v_past = cache_v[l][page_table].reshape(DB, P, N_HEADS, HEAD_DIM)
        lf_past = cache_logf[l][page_table].reshape(DB, P, N_HEADS)
        attend_s = functools.partial(_fox_sample, k_past=k_past, v_past=v_past, logf_past=lf_past)
        xs, k_s, v_s, lf_s, c_s = _layer(xs, c_sample, lw, attend_s, state_conv[l])
        ks_l.append(k_s)
        vs_l.append(v_s)
        ls_l.append(lf_s)
        cs_l.append(c_s)
    y_prompt = _rmsnorm(xp, final_g)
    y_sample = _rmsnorm(xs, final_g)
    return (y_prompt, y_sample,
            jnp.stack(kp_l), jnp.stack(vp_l), jnp.stack(lp_l), jnp.stack(cp_l),
            jnp.stack(ks_l), jnp.stack(vs_l), jnp.stack(ls_l), jnp.stack(cs_l))
```

```python
import functools

import numpy as np
import jax
import jax.numpy as jnp
from jax import lax
from jax.experimental import pallas as pl
from jax.experimental.pallas import tpu as pltpu

F32 = jnp.float32
BF16 = jnp.bfloat16

D_MODEL = 1024
N_HEADS = 8
HEAD_DIM = 64
ATT_WIDTH = N_HEADS * HEAD_DIM
CONV_WIDTH = 512
CONV_K = 31
FFN_HIDDEN = 2816
PAGE_SIZE = 128
EPS = 1e-6
NEG_INF = -1e30
SCALE = HEAD_DIM ** -0.5

LANES = 128
N_PAIRS = N_HEADS // 2
N_SLABS = CONV_WIDTH // LANES
AUG = 3
SAMPLE_T = 8
CONV_HALO = 32
SAMPLE_WIN = 40
PAGES_PER_STEP = 8
VMEM_LIMIT = 56 * 1024 * 1024

C_Q, C_K, C_V, C_F, C_GLA, C_GLB, C_GA, C_GB, C_END = 0, 512, 1024, 1536, 1664, 2176, 2688, 3712, 4736


def _sigmoid(x):
    return 1.0 / (1.0 + jnp.exp(-x))


def _silu(x):
    return x * _sigmoid(x)


def _split3_packed(a, lane):
    a = jnp.where(lane < N_HEADS, a, 0.0)
    hi = a.astype(BF16).astype(F32)
    r1 = a - hi
    mid = r1.astype(BF16).astype(F32)
    lo = (r1 - mid).astype(BF16).astype(F32)
    packed = hi + pltpu.roll(mid, N_HEADS, axis=1) + pltpu.roll(lo, 2 * N_HEADS, axis=1)
    return packed.astype(BF16)


def _unpack3(p, lane):
    s = p + pltpu.roll(p, LANES - N_HEADS, axis=1) + pltpu.roll(p, LANES - 2 * N_HEADS, axis=1)
    return jnp.where(lane < N_HEADS, s, 0.0)


def _mod_kernel(c_ref, w_ref, b_ref, o_ref):
    s = _silu(c_ref[...]).astype(BF16)
    o_ref[...] = jnp.dot(s, w_ref[...].astype(BF16), preferred_element_type=F32) + b_ref[...]


def _modulation(c_all, w_ada, b_ada):
    rows = c_all.shape[0]
    n = w_ada.shape[1]
    tn = 768
    return pl.pallas_call(
        _mod_kernel,
        out_shape=jax.ShapeDtypeStruct((rows, n), F32),
        grid=(n // tn,),
        in_specs=[pl.BlockSpec((rows, D_MODEL), lambda j: (0, 0)),
                  pl.BlockSpec((D_MODEL, tn), lambda j: (0, j)),
                  pl.BlockSpec((1, tn), lambda j: (0, j))],
        out_specs=pl.BlockSpec((rows, tn), lambda j: (0, j)),
        compiler_params=pltpu.CompilerParams(dimension_semantics=("arbitrary",), vmem_limit_bytes=VMEM_LIMIT),
        name="modulation",
    )(c_all, w_ada, b_ada)


def _inproj_kernel(x_ref, sh_ref, sc_ref, g_ref, w_ref, b_ref, tri_ref, place_ref, ones_ref,
                   qa_ref, ka_ref, k_ref, v_ref, vb_ref, lf_ref, c_ref, u_ref, ga_ref, gb_ref, *rest,
                   sample):
    if sample:
        qp_ref, carry_ref = rest
    else:
        (carry_ref,) = rest
    tm = x_ref.shape[0]

    @pl.when(pl.program_id(1) == 0)
    def _():
        carry_ref[...] = jnp.zeros_like(carry_ref)

    x = x_ref[...]
    ms = jnp.mean(x * x, axis=-1, keepdims=True)
    h = x * lax.rsqrt(ms + EPS) * g_ref[...]
    h = h * (1.0 + sc_ref[0]) + sh_ref[0]
    hb = h.astype(BF16)

    def seg(lo, hi):
        return jnp.dot(hb, w_ref[:, lo:hi], preferred_element_type=F32) + b_ref[:, lo:hi]

    lane = lax.broadcasted_iota(jnp.int32, (tm, LANES), 1)
    zf = seg(C_F, C_GLA)
    lf = jnp.minimum(zf, 0.0) - jnp.log1p(jnp.exp(-jnp.abs(zf)))
    lf = jnp.where(lane < N_HEADS, lf, 0.0)
    lf_ref[...] = lf[:, :N_HEADS]
    csum = _unpack3(jnp.dot(tri_ref[...], _split3_packed(lf, lane), preferred_element_type=F32), lane)
    csum = csum + carry_ref[...]
    carry_ref[...] = csum[tm - 1:tm, :]
    c_ref[...] = csum
    aug = jnp.dot(_split3_packed(csum, lane), place_ref[...], preferred_element_type=F32) + ones_ref[...]

    zq = seg(C_Q, C_K) * SCALE
    zk = seg(C_K, C_V)
    zv = seg(C_V, C_F)
    k_ref[...] = zk
    v_ref[...] = zv
    vb_ref[...] = zv.astype(BF16)
    if sample:
        qp_ref[...] = zq.astype(BF16)
    for p in range(N_PAIRS):
        lo, hi = p * LANES, (p + 1) * LANES
        qa_ref[:, 2 * lo:2 * lo + LANES] = zq[:, lo:hi].astype(BF16)
        qa_ref[:, 2 * lo + LANES:2 * hi] = aug[:, lo:hi].astype(BF16)
        ka_ref[:, 2 * lo:2 * lo + LANES] = zk[:, lo:hi].astype(BF16)
        ka_ref[:, 2 * lo + LANES:2 * hi] = aug[:, ATT_WIDTH + lo:ATT_WIDTH + hi].astype(BF16)

    u = seg(C_GLA, C_GLB) * _sigmoid(seg(C_GLB, C_GA))
    for c in range(N_SLABS):
        u_ref[c] = u[:, c * LANES:(c + 1) * LANES]
    ga_ref[...] = _sigmoid(seg(C_GA, C_GB)).astype(BF16)
    gb_ref[...] = _sigmoid(seg(C_GB, C_END)).astype(BF16)


def _inproj(x, sh, sc, g1, w_all, b_all, tri, place, ones_row, *, nb, tm, sample):
    n = x.shape[0]
    nt = n // (nb * tm)
    row = lambda b, i: (b * nt + i, 0)
    const2 = lambda b, i: (0, 0)
    mod_rows = sh.shape[1]
    mod_spec = pl.BlockSpec((1, mod_rows, D_MODEL), lambda b, i: (b, 0, 0))
    once = dict(pipeline_mode=pl.Buffered(1))
    in_specs = [pl.BlockSpec((tm, D_MODEL), row), mod_spec, mod_spec,
                pl.BlockSpec((1, D_MODEL), const2),
                pl.BlockSpec((D_MODEL, C_END), const2, **once),
                pl.BlockSpec((1, C_END), const2),
                pl.BlockSpec((tm, tm), const2, **once),
                pl.BlockSpec((LANES, 2 * ATT_WIDTH), const2, **once),
                pl.BlockSpec((1, 2 * ATT_WIDTH), const2)]
    out_shape = [jax.ShapeDtypeStruct((n, 2 * ATT_WIDTH), BF16),
                 jax.ShapeDtypeStruct((n, 2 * ATT_WIDTH), BF16),
                 jax.ShapeDtypeStruct((n, ATT_WIDTH), F32),
                 jax.ShapeDtypeStruct((n, ATT_WIDTH), F32),
                 jax.ShapeDtypeStruct((n, ATT_WIDTH), BF16),
                 jax.ShapeDtypeStruct((n, N_HEADS), F32),
                 jax.ShapeDtypeStruct((n, LANES), F32),
                 jax.ShapeDtypeStruct((N_SLABS, n, LANES), F32),
                 jax.ShapeDtypeStruct((n, D_MODEL), BF16),
                 jax.ShapeDtypeStruct((n, D_MODEL), BF16)]
    out_specs = [pl.BlockSpec((tm, 2 * ATT_WIDTH), row), pl.BlockSpec((tm, 2 * ATT_WIDTH), row),
                 pl.BlockSpec((tm, ATT_WIDTH), row), pl.BlockSpec((tm, ATT_WIDTH), row),
                 pl.BlockSpec((tm, ATT_WIDTH), row), pl.BlockSpec((tm, N_HEADS), row),
                 pl.BlockSpec((tm, LANES), row),
                 pl.BlockSpec((N_SLABS, tm, LANES), lambda b, i: (0, b * nt + i, 0)),
                 pl.BlockSpec((tm, D_MODEL), row), pl.BlockSpec((tm, D_MODEL), row)]
    if sample:
        out_shape.append(jax.ShapeDtypeStruct((n, ATT_WIDTH), BF16))
        out_specs.append(pl.BlockSpec((tm, ATT_WIDTH), row))
    return pl.pallas_call(
        functools.partial(_inproj_kernel, sample=sample),
        out_shape=out_shape,
        grid=(nb, nt),
        in_specs=in_specs,
        out_specs=out_specs,
        scratch_shapes=[pltpu.VMEM((1, LANES), F32)],
        compiler_params=pltpu.CompilerParams(dimension_semantics=("arbitrary", "arbitrary"),
                                             vmem_limit_bytes=VMEM_LIMIT),
        name="inproj_sample" if sample else "inproj_prompt",
    )(x, sh, sc, g1, w_all, b_all, tri, place, ones_row)


def _attn_kernel(qa_ref, ka_ref, v_ref, o_ref, *, tq):
    qi = pl.program_id(2)
    qa = qa_ref[...]
    qlane = lax.broadcasted_iota(jnp.int32, qa.shape, 1)
    q_heads = [jnp.where((qlane // HEAD_DIM) % 2 == hh, qa, jnp.zeros_like(qa)) for hh in range(2)]
    vlane = lax.broadcasted_iota(jnp.int32, (tq, LANES), 1)
    first_head = vlane < HEAD_DIM
    row = lax.broadcasted_iota(jnp.int32, (tq, tq), 0)
    col = lax.broadcasted_iota(jnp.int32, (tq, tq), 1)

    def tile(ki, carry, diagonal):
        m0, l0, m1, l1, acc = carry
        start = pl.multiple_of(ki * tq, tq)
        ka = ka_ref[pl.ds(start, tq), :]
        v2 = v_ref[pl.ds(start, tq), :]
        stats = []
        pv = []
        for hh, (m_prev, l_prev) in enumerate(((m0, l0), (m1, l1))):
            s = lax.dot_general(q_heads[hh], ka, (((1,), (1,)), ((), ())), preferred_element_type=F32)
            if diagonal:
                s = jnp.where(col <= row, s, NEG_INF)
            m_new = jnp.maximum(m_prev, jnp.max(s, axis=-1, keepdims=True))
            alpha = jnp.exp(m_prev - m_new)
            p = jnp.exp(s - m_new)
            l_new = alpha * l_prev + jnp.sum(p, axis=-1, keepdims=True)
            vh = jnp.where(first_head if hh == 0 else ~first_head, v2, jnp.zeros_like(v2))
            pv.append(jnp.dot(p.astype(BF16), vh, preferred_element_type=F32))
            stats.append((m_new, l_new, alpha))
        alpha2 = jnp.where(first_head, stats[0][2], stats[1][2])
        acc = acc * alpha2 + pv[0] + pv[1]
        return stats[0][0], stats[0][1], stats[1][0], stats[1][1], acc

    neg = jnp.full((tq, 1), NEG_INF, F32)
    zero = jnp.zeros((tq, 1), F32)
    init = (neg, zero, neg, zero, jnp.zeros((tq, LANES), F32))
    carry = lax.fori_loop(0, qi, lambda ki, c: tile(ki, c, False), init)
    _, l0, _, l1, acc = tile(qi, carry, True)
    o_ref[...] = (acc / jnp.where(first_head, l0, l1)).astype(o_ref.dtype)


def _prompt_attention(qa, ka, vb, *, nb, seq, tq):
    qa3 = qa.reshape(nb, seq, 2 * ATT_WIDTH)
    ka3 = ka.reshape(nb, seq, 2 * ATT_WIDTH)
    vb3 = vb.reshape(nb, seq, ATT_WIDTH)
    out = pl.pallas_call(
        functools.partial(_attn_kernel, tq=tq),
        out_shape=jax.ShapeDtypeStruct((nb, seq, ATT_WIDTH), BF16),
        grid=(nb, N_PAIRS, seq // tq),
        in_specs=[pl.BlockSpec((None, tq, 2 * LANES), lambda b, p, i: (b, i, p)),
                  pl.BlockSpec((None, seq, 2 * LANES), lambda b, p, i: (b, 0, p)),
                  pl.BlockSpec((None, seq, LANES), lambda b, p, i: (b, 0, p))],
        out_specs=pl.BlockSpec((None, tq, LANES), lambda b, p, i: (b, i, p)),
        compiler_params=pltpu.CompilerParams(dimension_semantics=("arbitrary", "arbitrary", "arbitrary"),
                                             vmem_limit_bytes=VMEM_LIMIT),
        name="prompt_attention",
    )(qa3, ka3, vb3)
    return out.reshape(nb * seq, ATT_WIDTH)


def _sample_attn_kernel(pt_ref, q_ref, kn_ref, vn_ref, cn_ref, tri_ref, *rest):
    del pt_ref
    npg = PAGES_PER_STEP
    k_refs = rest[:npg]
    v_refs = rest[npg:2 * npg]
    f_refs = rest[2 * npg:3 * npg]
    o_ref, qa_scr, m_scr, l_scr, acc_scr, carry_scr, pad_scr = rest[3 * npg:]
    step = pl.program_id(1)
    rows = N_HEADS * SAMPLE_T
    lane = lax.broadcasted_iota(jnp.int32, (PAGE_SIZE, LANES), 1)
    nt_dims = (((1,), (1,)), ((), ()))

    def update(s, v_all):
        m_prev = m_scr[...]
        m_new = jnp.maximum(m_prev, jnp.max(s, axis=-1, keepdims=True))
        alpha = jnp.exp(m_prev - m_new)
        p = jnp.exp(s - m_new)
        l_scr[...] = alpha * l_scr[...] + jnp.sum(p, axis=-1, keepdims=True)
        acc_scr[...] = alpha * acc_scr[...] + jnp.dot(p.astype(BF16), v_all, preferred_element_type=F32)
        m_scr[...] = m_new

    @pl.when(step == 0)
    def _():
        q8 = q_ref[...].astype(F32)
        qt = jnp.concatenate([q8] * N_HEADS, axis=0)
        r_i = lax.broadcasted_iota(jnp.int32, (rows, ATT_WIDTH), 0)
        c_i = lax.broadcasted_iota(jnp.int32, (rows, ATT_WIDTH), 1)
        qa_scr[:, :ATT_WIDTH] = jnp.where(r_i // SAMPLE_T == c_i // HEAD_DIM, qt, 0.0).astype(BF16)
        r_a = lax.broadcasted_iota(jnp.int32, (rows, LANES), 0)
        c_a = lax.broadcasted_iota(jnp.int32, (rows, LANES), 1)
        pick = (c_a < AUG * N_HEADS) & (c_a % N_HEADS == r_a // SAMPLE_T)
        qa_scr[:, ATT_WIDTH:] = jnp.where(pick, 1.0, 0.0).astype(BF16)
        m_scr[...] = jnp.full_like(m_scr, NEG_INF)
        l_scr[...] = jnp.zeros_like(l_scr)
        acc_scr[...] = jnp.zeros_like(acc_scr)
        carry_scr[...] = jnp.zeros_like(carry_scr)
        pad = jnp.zeros((PAGE_SIZE - SAMPLE_T, ATT_WIDTH), F32)
        k_new = jnp.concatenate([kn_ref[...], pad], axis=0).astype(BF16)
        v_new = jnp.concatenate([vn_ref[...], pad], axis=0).astype(BF16)
        c_new = jnp.concatenate([-cn_ref[...], jnp.zeros((PAGE_SIZE - SAMPLE_T, LANES), F32)], axis=0)
        ka = jnp.concatenate([k_new, _split3_packed(c_new, lane)], axis=1)
        s = lax.dot_general(qa_scr[...], ka, nt_dims, preferred_element_type=F32)
        r_s = lax.broadcasted_iota(jnp.int32, (rows, PAGE_SIZE), 0)
        c_s = lax.broadcasted_iota(jnp.int32, (rows, PAGE_SIZE), 1)
        update(jnp.where(c_s <= r_s % SAMPLE_T, s, NEG_INF), v_new)

    pad_scr[...] = jnp.zeros_like(pad_scr)
    carry = carry_scr[...]
    k_parts, v_parts = [], []
    for i in range(npg):
        pad_scr[:, :N_HEADS] = f_refs[i][...]
        lf = pad_scr[...]
        local = _unpack3(jnp.dot(tri_ref[...], _split3_packed(lf, lane), preferred_element_type=F32), lane)
        r_bias = local + carry
        carry = r_bias[0:1, :] + lf[0:1, :]
        k_parts.append(jnp.concatenate([k_refs[i][...].astype(BF16), _split3_packed(r_bias, lane)], axis=1))
        v_parts.append(v_refs[i][...].astype(BF16))
    carry_scr[...] = carry
    s = lax.dot_general(qa_scr[...], jnp.concatenate(k_parts, axis=0), nt_dims, preferred_element_type=F32)
    update(s, jnp.concatenate(v_parts, axis=0))

    @pl.when(step == pl.num_programs(1) - 1)
    def _():
        o = acc_scr[...] / l_scr[...]
        r_i = lax.broadcasted_iota(jnp.int32, (rows, ATT_WIDTH), 0)
        c_i = lax.broadcasted_iota(jnp.int32, (rows, ATT_WIDTH), 1)
        o = jnp.where(r_i // SAMPLE_T == c_i // HEAD_DIM, o, 0.0).astype(BF16)
        t_i = lax.broadcasted_iota(jnp.int32, (SAMPLE_T, rows), 0)
        r_j = lax.broadcasted_iota(jnp.int32, (SAMPLE_T, rows), 1)
        sel = jnp.where(r_j % SAMPLE_T == t_i, 1.0, 0.0).astype(BF16)
        o_ref[...] = jnp.dot(sel, o, preferred_element_type=F32).astype(o_ref.dtype)


def _sample_attention(page_table, q, k_new, v_new, c_new, tri_page, cache_k, cache_v, cache_logf):
    nseq, n_pages = page_table.shape
    npg = PAGES_PER_STEP
    steps = n_pages // npg
    rows = N_HEADS * SAMPLE_T
    seq_spec = lambda width: pl.BlockSpec((None, SAMPLE_T, width), lambda b, s, pt: (b, 0, 0))

    def page_spec(width, i):
        def index(b, s, pt):
            return (pt[b * n_pages + (n_pages - 1) - (s * npg + i)], 0, 0)
        return pl.BlockSpec((None, PAGE_SIZE, width), index)

    in_specs = ([seq_spec(ATT_WIDTH), seq_spec(ATT_WIDTH), seq_spec(ATT_WIDTH), seq_spec(LANES),
                 pl.BlockSpec((PAGE_SIZE, PAGE_SIZE), lambda b, s, pt: (0, 0))]
                + [page_spec(ATT_WIDTH, i) for i in range(npg)]
                + [page_spec(ATT_WIDTH, i) for i in range(npg)]
                + [page_spec(N_HEADS, i) for i in range(npg)])
    grid_spec = pltpu.PrefetchScalarGridSpec(
        num_scalar_prefetch=1,
        grid=(nseq, steps),
        in_specs=in_specs,
        out_specs=seq_spec(ATT_WIDTH),
        scratch_shapes=[pltpu.VMEM((rows, ATT_WIDTH + LANES), BF16),
                        pltpu.VMEM((rows, 1), F32), pltpu.VMEM((rows, 1), F32),
                        pltpu.VMEM((rows, ATT_WIDTH), F32),
                        pltpu.VMEM((1, LANES), F32),
                        pltpu.VMEM((PAGE_SIZE, LANES), F32)])
    return pl.pallas_call(
        _sample_attn_kernel,
        out_shape=jax.ShapeDtypeStruct((nseq, SAMPLE_T, ATT_WIDTH), BF16),
        grid_spec=grid_spec,
        compiler_params=pltpu.CompilerParams(dimension_semantics=("arbitrary", "arbitrary"),
                                             vmem_limit_bytes=VMEM_LIMIT),
        name="sample_attention",
    )(page_table.reshape(-1), q, k_new, v_new, c_new, tri_page,
      *([cache_k] * npg), *([cache_v] * npg), *([cache_logf] * npg))


def _merge_tail(x, conv, o_ref, ga_ref, gb_ref, g1, dwb_ref, lng_ref, lnb_ref, wpa_ref, wpb_ref, bpb_ref, wo_ref):
    yb = conv + dwb_ref[...]
    mu = jnp.mean(yb, axis=-1, keepdims=True)
    var = jnp.mean(jnp.square(yb - mu), axis=-1, keepdims=True)
    yb = (yb - mu) * lax.rsqrt(var + EPS) * lng_ref[...] + lnb_ref[...]
    yb = jnp.dot(_silu(yb).astype(BF16), wpb_ref[...], preferred_element_type=F32) + bpb_ref[...]
    ya = jnp.dot(o_ref[...], wpa_ref[...], preferred_element_type=F32)
    m = ga_ref[...].astype(F32) * ya + gb_ref[...].astype(F32) * yb
    return x + g1 * jnp.dot(m.astype(BF16), wo_ref[...], preferred_element_type=F32)


def _merge_prompt_kernel(x_ref, o_ref, ucur_ref, uprev_ref, ga_ref, gb_ref, g1_ref, dww_ref, dwb_ref, lng_ref,
                         lnb_ref, wpa_ref, wpb_ref, bpb_ref, wo_ref, out_ref, ubuf, ybuf):
    tm = x_ref.shape[0]
    first = pl.program_id(1) == 0
    prev = uprev_ref[...]
    ubuf[:, 0:CONV_HALO, :] = jnp.where(first, jnp.zeros_like(prev), prev)
    ubuf[:, CONV_HALO:, :] = ucur_ref[...]
    chunk = 64
    for c in range(N_SLABS):
        for r0 in range(0, tm, chunk):
            acc = jnp.zeros((chunk, LANES), F32)
            for j in range(CONV_K):
                off = r0 + CONV_HALO - (CONV_K - 1) + j
                acc = acc + dww_ref[c, j:j + 1, :] * ubuf[c, off:off + chunk, :]
            ybuf[r0:r0 + chunk, c * LANES:(c + 1) * LANES] = acc
    out_ref[...] = _merge_tail(x_ref[...], ybuf[...], o_ref, ga_ref, gb_ref, g1_ref[0], dwb_ref, lng_ref, lnb_ref,
                               wpa_ref, wpb_ref, bpb_ref, wo_ref)


def _merge_sample_kernel(x_ref, o_ref, uwin_ref, ga_ref, gb_ref, g1_ref, dww_ref, dwb_ref, lng_ref,
                         lnb_ref, wpa_ref, wpb_ref, bpb_ref, wo_ref, out_ref):
    nseq = uwin_ref.shape[1]
    slabs = []
    for c in range(N_SLABS):
        acc = jnp.zeros((nseq, SAMPLE_T, LANES), F32)
        for j in range(CONV_K):
            acc = acc + dww_ref[c, j:j + 1, :] * uwin_ref[c, :, j:j + SAMPLE_T, :]
        slabs.append(acc.reshape(nseq * SAMPLE_T, LANES))
    conv = jnp.concatenate(slabs, axis=1)
    out_ref[...] = _merge_tail(x_ref[...], conv, o_ref, ga_ref, gb_ref, g1_ref[0], dwb_ref, lng_ref, lnb_ref,
                               wpa_ref, wpb_ref, bpb_ref, wo_ref)


def _merge(x, o, u_args, ga, gb, g1, weights, *, nb, tm, sample):
    n = x.shape[0]
    nt = n // (nb * tm)
    row = lambda b, i: (b * nt + i, 0)
    const2 = lambda b, i: (0, 0)
    const3 = lambda b, i: (0, 0, 0)
    once = dict(pipeline_mode=pl.Buffered(1))
    mod_spec = pl.BlockSpec((1, g1.shape[1], D_MODEL), lambda b, i: (b, 0, 0))
    if sample:
        (uwin,) = u_args
        u_specs = [pl.BlockSpec(uwin.shape, lambda b, i: (0, 0, 0, 0))]
        kernel, scratch = _merge_sample_kernel, []
    else:
        (u,) = u_args
        u_args = (u, u)
        blocks_per_tile = tm // CONV_HALO
        u_specs = [pl.BlockSpec((N_SLABS, tm, LANES), lambda b, i: (0, b * nt + i, 0)),
                   pl.BlockSpec((N_SLABS, CONV_HALO, LANES),
                                lambda b, i: (0, jnp.maximum((b * nt + i) * blocks_per_tile - 1, 0), 0))]
        kernel = _merge_prompt_kernel
        scratch = [pltpu.VMEM((N_SLABS, CONV_HALO + tm, LANES), F32), pltpu.VMEM((tm, CONV_WIDTH), F32)]
    dww, dwb, lng, lnb, wpa, wpb, bpb, wo = weights
    in_specs = ([pl.BlockSpec((tm, D_MODEL), row), pl.BlockSpec((tm, ATT_WIDTH), row)] + u_specs
                + [pl.BlockSpec((tm, D_MODEL), row), pl.BlockSpec((tm, D_MODEL), row), mod_spec,
                   pl.BlockSpec(dww.shape, const3),
                   pl.BlockSpec((1, CONV_WIDTH), const2), pl.BlockSpec((1, CONV_WIDTH), const2),
                   pl.BlockSpec((1, CONV_WIDTH), const2),
                   pl.BlockSpec((ATT_WIDTH, D_MODEL), const2, **once),
                   pl.BlockSpec((CONV_WIDTH, D_MODEL), const2, **once),
                   pl.BlockSpec((1, D_MODEL), const2),
                   pl.BlockSpec((D_MODEL, D_MODEL), const2, **once)])
    return pl.pallas_call(
        kernel,
        out_shape=jax.ShapeDtypeStruct((n, D_MODEL), F32),
        grid=(nb, nt),
        in_specs=in_specs,
        out_specs=pl.BlockSpec((tm, D_MODEL), row),
        scratch_shapes=scratch,
        compiler_params=pltpu.CompilerParams(dimension_semantics=("arbitrary", "arbitrary"),
                                             vmem_limit_bytes=VMEM_LIMIT),
        name="merge_sample" if sample else "merge_prompt",
    )(x, o, *u_args, ga, gb, g1, dww, dwb, lng, lnb, wpa, wpb, bpb, wo)


def _ffn_kernel(x_ref, sh_ref, sc_ref, g2_ref, rg_ref, fg_ref, win_ref, wout_ref, out_ref, *, chunk):
    x = x_ref[...]
    ms = jnp.mean(x * x, axis=-1, keepdims=True)
    h = x * lax.rsqrt(ms + EPS) * rg_ref[...]
    hb = (h * (1.0 + sc_ref[0]) + sh_ref[0]).astype(BF16)
    acc = jnp.zeros(x.shape, F32)
    for lo in range(0, FFN_HIDDEN, chunk):
        gate = jnp.dot(hb, win_ref[:, lo:lo + chunk], preferred_element_type=F32)
        up = jnp.dot(hb, win_ref[:, FFN_HIDDEN + lo:FFN_HIDDEN + lo + chunk], preferred_element_type=F32)
        act = (_silu(gate) * up).astype(BF16)
        acc = acc + jnp.dot(act, wout_ref[lo:lo + chunk, :], preferred_element_type=F32)
    x2 = x + g2_ref[0] * acc
    ms2 = jnp.mean(x2 * x2, axis=-1, keepdims=True)
    out_ref[...] = x2 * lax.rsqrt(ms2 + EPS) * fg_ref[...]


def _ffn(x, sh, sc, g2, rms_g, final_g, w_in, w_out, *, nb, tm, name):
    n = x.shape[0]
    nt = n // (nb * tm)
    row = lambda b, i: (b * nt + i, 0)
    const2 = lambda b, i: (0, 0)
    once = dict(pipeline_mode=pl.Buffered(1))
    mod_spec = pl.BlockSpec((1, sh.shape[1], D_MODEL), lambda b, i: (b, 0, 0))
    return pl.pallas_call(
        functools.partial(_ffn_kernel, chunk=FFN_HIDDEN // 2),
        out_shape=jax.ShapeDtypeStruct((n, D_MODEL), F32),
        grid=(nb, nt),
        in_specs=[pl.BlockSpec((tm, D_MODEL), row), mod_spec, mod_spec, mod_spec,
                  pl.BlockSpec((1, D_MODEL), const2), pl.BlockSpec((1, D_MODEL), const2),
                  pl.BlockSpec((D_MODEL, 2 * FFN_HIDDEN), const2, **once),
                  pl.BlockSpec((FFN_HIDDEN, D_MODEL), const2, **once)],
        out_specs=pl.BlockSpec((tm, D_MODEL), row),
        compiler_params=pltpu.CompilerParams(dimension_semantics=("arbitrary", "arbitrary"),
                                             vmem_limit_bytes=VMEM_LIMIT),
        name=name,
    )(x, sh, sc, g2, rms_g, final_g, w_in, w_out)


def _bias_placement():
    place = np.zeros((LANES, 2 * ATT_WIDTH), np.float32)
    ones = np.zeros((1, 2 * ATT_WIDTH), np.float32)
    for h in range(N_HEADS):
        base = (h // 2) * LANES + (h % 2) * HEAD_DIM
        for piece in range(AUG):
            place[piece * N_HEADS + h, base + piece] = 1.0
            ones[0, base + AUG + piece] = 1.0
            ones[0, ATT_WIDTH + base + piece] = 1.0
            place[piece * N_HEADS + h, ATT_WIDTH + base + AUG + piece] = -1.0
    return jnp.asarray(place, BF16), jnp.asarray(ones, F32)


def _lower_tri(n, block):
    t = np.arange(n)[:, None]
    s = np.arange(n)[None, :]
    return jnp.asarray(((s <= t) & (t // block == s // block)).astype(np.float32), BF16)


def _strict_upper(n):
    r = np.arange(n)[:, None]
    j = np.arange(n)[None, :]
    return jnp.asarray((j > r).astype(np.float32), BF16)


def kernel(x_prompt, x_sample, c_prompt, c_sample, cache_k, cache_v, cache_logf, state_conv, page_table, rms1_g, rms2_g, w_ada, b_ada, w_in, b_in, dw_w, dw_b, ln_g, ln_b, w_pa, w_pb, b_pb, w_o, w_ffn_in, w_ffn_out, final_g):
    nb, seq, _ = x_prompt.shape
    nseq, dec_t, _ = x_sample.shape
    depth = w_in.shape[0]
    assert depth == 1 and dec_t <= SAMPLE_T
    n_prompt = nb * seq
    tm = 512
    n_sample = nseq * SAMPLE_T

    w, b = w_in[0], b_in[0]
    f_off = 3 * ATT_WIDTH
    g_off = f_off + N_HEADS
    zpad = jnp.zeros((D_MODEL, LANES - N_HEADS), F32)
    w_all = jnp.concatenate([w[:, :f_off], w[:, f_off:g_off], zpad, w[:, g_off:]], axis=1).astype(BF16)
    b_all = jnp.concatenate([b[:f_off], b[f_off:g_off], jnp.zeros((LANES - N_HEADS,), F32), b[g_off:]])[None, :]
    dww = jnp.pad(dw_w[0], ((0, CONV_HALO - CONV_K), (0, 0))).reshape(CONV_HALO, N_SLABS, LANES).transpose(1, 0, 2)
    merge_w = (dww, dw_b[0][None, :], ln_g[0][None, :], ln_b[0][None, :], w_pa[0].astype(BF16),
               w_pb[0].astype(BF16), b_pb[0][None, :], w_o[0].astype(BF16))
    wf_in, wf_out = w_ffn_in[0].astype(BF16), w_ffn_out[0].astype(BF16)
    g1w, g2w, gfw = rms1_g[0][None, :], rms2_g[0][None, :], final_g[None, :]
    place, ones_row = _bias_placement()

    n_cond = nb + nseq
    c_all = jnp.pad(jnp.concatenate([c_prompt, c_sample], axis=0), ((0, -n_cond % 8), (0, 0)))
    mod = _modulation(c_all, w_ada[0], b_ada[0][None, :])
    mod_p = mod[:nb].reshape(nb, 1, 6, D_MODEL)
    sh1_p, sc1_p, g1_p, sh2_p, sc2_p, g2_p = (mod_p[:, :, i] for i in range(6))
    mod_s = jnp.repeat(mod[nb:n_cond], SAMPLE_T, axis=0).reshape(1, n_sample, 6, D_MODEL)
    sh1_s, sc1_s, g1_s, sh2_s, sc2_s, g2_s = (mod_s[:, :, i] for i in range(6))

    xp = x_prompt.reshape(n_prompt, D_MODEL)
    (qa, ka, k_p, v_p, vb_p, lf_p, _, u_p, ga_p, gb_p) = _inproj(
        xp, sh1_p, sc1_p, g1w, w_all, b_all, _lower_tri(tm, tm), place, ones_row, nb=nb, tm=tm, sample=False)
    o_p = _prompt_attention(qa, ka, vb_p, nb=nb, seq=seq, tq=tm)
    x1_p = _merge(xp, o_p, (u_p,), ga_p, gb_p, g1_p, merge_w, nb=nb, tm=tm, sample=False)
    y_p = _ffn(x1_p, sh2_p, sc2_p, g2_p, g2w, gfw, wf_in, wf_out, nb=nb, tm=tm, name="ffn_prompt")

    xs = jnp.pad(x_sample, ((0, 0), (0, SAMPLE_T - dec_t), (0, 0))).reshape(n_sample, D_MODEL)
    (_, _, k_s, v_s, _, lf_s, c_s, u_s, ga_s, gb_s, q_s) = _inproj(
        xs, sh1_s, sc1_s, g1w, w_all, b_all, _lower_tri(n_sample, SAMPLE_T), place, ones_row,
        nb=1, tm=n_sample, sample=True)
    n_phys = cache_k.shape[1]
    o_s = _sample_attention(
        page_table, q_s.reshape(nseq, SAMPLE_T, ATT_WIDTH), k_s.reshape(nseq, SAMPLE_T, ATT_WIDTH),
        v_s.reshape(nseq, SAMPLE_T, ATT_WIDTH), c_s.reshape(nseq, SAMPLE_T, LANES), _strict_upper(PAGE_SIZE),
        cache_k[0].reshape(n_phys, PAGE_SIZE, ATT_WIDTH), cache_v[0].reshape(n_phys, PAGE_SIZE, ATT_WIDTH),
        cache_logf[0])
    state_slabs = state_conv[0].reshape(nseq, CONV_K - 1, N_SLABS, LANES).transpose(2, 0, 1, 3)
    u_slabs = u_s.reshape(N_SLABS, nseq, SAMPLE_T, LANES)
    uwin = jnp.concatenate(
        [state_slabs, u_slabs, jnp.zeros((N_SLABS, nseq, SAMPLE_WIN - (CONV_K - 1) - SAMPLE_T, LANES), F32)], axis=2)
    x1_s = _merge(xs, o_s.reshape(n_sample, ATT_WIDTH), (uwin,), ga_s, gb_s, g1_s, merge_w,
                  nb=1, tm=n_sample, sample=True)
    y_s = _ffn(x1_s, sh2_s, sc2_s, g2_s, g2w, gfw, wf_in, wf_out, nb=1, tm=n_sample, name="ffn_sample")

    n_pg = seq // PAGE_SIZE
    tail = CONV_K - 1
    u_tail = u_p.reshape(N_SLABS, nb, seq, LANES)[:, :, seq - tail:]
    u_tail = u_tail.transpose(1, 2, 0, 3).reshape(nb, tail, CONV_WIDTH)
    us_rows = u_s.reshape(N_SLABS, nseq, SAMPLE_T, LANES)[:, :, :dec_t]
    us_rows = us_rows.transpose(1, 2, 0, 3).reshape(nseq, dec_t, CONV_WIDTH)
    unpad = lambda a, width: a.reshape(nseq, SAMPLE_T, *width)[:, :dec_t]
    return (y_p.reshape(nb, seq, D_MODEL),
            unpad(y_s, (D_MODEL,)),
            k_p.reshape(1, nb, n_pg, PAGE_SIZE, N_HEADS, HEAD_DIM),
            v_p.reshape(1, nb, n_pg, PAGE_SIZE, N_HEADS, HEAD_DIM),
            lf_p.reshape(1, nb, n_pg, PAGE_SIZE, N_HEADS),
            u_tail[None],
            unpad(k_s, (N_HEADS, HEAD_DIM))[None],
            unpad(v_s, (N_HEADS, HEAD_DIM))[None],
            unpad(lf_s, (N_HEADS,))[None],
            jnp.concatenate([state_conv[0][:, dec_t:], us_rows], axis=1)[None])
```

```python
import functools

import numpy as np
import jax
import jax.numpy as jnp
from jax import lax
from jax.experimental import pallas as pl
from jax.experimental.pallas import tpu as pltpu

F32 = jnp.float32
BF16 = jnp.bfloat16

D_MODEL = 1024
N_HEADS = 8
HEAD_DIM = 64
ATT_WIDTH = N_HEADS * HEAD_DIM
CONV_WIDTH = 512
CONV_K = 31
FFN_HIDDEN = 2816
PAGE_SIZE = 128
EPS = 1e-6
NEG_INF = -1e30
SCALE = HEAD_DIM ** -0.5

LANES = 128
N_PAIRS = N_HEADS // 2
N_SLABS = CONV_WIDTH // LANES
AUG = 3
SAMPLE_T = 8
CONV_HALO = 32
SAMPLE_WIN = 40
PAGES_PER_STEP = 8
VMEM_LIMIT = 56 * 1024 * 1024

C_Q, C_K, C_V, C_F, C_GLA, C_GLB, C_GA, C_GB, C_END = 0, 512, 1024, 1536, 1664, 2176, 2688, 3712, 4736


def _sigmoid(x):
    return 1.0 / (1.0 + jnp.exp(-x))


def _silu(x):
    return x * _sigmoid(x)


def _split3_packed(a, lane):
    a = jnp.where(lane < N_HEADS, a, 0.0)
    hi = a.astype(BF16).astype(F32)
    r1 = a - hi
    mid = r1.astype(BF16).astype(F32)
    lo = (r1 - mid).astype(BF16).astype(F32)
    packed = hi + pltpu.roll(mid, N_HEADS, axis=1) + pltpu.roll(lo, 2 * N_HEADS, axis=1)
    return packed.astype(BF16)


def _unpack3(p, lane):
    s = p + pltpu.roll(p, LANES - N_HEADS, axis=1) + pltpu.roll(p, LANES - 2 * N_HEADS, axis=1)
    return jnp.where(lane < N_HEADS, s, 0.0)


def _mod_kernel(c_ref, w_ref, b_ref, o_ref):
    s = _silu(c_ref[...]).astype(BF16)
    o_ref[...] = jnp.dot(s, w_ref[...].astype(BF16), preferred_element_type=F32) + b_ref[...]


def _modulation(c_all, w_ada, b_ada):
    rows = c_all.shape[0]
    n = w_ada.shape[1]
    tn = 768
    return pl.pallas_call(
        _mod_kernel,
        out_shape=jax.ShapeDtypeStruct((rows, n), F32),
        grid=(n // tn,),
        in_specs=[pl.BlockSpec((rows, D_MODEL), lambda j: (0, 0)),
                  pl.BlockSpec((D_MODEL, tn), lambda j: (0, j)),
                  pl.BlockSpec((1, tn), lambda j: (0, j))],
        out_specs=pl.BlockSpec((rows, tn), lambda j: (0, j)),
        compiler_params=pltpu.CompilerParams(dimension_semantics=("arbitrary",), vmem_limit_bytes=VMEM_LIMIT),
        name="modulation",
    )(c_all, w_ada, b_ada)


def _inproj_kernel(x_ref, sh_ref, sc_ref, g_ref, w_ref, b_ref, tri_ref, place_ref, ones_ref, *refs, sample):
    if sample:
        qp_ref, k_ref, v_ref, lf_ref, c_ref, u_ref, ga_ref, gb_ref, carry_ref = refs
    else:
        qa_ref, ka_ref, kt_ref, vt_ref, vb_ref, lft_ref, u_ref, ga_ref, gb_ref, carry_ref = refs
    tm = x_ref.shape[0]

    @pl.when(pl.program_id(1) == 0)
    def _():
        carry_ref[...] = jnp.zeros_like(carry_ref)

    x = x_ref[...]
    ms = jnp.mean(x * x, axis=-1, keepdims=True)
    h = x * lax.rsqrt(ms + EPS) * g_ref[...]
    h = h * (1.0 + sc_ref[0]) + sh_ref[0]
    hb = h.astype(BF16)

    def seg(lo, hi):
        return jnp.dot(hb, w_ref[:, lo:hi], preferred_element_type=F32) + b_ref[:, lo:hi]

    lane = lax.broadcasted_iota(jnp.int32, (tm, LANES), 1)
    zf = seg(C_F, C_GLA)
    lf = jnp.minimum(zf, 0.0) - jnp.log1p(jnp.exp(-jnp.abs(zf)))
    lf = jnp.where(lane < N_HEADS, lf, 0.0)
    csum = _unpack3(jnp.dot(tri_ref[...], _split3_packed(lf, lane), preferred_element_type=F32), lane)
    csum = csum + carry_ref[...]
    carry_ref[...] = csum[tm - 1:tm, :]

    zq = seg(C_Q, C_K) * SCALE
    zk = seg(C_K, C_V)
    zv = seg(C_V, C_F)
    if sample:
        qp_ref[...] = zq.astype(BF16)
        k_ref[...] = zk
        v_ref[...] = zv
        lf_ref[...] = lf[:, :N_HEADS]
        c_ref[...] = csum
    else:
        for pg in range(tm // PAGE_SIZE):
            rows = slice(pg * PAGE_SIZE, (pg + 1) * PAGE_SIZE)
            kt_ref[pg] = zk[rows, :].T
            vt_ref[pg] = zv[rows, :].T
            lft_ref[pg] = lf[rows, :].T[:N_HEADS, :]
        vb_ref[...] = zv.astype(BF16)
        aug = jnp.dot(_split3_packed(csum, lane), place_ref[...], preferred_element_type=F32) + ones_ref[...]
        for p in range(N_PAIRS):
            lo, hi = p * LANES, (p + 1) * LANES
            qa_ref[:, 2 * lo:2 * lo + LANES] = zq[:, lo:hi].astype(BF16)
            qa_ref[:, 2 * lo + LANES:2 * hi] = aug[:, lo:hi].astype(BF16)
            ka_ref[:, 2 * lo:2 * lo + LANES] = zk[:, lo:hi].astype(BF16)
            ka_ref[:, 2 * lo + LANES:2 * hi] = aug[:, ATT_WIDTH + lo:ATT_WIDTH + hi].astype(BF16)

    u = seg(C_GLA, C_GLB) * _sigmoid(seg(C_GLB, C_GA))
    for c in range(N_SLABS):
        u_ref[c] = u[:, c * LANES:(c + 1) * LANES]
    ga_ref[...] = _sigmoid(seg(C_GA, C_GB)).astype(BF16)
    gb_ref[...] = _sigmoid(seg(C_GB, C_END)).astype(BF16)


def _inproj(x, sh, sc, g1, w_all, b_all, tri, place, ones_row, *, nb, tm, sample):
    n = x.shape[0]
    nt = n // (nb * tm)
    row = lambda b, i: (b * nt + i, 0)
    const2 = lambda b, i: (0, 0)
    mod_rows = sh.shape[1]
    mod_spec = pl.BlockSpec((1, mod_rows, D_MODEL), lambda b, i: (b, 0, 0))
    once = dict(pipeline_mode=pl.Buffered(1))
    in_specs = [pl.BlockSpec((tm, D_MODEL), row), mod_spec, mod_spec,
                pl.BlockSpec((1, D_MODEL), const2),
                pl.BlockSpec((D_MODEL, C_END), const2, **once),
                pl.BlockSpec((1, C_END), const2),
                pl.BlockSpec((tm, tm), const2, **once),
                pl.BlockSpec((LANES, 2 * ATT_WIDTH), const2, **once),
                pl.BlockSpec((1, 2 * ATT_WIDTH), const2)]
    shared_shape = [jax.ShapeDtypeStruct((N_SLABS, n, LANES), F32),
                    jax.ShapeDtypeStruct((n, D_MODEL), BF16),
                    jax.ShapeDtypeStruct((n, D_MODEL), BF16)]
    shared_specs = [pl.BlockSpec((N_SLABS, tm, LANES), lambda b, i: (0, b * nt + i, 0)),
                    pl.BlockSpec((tm, D_MODEL), row), pl.BlockSpec((tm, D_MODEL), row)]
    if sample:
        out_shape = [jax.ShapeDtypeStruct((n, ATT_WIDTH), BF16),
                     jax.ShapeDtypeStruct((n, ATT_WIDTH), F32),
                     jax.ShapeDtypeStruct((n, ATT_WIDTH), F32),
                     jax.ShapeDtypeStruct((n, N_HEADS), F32),
                     jax.ShapeDtypeStruct((n, LANES), F32)]
        out_specs = [pl.BlockSpec((tm, ATT_WIDTH), row), pl.BlockSpec((tm, ATT_WIDTH), row),
                     pl.BlockSpec((tm, ATT_WIDTH), row), pl.BlockSpec((tm, N_HEADS), row),
                     pl.BlockSpec((tm, LANES), row)]
    else:
        n_pg, pg_tile = n // PAGE_SIZE, tm // PAGE_SIZE
        page = lambda b, i: (b * nt + i, 0, 0)
        out_shape = [jax.ShapeDtypeStruct((n, 2 * ATT_WIDTH), BF16),
                     jax.ShapeDtypeStruct((n, 2 * ATT_WIDTH), BF16),
                     jax.ShapeDtypeStruct((n_pg, ATT_WIDTH, PAGE_SIZE), F32),
                     jax.ShapeDtypeStruct((n_pg, ATT_WIDTH, PAGE_SIZE), F32),
                     jax.ShapeDtypeStruct((n, ATT_WIDTH), BF16),
                     jax.ShapeDtypeStruct((n_pg, N_HEADS, PAGE_SIZE), F32)]
        out_specs = [pl.BlockSpec((tm, 2 * ATT_WIDTH), row), pl.BlockSpec((tm, 2 * ATT_WIDTH), row),
                     pl.BlockSpec((pg_tile, ATT_WIDTH, PAGE_SIZE), page),
                     pl.BlockSpec((pg_tile, ATT_WIDTH, PAGE_SIZE), page),
                     pl.BlockSpec((tm, ATT_WIDTH), row),
                     pl.BlockSpec((pg_tile, N_HEADS, PAGE_SIZE), page)]
    out_shape += shared_shape
    out_specs += shared_specs
    return pl.pallas_call(
        functools.partial(_inproj_kernel, sample=sample),
        out_shape=out_shape,
        grid=(nb, nt),
        in_specs=in_specs,
        out_specs=out_specs,
        scratch_shapes=[pltpu.VMEM((1, LANES), F32)],
        compiler_params=pltpu.CompilerParams(dimension_semantics=("arbitrary", "arbitrary"),
                                             vmem_limit_bytes=VMEM_LIMIT),
        name="inproj_sample" if sample else "inproj_prompt",
    )(x, sh, sc, g1, w_all, b_all, tri, place, ones_row)


def _attn_kernel(qa_ref, ka_ref, v_ref, o_ref, *, tq):
    qi = pl.program_id(2)
    qa = qa_ref[...]
    qlane = lax.broadcasted_iota(jnp.int32, qa.shape, 1)
    q_heads = [jnp.where((qlane // HEAD_DIM) % 2 == hh, qa, jnp.zeros_like(qa)) for hh in range(2)]
    vlane = lax.broadcasted_iota(jnp.int32, (tq, LANES), 1)
    first_head = vlane < HEAD_DIM
    row = lax.broadcasted_iota(jnp.int32, (tq, tq), 0)
    col = lax.broadcasted_iota(jnp.int32, (tq, tq), 1)

    def tile(ki, carry, diagonal):
        m0, l0, m1, l1, acc = carry
        start = pl.multiple_of(ki * tq, tq)
        ka = ka_ref[pl.ds(start, tq), :]
        v2 = v_ref[pl.ds(start, tq), :]
        stats = []
        pv = []
        for hh, (m_prev, l_prev) in enumerate(((m0, l0), (m1, l1))):
            s = lax.dot_general(q_heads[hh], ka, (((1,), (1,)), ((), ())), preferred_element_type=F32)
            if diagonal:
                s = jnp.where(col <= row, s, NEG_INF)
            m_new = jnp.maximum(m_prev, jnp.max(s, axis=-1, keepdims=True))
            alpha = jnp.exp(m_prev - m_new)
            p = jnp.exp(s - m_new)
            l_new = alpha * l_prev + jnp.sum(p, axis=-1, keepdims=True)
            vh = jnp.where(first_head if hh == 0 else ~first_head, v2, jnp.zeros_like(v2))
            pv.append(jnp.dot(p.astype(BF16), vh, preferred_element_type=F32))
            stats.append((m_new, l_new, alpha))
        alpha2 = jnp.where(first_head, stats[0][2], stats[1][2])
        acc = acc * alpha2 + pv[0] + pv[1]
        return stats[0][0], stats[0][1], stats[1][0], stats[1][1], acc

    neg = jnp.full((tq, 1), NEG_INF, F32)
    zero = jnp.zeros((tq, 1), F32)
    init = (neg, zero, neg, zero, jnp.zeros((tq, LANES), F32))
    carry = lax.fori_loop(0, qi, lambda ki, c: tile(ki, c, False), init)
    _, l0, _, l1, acc = tile(qi, carry, True)
    o_ref[...] = (acc / jnp.where(first_head, l0, l1)).astype(o_ref.dtype)


def _prompt_attention(qa, ka, vb, *, nb, seq, tq):
    qa3 = qa.reshape(nb, seq, 2 * ATT_WIDTH)
    ka3 = ka.reshape(nb, seq, 2 * ATT_WIDTH)
    vb3 = vb.reshape(nb, seq, ATT_WIDTH)
    out = pl.pallas_call(
        functools.partial(_attn_kernel, tq=tq),
        out_shape=jax.ShapeDtypeStruct((nb, seq, ATT_WIDTH), BF16),
        grid=(nb, N_PAIRS, seq // tq),
        in_specs=[pl.BlockSpec((None, tq, 2 * LANES), lambda b, p, i: (b, i, p)),
                  pl.BlockSpec((None, seq, 2 * LANES), lambda b, p, i: (b, 0, p)),
                  pl.BlockSpec((None, seq, LANES), lambda b, p, i: (b, 0, p))],
        out_specs=pl.BlockSpec((None, tq, LANES), lambda b, p, i: (b, i, p)),
        compiler_params=pltpu.CompilerParams(dimension_semantics=("arbitrary", "arbitrary", "arbitrary"),
                                             vmem_limit_bytes=VMEM_LIMIT),
        name="prompt_attention",
    )(qa3, ka3, vb3)
    return out.reshape(nb * seq, ATT_WIDTH)


def _sample_attn_kernel(pt_ref, q_ref, kn_ref, vn_ref, cn_ref, tri_ref, *rest):
    del pt_ref
    npg = PAGES_PER_STEP
    k_refs = rest[:npg]
    v_refs = rest[npg:2 * npg]
    f_refs = rest[2 * npg:3 * npg]
    o_ref, q_scr, m_scr, l_scr, acc_scr, carry_scr = rest[3 * npg:]
    step = pl.program_id(1)
    rows = N_HEADS * SAMPLE_T
    nt_dims = (((1,), (1,)), ((), ()))

    def per_head_rows(a):
        return jnp.concatenate([jnp.broadcast_to(a[h:h + 1, :], (SAMPLE_T, a.shape[1])) for h in range(N_HEADS)],
                               axis=0)

    def update(s, pv):
        m_prev = m_scr[...]
        m_new = jnp.maximum(m_prev, jnp.max(s, axis=-1, keepdims=True))
        alpha = jnp.exp(m_prev - m_new)
        p = jnp.exp(s - m_new)
        l_scr[...] = alpha * l_scr[...] + jnp.sum(p, axis=-1, keepdims=True)
        acc_scr[...] = alpha * acc_scr[...] + pv(p.astype(BF16))
        m_scr[...] = m_new

    @pl.when(step == 0)
    def _():
        q8 = q_ref[...].astype(F32)
        qt = jnp.concatenate([q8] * N_HEADS, axis=0)
        r_i = lax.broadcasted_iota(jnp.int32, (rows, ATT_WIDTH), 0)
        c_i = lax.broadcasted_iota(jnp.int32, (rows, ATT_WIDTH), 1)
        q_scr[...] = jnp.where(r_i // SAMPLE_T == c_i // HEAD_DIM, qt, 0.0).astype(BF16)
        m_scr[...] = jnp.full_like(m_scr, NEG_INF)
        l_scr[...] = jnp.zeros_like(l_scr)
        acc_scr[...] = jnp.zeros_like(acc_scr)
        carry_scr[...] = jnp.zeros_like(carry_scr)
        pad = jnp.zeros((PAGE_SIZE - SAMPLE_T, ATT_WIDTH), F32)
        k_new = jnp.concatenate([kn_ref[...], pad], axis=0).astype(BF16)
        v_new = jnp.concatenate([vn_ref[...], pad], axis=0).astype(BF16)
        c_new = jnp.concatenate([cn_ref[...], jnp.zeros((PAGE_SIZE - SAMPLE_T, LANES), F32)], axis=0)
        s = lax.dot_general(q_scr[...], k_new, nt_dims, preferred_element_type=F32)
        s = s - per_head_rows(c_new.T[:N_HEADS, :])
        r_s = lax.broadcasted_iota(jnp.int32, (rows, PAGE_SIZE), 0)
        c_s = lax.broadcasted_iota(jnp.int32, (rows, PAGE_SIZE), 1)
        update(jnp.where(c_s <= r_s % SAMPLE_T, s, NEG_INF),
               lambda p: jnp.dot(p, v_new, preferred_element_type=F32))

    pieces = []
    for i in range(npg):
        lf = f_refs[i][...]
        hi = lf.astype(BF16).astype(F32)
        mid = (lf - hi).astype(BF16).astype(F32)
        lo = (lf - hi - mid).astype(BF16).astype(F32)
        pieces += [hi, mid, lo]
    sums = jnp.dot(jnp.concatenate(pieces, axis=0).astype(BF16), tri_ref[...], preferred_element_type=F32)
    carry = carry_scr[...]
    bias = []
    for i in range(npg):
        part = [sums[(3 * i + j) * N_HEADS:(3 * i + j + 1) * N_HEADS, :] for j in range(AUG)]
        local = part[0] + part[1] + part[2]
        bias.append(local[:, :PAGE_SIZE] + carry)
        carry = carry + local[:, PAGE_SIZE:]
    carry_scr[...] = carry
    kt = jnp.concatenate([k_refs[i][...].astype(BF16) for i in range(npg)], axis=1)
    vt = jnp.concatenate([v_refs[i][...].astype(BF16) for i in range(npg)], axis=1)
    s = jnp.dot(q_scr[...], kt, preferred_element_type=F32) + per_head_rows(jnp.concatenate(bias, axis=1))
    update(s, lambda p: lax.dot_general(p, vt, nt_dims, preferred_element_type=F32))

    @pl.when(step == pl.num_programs(1) - 1)
    def _():
        o = acc_scr[...] / l_scr[...]
        r_i = lax.broadcasted_iota(jnp.int32, (rows, ATT_WIDTH), 0)
        c_i = lax.broadcasted_iota(jnp.int32, (rows, ATT_WIDTH), 1)
        o = jnp.where(r_i // SAMPLE_T == c_i // HEAD_DIM, o, 0.0).astype(BF16)
        t_i = lax.broadcasted_iota(jnp.int32, (SAMPLE_T, rows), 0)
        r_j = lax.broadcasted_iota(jnp.int32, (SAMPLE_T, rows), 1)
        sel = jnp.where(r_j % SAMPLE_T == t_i, 1.0, 0.0).astype(BF16)
        o_ref[...] = jnp.dot(sel, o, preferred_element_type=F32).astype(o_ref.dtype)


def _sample_attention(page_table, q, k_new, v_new, c_new, tri_page, cache_k, cache_v, cache_logf):
    nseq, n_pages = page_table.shape
    npg = PAGES_PER_STEP
    steps = n_pages // npg
    rows = N_HEADS * SAMPLE_T
    seq_spec = lambda width: pl.BlockSpec((None, SAMPLE_T, width), lambda b, s, pt: (b, 0, 0))

    def page_spec(height, i):
        def index(b, s, pt):
            return (pt[b * n_pages + (n_pages - 1) - (s * npg + i)], 0, 0)
        return pl.BlockSpec((None, height, PAGE_SIZE), index)

    in_specs = ([seq_spec(ATT_WIDTH), seq_spec(ATT_WIDTH), seq_spec(ATT_WIDTH), seq_spec(LANES),
                 pl.BlockSpec((PAGE_SIZE, 2 * PAGE_SIZE), lambda b, s, pt: (0, 0))]
                + [page_spec(ATT_WIDTH, i) for i in range(npg)]
                + [page_spec(ATT_WIDTH, i) for i in range(npg)]
                + [page_spec(N_HEADS, i) for i in range(npg)])
    grid_spec = pltpu.PrefetchScalarGridSpec(
        num_scalar_prefetch=1,
        grid=(nseq, steps),
        in_specs=in_specs,
        out_specs=seq_spec(ATT_WIDTH),
        scratch_shapes=[pltpu.VMEM((rows, ATT_WIDTH), BF16),
                        pltpu.VMEM((rows, 1), F32), pltpu.VMEM((rows, 1), F32),
                        pltpu.VMEM((rows, ATT_WIDTH), F32),
                        pltpu.VMEM((N_HEADS, LANES), F32)])
    return pl.pallas_call(
        _sample_attn_kernel,
        out_shape=jax.ShapeDtypeStruct((nseq, SAMPLE_T, ATT_WIDTH), BF16),
        grid_spec=grid_spec,
        compiler_params=pltpu.CompilerParams(dimension_semantics=("arbitrary", "arbitrary"),
                                             vmem_limit_bytes=VMEM_LIMIT),
        name="sample_attention",
    )(page_table.reshape(-1), q, k_new, v_new, c_new, tri_page,
      *([cache_k] * npg), *([cache_v] * npg), *([cache_logf] * npg))


def _merge_tail(x, conv, o_ref, ga_ref, gb_ref, g1, dwb_ref, lng_ref, lnb_ref, wpa_ref, wpb_ref, bpb_ref, wo_ref):
    yb = conv + dwb_ref[...]
    mu = jnp.mean(yb, axis=-1, keepdims=True)
    var = jnp.mean(jnp.square(yb - mu), axis=-1, keepdims=True)
    yb = (yb - mu) * lax.rsqrt(var + EPS) * lng_ref[...] + lnb_ref[...]
    yb = jnp.dot(_silu(yb).astype(BF16), wpb_ref[...], preferred_element_type=F32) + bpb_ref[...]
    ya = jnp.dot(o_ref[...], wpa_ref[...], preferred_element_type=F32)
    m = ga_ref[...].astype(F32) * ya + gb_ref[...].astype(F32) * yb
    return x + g1 * jnp.dot(m.astype(BF16), wo_ref[...], preferred_element_type=F32)


def _merge_prompt_kernel(x_ref, o_ref, ucur_ref, uprev_ref, ga_ref, gb_ref, g1_ref, dww_ref, dwb_ref, lng_ref,
                         lnb_ref, wpa_ref, wpb_ref, bpb_ref, wo_ref, out_ref, ubuf, ybuf):
    tm = x_ref.shape[0]
    first = pl.program_id(1) == 0
    prev = uprev_ref[...]
    ubuf[:, 0:CONV_HALO, :] = jnp.where(first, jnp.zeros_like(prev), prev)
    ubuf[:, CONV_HALO:, :] = ucur_ref[...]
    chunk = 64
    for c in range(N_SLABS):
        for r0 in range(0, tm, chunk):
            acc = jnp.zeros((chunk, LANES), F32)
            for j in range(CONV_K):
                off = r0 + CONV_HALO - (CONV_K - 1) + j
                acc = acc + dww_ref[c, j:j + 1, :] * ubuf[c, off:off + chunk, :]
            ybuf[r0:r0 + chunk, c * LANES:(c + 1) * LANES] = acc
    out_ref[...] = _merge_tail(x_ref[...], ybuf[...], o_ref, ga_ref, gb_ref, g1_ref[0], dwb_ref, lng_ref, lnb_ref,
                               wpa_ref, wpb_ref, bpb_ref, wo_ref)


def _merge_sample_kernel(x_ref, o_ref, uwin_ref, ga_ref, gb_ref, g1_ref, dww_ref, dwb_ref, lng_ref,
                         lnb_ref, wpa_ref, wpb_ref, bpb_ref, wo_ref, out_ref):
    nseq = uwin_ref.shape[1]
    slabs = []
    for c in range(N_SLABS):
        acc = jnp.zeros((nseq, SAMPLE_T, LANES), F32)
        for j in range(CONV_K):
            acc = acc + dww_ref[c, j:j + 1, :] * uwin_ref[c, :, j:j + SAMPLE_T, :]
        slabs.append(acc.reshape(nseq * SAMPLE_T, LANES))
    conv = jnp.concatenate(slabs, axis=1)
    out_ref[...] = _merge_tail(x_ref[...], conv, o_ref, ga_ref, gb_ref, g1_ref[0], dwb_ref, lng_ref, lnb_ref,
                               wpa_ref, wpb_ref, bpb_ref, wo_ref)


def _merge(x, o, u_args, ga, gb, g1, weights, *, nb, tm, sample):
    n = x.shape[0]
    nt = n // (nb * tm)
    row = lambda b, i: (b * nt + i, 0)
    const2 = lambda b, i: (0, 0)
    const3 = lambda b, i: (0, 0, 0)
    once = dict(pipeline_mode=pl.Buffered(1))
    mod_spec = pl.BlockSpec((1, g1.shape[1], D_MODEL), lambda b, i: (b, 0, 0))
    if sample:
        (uwin,) = u_args
        u_specs = [pl.BlockSpec(uwin.shape, lambda b, i: (0, 0, 0, 0))]
        kernel, scratch = _merge_sample_kernel, []
    else:
        (u,) = u_args
        u_args = (u, u)
        blocks_per_tile = tm // CONV_HALO
        u_specs = [pl.BlockSpec((N_SLABS, tm, LANES), lambda b, i: (0, b * nt + i, 0)),
                   pl.BlockSpec((N_SLABS, CONV_HALO, LANES),
                                lambda b, i: (0, jnp.maximum((b * nt + i) * blocks_per_tile - 1, 0), 0))]
        kernel = _merge_prompt_kernel
        scratch = [pltpu.VMEM((N_SLABS, CONV_HALO + tm, LANES), F32), pltpu.VMEM((tm, CONV_WIDTH), F32)]
    dww, dwb, lng, lnb, wpa, wpb, bpb, wo = weights
    in_specs = ([pl.BlockSpec((tm, D_MODEL), row), pl.BlockSpec((tm, ATT_WIDTH), row)] + u_specs
                + [pl.BlockSpec((tm, D_MODEL), row), pl.BlockSpec((tm, D_MODEL), row), mod_spec,
                   pl.BlockSpec(dww.shape, const3),
                   pl.BlockSpec((1, CONV_WIDTH), const2), pl.BlockSpec((1, CONV_WIDTH), const2),
                   pl.BlockSpec((1, CONV_WIDTH), const2),
                   pl.BlockSpec((ATT_WIDTH, D_MODEL), const2, **once),
                   pl.BlockSpec((CONV_WIDTH, D_MODEL), const2, **once),
                   pl.BlockSpec((1, D_MODEL), const2),
                   pl.BlockSpec((D_MODEL, D_MODEL), const2, **once)])
    return pl.pallas_call(
        kernel,
        out_shape=jax.ShapeDtypeStruct((n, D_MODEL), F32),
        grid=(nb, nt),
        in_specs=in_specs,
        out_specs=pl.BlockSpec((tm, D_MODEL), row),
        scratch_shapes=scratch,
        compiler_params=pltpu.CompilerParams(dimension_semantics=("arbitrary", "arbitrary"),
                                             vmem_limit_bytes=VMEM_LIMIT),
        name="merge_sample" if sample else "merge_prompt",
    )(x, o, *u_args, ga, gb, g1, dww, dwb, lng, lnb, wpa, wpb, bpb, wo)


def _ffn_kernel(x_ref, sh_ref, sc_ref, g2_ref, rg_ref, fg_ref, win_ref, wout_ref, out_ref, *, chunk):
    x = x_ref[...]
    ms = jnp.mean(x * x, axis=-1, keepdims=True)
    h = x * lax.rsqrt(ms + EPS) * rg_ref[...]
    hb = (h * (1.0 + sc_ref[0]) + sh_ref[0]).astype(BF16)
    acc = jnp.zeros(x.shape, F32)
    for lo in range(0, FFN_HIDDEN, chunk):
        gate = jnp.dot(hb, win_ref[:, lo:lo + chunk], preferred_element_type=F32)
        up = jnp.dot(hb, win_ref[:, FFN_HIDDEN + lo:FFN_HIDDEN + lo + chunk], preferred_element_type=F32)
        act = (_silu(gate) * up).astype(BF16)
        acc = acc + jnp.dot(act, wout_ref[lo:lo + chunk, :], preferred_element_type=F32)
    x2 = x + g2_ref[0] * acc
    ms2 = jnp.mean(x2 * x2, axis=-1, keepdims=True)
    out_ref[...] = x2 * lax.rsqrt(ms2 + EPS) * fg_ref[...]


def _ffn(x, sh, sc, g2, rms_g, final_g, w_in, w_out, *, nb, tm, name):
    n = x.shape[0]
    nt = n // (nb * tm)
    row = lambda b, i: (b * nt + i, 0)
    const2 = lambda b, i: (0, 0)
    once = dict(pipeline_mode=pl.Buffered(1))
    mod_spec = pl.BlockSpec((1, sh.shape[1], D_MODEL), lambda b, i: (b, 0, 0))
    return pl.pallas_call(
        functools.partial(_ffn_kernel, chunk=FFN_HIDDEN // 2),
        out_shape=jax.ShapeDtypeStruct((n, D_MODEL), F32),
        grid=(nb, nt),
        in_specs=[pl.BlockSpec((tm, D_MODEL), row), mod_spec, mod_spec, mod_spec,
                  pl.BlockSpec((1, D_MODEL), const2), pl.BlockSpec((1, D_MODEL), const2),
                  pl.BlockSpec((D_MODEL, 2 * FFN_HIDDEN), const2, **once),
                  pl.BlockSpec((FFN_HIDDEN, D_MODEL), const2, **once)],
        out_specs=pl.BlockSpec((tm, D_MODEL), row),
        compiler_params=pltpu.CompilerParams(dimension_semantics=("arbitrary", "arbitrary"),
                                             vmem_limit_bytes=VMEM_LIMIT),
        name=name,
    )(x, sh, sc, g2, rms_g, final_g, w_in, w_out)


def _bias_placement():
    place = np.zeros((LANES, 2 * ATT_WIDTH), np.float32)
    ones = np.zeros((1, 2 * ATT_WIDTH), np.float32)
    for h in range(N_HEADS):
        base = (h // 2) * LANES + (h % 2) * HEAD_DIM
        for piece in range(AUG):
            place[piece * N_HEADS + h, base + piece] = 1.0
            ones[0, base + AUG + piece] = 1.0
            ones[0, ATT_WIDTH + base + piece] = 1.0
            place[piece * N_HEADS + h, ATT_WIDTH + base + AUG + piece] = -1.0
    return jnp.asarray(place, BF16), jnp.asarray(ones, F32)


def _lower_tri(n, block):
    t = np.arange(n)[:, None]
    s = np.arange(n)[None, :]
    return jnp.asarray(((s <= t) & (t // block == s // block)).astype(np.float32), BF16)


def _later_keys(n):
    j = np.arange(n)[:, None]
    s = np.arange(n)[None, :]
    return jnp.asarray(np.concatenate([(j > s).astype(np.float32), np.ones((n, n), np.float32)], axis=1), BF16)


def kernel(x_prompt, x_sample, c_prompt, c_sample, cache_k, cache_v, cache_logf, state_conv, page_table, rms1_g, rms2_g, w_ada, b_ada, w_in, b_in, dw_w, dw_b, ln_g, ln_b, w_pa, w_pb, b_pb, w_o, w_ffn_in, w_ffn_out, final_g):
    nb, seq, _ = x_prompt.shape
    nseq, dec_t, _ = x_sample.shape
    depth = w_in.shape[0]
    assert depth == 1 and dec_t <= SAMPLE_T
    n_prompt = nb * seq
    tm = 512
    n_sample = nseq * SAMPLE_T

    w, b = w_in[0], b_in[0]
    f_off = 3 * ATT_WIDTH
    g_off = f_off + N_HEADS
    zpad = jnp.zeros((D_MODEL, LANES - N_HEADS), F32)
    w_all = jnp.concatenate([w[:, :f_off], w[:, f_off:g_off], zpad, w[:, g_off:]], axis=1).astype(BF16)
    b_all = jnp.concatenate([b[:f_off], b[f_off:g_off], jnp.zeros((LANES - N_HEADS,), F32), b[g_off:]])[None, :]
    dww = jnp.pad(dw_w[0], ((0, CONV_HALO - CONV_K), (0, 0))).reshape(CONV_HALO, N_SLABS, LANES).transpose(1, 0, 2)
    merge_w = (dww, dw_b[0][None, :], ln_g[0][None, :], ln_b[0][None, :], w_pa[0].astype(BF16),
               w_pb[0].astype(BF16), b_pb[0][None, :], w_o[0].astype(BF16))
    wf_in, wf_out = w_ffn_in[0].astype(BF16), w_ffn_out[0].astype(BF16)
    g1w, g2w, gfw = rms1_g[0][None, :], rms2_g[0][None, :], final_g[None, :]
    place, ones_row = _bias_placement()

    n_cond = nb + nseq
    c_all = jnp.pad(jnp.concatenate([c_prompt, c_sample], axis=0), ((0, -n_cond % 8), (0, 0)))
    mod = _modulation(c_all, w_ada[0], b_ada[0][None, :])
    mod_p = mod[:nb].reshape(nb, 1, 6, D_MODEL)
    sh1_p, sc1_p, g1_p, sh2_p, sc2_p, g2_p = (mod_p[:, :, i] for i in range(6))
    mod_s = jnp.repeat(mod[nb:n_cond], SAMPLE_T, axis=0).reshape(1, n_sample, 6, D_MODEL)
    sh1_s, sc1_s, g1_s, sh2_s, sc2_s, g2_s = (mod_s[:, :, i] for i in range(6))

    xp = x_prompt.reshape(n_prompt, D_MODEL)
    (qa, ka, kt_p, vt_p, vb_p, lft_p, u_p, ga_p, gb_p) = _inproj(
        xp, sh1_p, sc1_p, g1w, w_all, b_all, _lower_tri(tm, tm), place, ones_row, nb=nb, tm=tm, sample=False)
    o_p = _prompt_attention(qa, ka, vb_p, nb=nb, seq=seq, tq=tm)
    x1_p = _merge(xp, o_p, (u_p,), ga_p, gb_p, g1_p, merge_w, nb=nb, tm=tm, sample=False)
    y_p = _ffn(x1_p, sh2_p, sc2_p, g2_p, g2w, gfw, wf_in, wf_out, nb=nb, tm=tm, name="ffn_prompt")

    xs = jnp.pad(x_sample, ((0, 0), (0, SAMPLE_T - dec_t), (0, 0))).reshape(n_sample, D_MODEL)
    (q_s, k_s, v_s, lf_s, c_s, u_s, ga_s, gb_s) = _inproj(
        xs, sh1_s, sc1_s, g1w, w_all, b_all, _lower_tri(n_sample, SAMPLE_T), place, ones_row,
        nb=1, tm=n_sample, sample=True)
    n_phys = cache_k.shape[1]
    page_t = lambda c: c[0].transpose(0, 2, 3, 1).reshape(n_phys, ATT_WIDTH, PAGE_SIZE)
    o_s = _sample_attention(
        page_table, q_s.reshape(nseq, SAMPLE_T, ATT_WIDTH), k_s.reshape(nseq, SAMPLE_T, ATT_WIDTH),
        v_s.reshape(nseq, SAMPLE_T, ATT_WIDTH), c_s.reshape(nseq, SAMPLE_T, LANES), _later_keys(PAGE_SIZE),
        page_t(cache_k), page_t(cache_v), cache_logf[0].transpose(0, 2, 1))
    state_slabs = state_conv[0].reshape(nseq, CONV_K - 1, N_SLABS, LANES).transpose(2, 0, 1, 3)
    u_slabs = u_s.reshape(N_SLABS, nseq, SAMPLE_T, LANES)
    uwin = jnp.concatenate(
        [state_slabs, u_slabs, jnp.zeros((N_SLABS, nseq, SAMPLE_WIN - (CONV_K - 1) - SAMPLE_T, LANES), F32)], axis=2)
    x1_s = _merge(xs, o_s.reshape(n_sample, ATT_WIDTH), (uwin,), ga_s, gb_s, g1_s, merge_w,
                  nb=1, tm=n_sample, sample=True)
    y_s = _ffn(x1_s, sh2_s, sc2_s, g2_s, g2w, gfw, wf_in, wf_out, nb=1, tm=n_sample, name="ffn_sample")

    n_pg = seq // PAGE_SIZE
    tail = CONV_K - 1
    u_tail = u_p.reshape(N_SLABS, nb, seq, LANES)[:, :, seq - tail:]
    u_tail = u_tail.transpose(1, 2, 0, 3).reshape(nb, tail, CONV_WIDTH)
    us_rows = u_s.reshape(N_SLABS, nseq, SAMPLE_T, LANES)[:, :, :dec_t]
    us_rows = us_rows.transpose(1, 2, 0, 3).reshape(nseq, dec_t, CONV_WIDTH)
    unpad = lambda a, width: a.reshape(nseq, SAMPLE_T, *width)[:, :dec_t]
    return (y_p.reshape(nb, seq, D_MODEL),
            unpad(y_s, (D_MODEL,)),
            kt_p.reshape(1, nb, n_pg, N_HEADS, HEAD_DIM, PAGE_SIZE).transpose(0, 1, 2, 5, 3, 4),
            vt_p.reshape(1, nb, n_pg, N_HEADS, HEAD_DIM, PAGE_SIZE).transpose(0, 1, 2, 5, 3, 4),
            lft_p.reshape(1, nb, n_pg, N_HEADS, PAGE_SIZE).transpose(0, 1, 2, 4, 3),
            u_tail[None],
            unpad(k_s, (N_HEADS, HEAD_DIM))[None],
            unpad(v_s, (N_HEADS, HEAD_DIM))[None],
            unpad(lf_s, (N_HEADS,))[None],
            jnp.concatenate([state_conv[0][:, dec_t:], us_rows], axis=1)[None])
```

```python
import functools

import numpy as np
import jax
import jax.numpy as jnp
from jax import lax
from jax.experimental import pallas as pl
from jax.experimental.pallas import tpu as pltpu

F32 = jnp.float32
BF16 = jnp.bfloat16

D_MODEL = 1024
N_HEADS = 8
HEAD_DIM = 64
ATT_WIDTH = N_HEADS * HEAD_DIM
CONV_WIDTH = 512
CONV_K = 31
FFN_HIDDEN = 2816
PAGE_SIZE = 128
EPS = 1e-6
NEG_INF = -1e30
SCALE = HEAD_DIM ** -0.5
LOG2E = 1.4426950408889634

LANES = 128
N_PAIRS = N_HEADS // 2
N_SLABS = CONV_WIDTH // LANES
AUG = 3
SAMPLE_T = 8
CONV_HALO = 32
SAMPLE_WIN = 40
PAGES_PER_STEP = 8
VMEM_LIMIT = 56 * 1024 * 1024

C_Q, C_K, C_V, C_F, C_GLA, C_GLB, C_GA, C_GB, C_END = 0, 512, 1024, 1536, 1664, 2176, 2688, 3712, 4736


def _sigmoid(x):
    return 1.0 / (1.0 + jnp.exp(-x))


def _silu(x):
    return x * _sigmoid(x)


def _split3_packed(a, lane):
    a = jnp.where(lane < N_HEADS, a, 0.0)
    hi = a.astype(BF16).astype(F32)
    r1 = a - hi
    mid = r1.astype(BF16).astype(F32)
    lo = (r1 - mid).astype(BF16).astype(F32)
    packed = hi + pltpu.roll(mid, N_HEADS, axis=1) + pltpu.roll(lo, 2 * N_HEADS, axis=1)
    return packed.astype(BF16)


def _unpack3(p, lane):
    s = p + pltpu.roll(p, LANES - N_HEADS, axis=1) + pltpu.roll(p, LANES - 2 * N_HEADS, axis=1)
    return jnp.where(lane < N_HEADS, s, 0.0)


def _mod_kernel(c_ref, w_ref, b_ref, o_ref):
    s = _silu(c_ref[...]).astype(BF16)
    o_ref[...] = jnp.dot(s, w_ref[...].astype(BF16), preferred_element_type=F32) + b_ref[...]


def _modulation(c_all, w_ada, b_ada):
    rows = c_all.shape[0]
    n = w_ada.shape[1]
    tn = 768
    return pl.pallas_call(
        _mod_kernel,
        out_shape=jax.ShapeDtypeStruct((rows, n), F32),
        grid=(n // tn,),
        in_specs=[pl.BlockSpec((rows, D_MODEL), lambda j: (0, 0)),
                  pl.BlockSpec((D_MODEL, tn), lambda j: (0, j)),
                  pl.BlockSpec((1, tn), lambda j: (0, j))],
        out_specs=pl.BlockSpec((rows, tn), lambda j: (0, j)),
        compiler_params=pltpu.CompilerParams(dimension_semantics=("arbitrary",), vmem_limit_bytes=VMEM_LIMIT),
        name="modulation",
    )(c_all, w_ada, b_ada)


def _inproj_kernel(x_ref, sh_ref, sc_ref, g_ref, w_ref, b_ref, tri_ref, place_ref, ones_ref, *refs, sample):
    if sample:
        qp_ref, k_ref, v_ref, lf_ref, c_ref, u_ref, ga_ref, gb_ref, carry_ref = refs
    else:
        qa_ref, ka_ref, kt_ref, vt_ref, vb_ref, lft_ref, u_ref, ga_ref, gb_ref, carry_ref = refs
    tm = x_ref.shape[0]

    @pl.when(pl.program_id(1) == 0)
    def _():
        carry_ref[...] = jnp.zeros_like(carry_ref)

    x = x_ref[...]
    ms = jnp.mean(x * x, axis=-1, keepdims=True)
    h = x * lax.rsqrt(ms + EPS) * g_ref[...]
    h = h * (1.0 + sc_ref[0]) + sh_ref[0]
    hb = h.astype(BF16)

    def seg(lo, hi):
        return jnp.dot(hb, w_ref[:, lo:hi], preferred_element_type=F32) + b_ref[:, lo:hi]

    lane = lax.broadcasted_iota(jnp.int32, (tm, LANES), 1)
    zf = seg(C_F, C_GLA)
    lf = jnp.minimum(zf, 0.0) - jnp.log1p(jnp.exp(-jnp.abs(zf)))
    lf = jnp.where(lane < N_HEADS, lf, 0.0)
    csum = _unpack3(jnp.dot(tri_ref[...], _split3_packed(lf, lane), preferred_element_type=F32), lane)
    csum = csum + carry_ref[...]
    carry_ref[...] = csum[tm - 1:tm, :]

    zq = seg(C_Q, C_K)
    zk = seg(C_K, C_V)
    zv = seg(C_V, C_F)
    if sample:
        qp_ref[...] = (zq * SCALE).astype(BF16)
        k_ref[...] = zk
        v_ref[...] = zv
        lf_ref[...] = lf[:, :N_HEADS]
        c_ref[...] = csum
    else:
        for pg in range(tm // PAGE_SIZE):
            rows = slice(pg * PAGE_SIZE, (pg + 1) * PAGE_SIZE)
            kt_ref[pg] = zk[rows, :].T
            vt_page = zv[rows, :].T
            vt_ref[pg] = vt_page
            vb_ref[:, rows] = vt_page.astype(BF16)
            lft_ref[pg] = lf[rows, :].T[:N_HEADS, :]
        zq = zq * (SCALE * LOG2E)
        aug = jnp.dot(_split3_packed(csum * LOG2E, lane), place_ref[...], preferred_element_type=F32) + ones_ref[...]
        for p in range(N_PAIRS):
            lo, hi = p * LANES, (p + 1) * LANES
            qa_ref[:, 2 * lo:2 * lo + LANES] = zq[:, lo:hi].astype(BF16)
            qa_ref[:, 2 * lo + LANES:2 * hi] = aug[:, lo:hi].astype(BF16)
            ka_ref[:, 2 * lo:2 * lo + LANES] = zk[:, lo:hi].astype(BF16)
            ka_ref[:, 2 * lo + LANES:2 * hi] = aug[:, ATT_WIDTH + lo:ATT_WIDTH + hi].astype(BF16)

    u = seg(C_GLA, C_GLB) * _sigmoid(seg(C_GLB, C_GA))
    for c in range(N_SLABS):
        u_ref[c] = u[:, c * LANES:(c + 1) * LANES]
    ga_ref[...] = _sigmoid(seg(C_GA, C_GB)).astype(BF16)
    gb_ref[...] = _sigmoid(seg(C_GB, C_END)).astype(BF16)


def _inproj(x, sh, sc, g1, w_all, b_all, tri, place, ones_row, *, nb, tm, sample):
    n = x.shape[0]
    nt = n // (nb * tm)
    row = lambda b, i: (b * nt + i, 0)
    const2 = lambda b, i: (0, 0)
    mod_rows = sh.shape[1]
    mod_spec = pl.BlockSpec((1, mod_rows, D_MODEL), lambda b, i: (b, 0, 0))
    once = dict(pipeline_mode=pl.Buffered(1))
    in_specs = [pl.BlockSpec((tm, D_MODEL), row), mod_spec, mod_spec,
                pl.BlockSpec((1, D_MODEL), const2),
                pl.BlockSpec((D_MODEL, C_END), const2, **once),
                pl.BlockSpec((1, C_END), const2),
                pl.BlockSpec((tm, tm), const2, **once),
                pl.BlockSpec((LANES, 2 * ATT_WIDTH), const2, **once),
                pl.BlockSpec((1, 2 * ATT_WIDTH), const2)]
    shared_shape = [jax.ShapeDtypeStruct((N_SLABS, n, LANES), F32),
                    jax.ShapeDtypeStruct((n, D_MODEL), BF16),
                    jax.ShapeDtypeStruct((n, D_MODEL), BF16)]
    shared_specs = [pl.BlockSpec((N_SLABS, tm, LANES), lambda b, i: (0, b * nt + i, 0)),
                    pl.BlockSpec((tm, D_MODEL), row), pl.BlockSpec((tm, D_MODEL), row)]
    if sample:
        out_shape = [jax.ShapeDtypeStruct((n, ATT_WIDTH), BF16),
                     jax.ShapeDtypeStruct((n, ATT_WIDTH), F32),
                     jax.ShapeDtypeStruct((n, ATT_WIDTH), F32),
                     jax.ShapeDtypeStruct((n, N_HEADS), F32),
                     jax.ShapeDtypeStruct((n, LANES), F32)]
        out_specs = [pl.BlockSpec((tm, ATT_WIDTH), row), pl.BlockSpec((tm, ATT_WIDTH), row),
                     pl.BlockSpec((tm, ATT_WIDTH), row), pl.BlockSpec((tm, N_HEADS), row),
                     pl.BlockSpec((tm, LANES), row)]
    else:
        n_pg, pg_tile = n // PAGE_SIZE, tm // PAGE_SIZE
        page = lambda b, i: (b * nt + i, 0, 0)
        out_shape = [jax.ShapeDtypeStruct((n, 2 * ATT_WIDTH), BF16),
                     jax.ShapeDtypeStruct((n, 2 * ATT_WIDTH), BF16),
                     jax.ShapeDtypeStruct((n_pg, ATT_WIDTH, PAGE_SIZE), F32),
                     jax.ShapeDtypeStruct((n_pg, ATT_WIDTH, PAGE_SIZE), F32),
                     jax.ShapeDtypeStruct((nb, ATT_WIDTH, n // nb), BF16),
                     jax.ShapeDtypeStruct((n_pg, N_HEADS, PAGE_SIZE), F32)]
        out_specs = [pl.BlockSpec((tm, 2 * ATT_WIDTH), row), pl.BlockSpec((tm, 2 * ATT_WIDTH), row),
                     pl.BlockSpec((pg_tile, ATT_WIDTH, PAGE_SIZE), page),
                     pl.BlockSpec((pg_tile, ATT_WIDTH, PAGE_SIZE), page),
                     pl.BlockSpec((None, ATT_WIDTH, tm), lambda b, i: (b, 0, i)),
                     pl.BlockSpec((pg_tile, N_HEADS, PAGE_SIZE), page)]
    out_shape += shared_shape
    out_specs += shared_specs
    return pl.pallas_call(
        functools.partial(_inproj_kernel, sample=sample),
        out_shape=out_shape,
        grid=(nb, nt),
        in_specs=in_specs,
        out_specs=out_specs,
        scratch_shapes=[pltpu.VMEM((1, LANES), F32)],
        compiler_params=pltpu.CompilerParams(dimension_semantics=("arbitrary", "arbitrary"),
                                             vmem_limit_bytes=VMEM_LIMIT),
        name="inproj_sample" if sample else "inproj_prompt",
    )(x, sh, sc, g1, w_all, b_all, tri, place, ones_row)


def _attn_kernel(qa_ref, ka_ref, vt_ref, o_ref, s_even, s_odd, mx_even, mx_odd, *, tq, tk):
    qi = pl.program_id(2)
    s_bufs = (s_even, s_odd)
    mx_bufs = (mx_even, mx_odd)
    qt = qa_ref[...].astype(F32).T
    feat = lax.broadcasted_iota(jnp.int32, qt.shape, 0)
    q_heads = [jnp.where((feat // HEAD_DIM) % 2 == hh, qt, 0.0).astype(BF16) for hh in range(2)]
    key = lax.broadcasted_iota(jnp.int32, (tk, tq), 0)
    qry = lax.broadcasted_iota(jnp.int32, (tk, tq), 1)

    def scores(ki, parity, diagonal):
        ka = ka_ref[pl.ds(pl.multiple_of(ki * tk, tk), tk), :]
        for hh in range(2):
            st = jnp.dot(ka, q_heads[hh], preferred_element_type=F32)
            if diagonal:
                st = jnp.where(key + parity * tk <= qry, st, NEG_INF)
            s_bufs[parity][hh] = st
            mx_bufs[parity][hh] = jnp.max(st, axis=0, keepdims=True)

    def softmax_pv(ki, parity, carry):
        start = pl.multiple_of(ki * tk, tk)
        out = []
        for hh in range(2):
            m_prev, l_prev, acc = carry[hh]
            m_new = jnp.maximum(m_prev, mx_bufs[parity][hh])
            alpha = jnp.exp2(m_prev - m_new)
            pt = jnp.exp2(s_bufs[parity][hh] - m_new)
            l_new = alpha * l_prev + jnp.sum(pt, axis=0, keepdims=True)
            vt = vt_ref[hh * HEAD_DIM:(hh + 1) * HEAD_DIM, pl.ds(start, tk)]
            acc = acc * alpha + jnp.dot(vt, pt.astype(BF16), preferred_element_type=F32)
            out.append((m_new, l_new, acc))
        return tuple(out)

    head0 = (jnp.full((1, tq), NEG_INF, F32), jnp.zeros((1, tq), F32), jnp.zeros((HEAD_DIM, tq), F32))
    scores(2 * qi, 0, True)
    scores(2 * qi + 1, 1, True)
    carry = softmax_pv(2 * qi, 0, (head0, head0))

    def pair(j, carry):
        scores(2 * j, 0, False)
        carry = softmax_pv(jnp.where(j == 0, 2 * qi + 1, 2 * j - 1), 1, carry)
        scores(2 * j + 1, 1, False)
        return softmax_pv(2 * j, 0, carry)

    carry = lax.fori_loop(0, qi, pair, carry)
    (_, l0, acc0), (_, l1, acc1) = softmax_pv(jnp.where(qi == 0, 1, 2 * qi - 1), 1, carry)
    o_ref[...] = jnp.concatenate([acc0 / l0, acc1 / l1], axis=0).T.astype(o_ref.dtype)


def _prompt_attention(qa, ka, vt, *, nb, seq, tq):
    tk = tq // 2
    qa3 = qa.reshape(nb, seq, 2 * ATT_WIDTH)
    ka3 = ka.reshape(nb, seq, 2 * ATT_WIDTH)
    out = pl.pallas_call(
        functools.partial(_attn_kernel, tq=tq, tk=tk),
        out_shape=jax.ShapeDtypeStruct((nb, seq, ATT_WIDTH), BF16),
        grid=(nb, N_PAIRS, seq // tq),
        in_specs=[pl.BlockSpec((None, tq, 2 * LANES), lambda b, p, i: (b, i, p)),
                  pl.BlockSpec((None, seq, 2 * LANES), lambda b, p, i: (b, 0, p)),
                  pl.BlockSpec((None, LANES, seq), lambda b, p, i: (b, p, 0))],
        out_specs=pl.BlockSpec((None, tq, LANES), lambda b, p, i: (b, i, p)),
        scratch_shapes=[pltpu.VMEM((2, tk, tq), F32)] * 2 + [pltpu.VMEM((2, 1, tq), F32)] * 2,
        compiler_params=pltpu.CompilerParams(dimension_semantics=("arbitrary", "arbitrary", "arbitrary"),
                                             vmem_limit_bytes=VMEM_LIMIT),
        name="prompt_attention",
    )(qa3, ka3, vt)
    return out.reshape(nb * seq, ATT_WIDTH)


def _sample_attn_kernel(pt_ref, q_ref, kn_ref, vn_ref, cn_ref, tri_ref, *rest):
    del pt_ref
    npg = PAGES_PER_STEP
    k_refs = rest[:npg]
    v_refs = rest[npg:2 * npg]
    f_refs = rest[2 * npg:3 * npg]
    o_ref, q_scr, m_scr, l_scr, acc_scr, carry_scr = rest[3 * npg:]
    step = pl.program_id(1)
    rows = N_HEADS * SAMPLE_T
    nt_dims = (((1,), (1,)), ((), ()))

    def per_head_rows(a):
        return jnp.concatenate([jnp.broadcast_to(a[h:h + 1, :], (SAMPLE_T, a.shape[1])) for h in range(N_HEADS)],
                               axis=0)

    def update(s, pv):
        m_prev = m_scr[...]
        m_new = jnp.maximum(m_prev, jnp.max(s, axis=-1, keepdims=True))
        alpha = jnp.exp(m_prev - m_new)
        p = jnp.exp(s - m_new)
        l_scr[...] = alpha * l_scr[...] + jnp.sum(p, axis=-1, keepdims=True)
        acc_scr[...] = alpha * acc_scr[...] + pv(p.astype(BF16))
        m_scr[...] = m_new

    @pl.when(step == 0)
    def _():
        q8 = q_ref[...].astype(F32)
        qt = jnp.concatenate([q8] * N_HEADS, axis=0)
        r_i = lax.broadcasted_iota(jnp.int32, (rows, ATT_WIDTH), 0)
        c_i = lax.broadcasted_iota(jnp.int32, (rows, ATT_WIDTH), 1)
        q_scr[...] = jnp.where(r_i // SAMPLE_T == c_i // HEAD_DIM, qt, 0.0).astype(BF16)
        m_scr[...] = jnp.full_like(m_scr, NEG_INF)
        l_scr[...] = jnp.zeros_like(l_scr)
        acc_scr[...] = jnp.zeros_like(acc_scr)
        carry_scr[...] = jnp.zeros_like(carry_scr)
        pad = jnp.zeros((PAGE_SIZE - SAMPLE_T, ATT_WIDTH), F32)
        k_new = jnp.concatenate([kn_ref[...], pad], axis=0).astype(BF16)
        v_new = jnp.concatenate([vn_ref[...], pad], axis=0).astype(BF16)
        c_new = jnp.concatenate([cn_ref[...], jnp.zeros((PAGE_SIZE - SAMPLE_T, LANES), F32)], axis=0)
        s = lax.dot_general(q_scr[...], k_new, nt_dims, preferred_element_type=F32)
        s = s - per_head_rows(c_new.T[:N_HEADS, :])
        r_s = lax.broadcasted_iota(jnp.int32, (rows, PAGE_SIZE), 0)
        c_s = lax.broadcasted_iota(jnp.int32, (rows, PAGE_SIZE), 1)
        update(jnp.where(c_s <= r_s % SAMPLE_T, s, NEG_INF),
               lambda p: jnp.dot(p, v_new, preferred_element_type=F32))

    pieces = []
    for i in range(npg):
        lf = f_refs[i][...]
        hi = lf.astype(BF16).astype(F32)
        mid = (lf - hi).astype(BF16).astype(F32)
        lo = (lf - hi - mid).astype(BF16).astype(F32)
        pieces += [hi, mid, lo]
    sums = jnp.dot(jnp.concatenate(pieces, axis=0).astype(BF16), tri_ref[...], preferred_element_type=F32)
    carry = carry_scr[...]
    bias = []
    for i in range(npg):
        part = [sums[(3 * i + j) * N_HEADS:(3 * i + j + 1) * N_HEADS, :] for j in range(AUG)]
        local = part[0] + part[1] + part[2]
        bias.append(local[:, :PAGE_SIZE] + carry)
        carry = carry + local[:, PAGE_SIZE:]
    carry_scr[...] = carry
    kt = jnp.concatenate([k_refs[i][...].astype(BF16) for i in range(npg)], axis=1)
    vt = jnp.concatenate([v_refs[i][...].astype(BF16) for i in range(npg)], axis=1)
    s = jnp.dot(q_scr[...], kt, preferred_element_type=F32) + per_head_rows(jnp.concatenate(bias, axis=1))
    update(s, lambda p: lax.dot_general(p, vt, nt_dims, preferred_element_type=F32))

    @pl.when(step == pl.num_programs(1) - 1)
    def _():
        o = acc_scr[...] / l_scr[...]
        r_i = lax.broadcasted_iota(jnp.int32, (rows, ATT_WIDTH), 0)
        c_i = lax.broadcasted_iota(jnp.int32, (rows, ATT_WIDTH), 1)
        o = jnp.where(r_i // SAMPLE_T == c_i // HEAD_DIM, o, 0.0).astype(BF16)
        t_i = lax.broadcasted_iota(jnp.int32, (SAMPLE_T, rows), 0)
        r_j = lax.broadcasted_iota(jnp.int32, (SAMPLE_T, rows), 1)
        sel = jnp.where(r_j % SAMPLE_T == t_i, 1.0, 0.0).astype(BF16)
        o_ref[...] = jnp.dot(sel, o, preferred_element_type=F32).astype(o_ref.dtype)


def _sample_attention(page_table, q, k_new, v_new, c_new, tri_page, cache_k, cache_v, cache_logf):
    nseq, n_pages = page_table.shape
    npg = PAGES_PER_STEP
    steps = n_pages // npg
    rows = N_HEADS * SAMPLE_T
    seq_spec = lambda width: pl.BlockSpec((None, SAMPLE_T, width), lambda b, s, pt: (b, 0, 0))

    def page_spec(height, i):
        def index(b, s, pt):
            return (pt[b * n_pages + (n_pages - 1) - (s * npg + i)], 0, 0)
        return pl.BlockSpec((None, height, PAGE_SIZE), index)

    in_specs = ([seq_spec(ATT_WIDTH), seq_spec(ATT_WIDTH), seq_spec(ATT_WIDTH), seq_spec(LANES),
                 pl.BlockSpec((PAGE_SIZE, 2 * PAGE_SIZE), lambda b, s, pt: (0, 0))]
                + [page_spec(ATT_WIDTH, i) for i in range(npg)]
                + [page_spec(ATT_WIDTH, i) for i in range(npg)]
                + [page_spec(N_HEADS, i) for i in range(npg)])
    grid_spec = pltpu.PrefetchScalarGridSpec(
        num_scalar_prefetch=1,
        grid=(nseq, steps),
        in_specs=in_specs,
        out_specs=seq_spec(ATT_WIDTH),
        scratch_shapes=[pltpu.VMEM((rows, ATT_WIDTH), BF16),
                        pltpu.VMEM((rows, 1), F32), pltpu.VMEM((rows, 1), F32),
                        pltpu.VMEM((rows, ATT_WIDTH), F32),
                        pltpu.VMEM((N_HEADS, LANES), F32)])
    return pl.pallas_call(
        _sample_attn_kernel,
        out_shape=jax.ShapeDtypeStruct((nseq, SAMPLE_T, ATT_WIDTH), BF16),
        grid_spec=grid_spec,
        compiler_params=pltpu.CompilerParams(dimension_semantics=("arbitrary", "arbitrary"),
                                             vmem_limit_bytes=VMEM_LIMIT),
        name="sample_attention",
    )(page_table.reshape(-1), q, k_new, v_new, c_new, tri_page,
      *([cache_k] * npg), *([cache_v] * npg), *([cache_logf] * npg))


def _merge_tail(x, conv, o_ref, ga_ref, gb_ref, g1, dwb_ref, lng_ref, lnb_ref, wpa_ref, wpb_ref, bpb_ref, wo_ref):
    yb = conv + dwb_ref[...]
    mu = jnp.mean(yb, axis=-1, keepdims=True)
    var = jnp.mean(jnp.square(yb - mu), axis=-1, keepdims=True)
    yb = (yb - mu) * lax.rsqrt(var + EPS) * lng_ref[...] + lnb_ref[...]
    yb = jnp.dot(_silu(yb).astype(BF16), wpb_ref[...], preferred_element_type=F32) + bpb_ref[...]
    ya = jnp.dot(o_ref[...], wpa_ref[...], preferred_element_type=F32)
    m = ga_ref[...].astype(F32) * ya + gb_ref[...].astype(F32) * yb
    return x + g1 * jnp.dot(m.astype(BF16), wo_ref[...], preferred_element_type=F32)


def _merge_prompt_kernel(x_ref, o_ref, ucur_ref, uprev_ref, ga_ref, gb_ref, g1_ref, dww_ref, dwb_ref, lng_ref,
                         lnb_ref, wpa_ref, wpb_ref, bpb_ref, wo_ref, out_ref, ubuf, ybuf):
    tm = x_ref.shape[0]
    first = pl.program_id(1) == 0
    prev = uprev_ref[...]
    ubuf[:, 0:CONV_HALO, :] = jnp.where(first, jnp.zeros_like(prev), prev)
    ubuf[:, CONV_HALO:, :] = ucur_ref[...]
    chunk = 64
    for c in range(N_SLABS):
        for r0 in range(0, tm, chunk):
            acc = jnp.zeros((chunk, LANES), F32)
            for j in range(CONV_K):
                off = r0 + CONV_HALO - (CONV_K - 1) + j
                acc = acc + dww_ref[c, j:j + 1, :] * ubuf[c, off:off + chunk, :]
            ybuf[r0:r0 + chunk, c * LANES:(c + 1) * LANES] = acc
    out_ref[...] = _merge_tail(x_ref[...], ybuf[...], o_ref, ga_ref, gb_ref, g1_ref[0], dwb_ref, lng_ref, lnb_ref,
                               wpa_ref, wpb_ref, bpb_ref, wo_ref)


def _merge_sample_kernel(x_ref, o_ref, uwin_ref, ga_ref, gb_ref, g1_ref, dww_ref, dwb_ref, lng_ref,
                         lnb_ref, wpa_ref, wpb_ref, bpb_ref, wo_ref, out_ref):
    nseq = uwin_ref.shape[1]
    slabs = []
    for c in range(N_SLABS):
        acc = jnp.zeros((nseq, SAMPLE_T, LANES), F32)
        for j in range(CONV_K):
            acc = acc + dww_ref[c, j:j + 1, :] * uwin_ref[c, :, j:j + SAMPLE_T, :]
        slabs.append(acc.reshape(nseq * SAMPLE_T, LANES))
    conv = jnp.concatenate(slabs, axis=1)
    out_ref[...] = _merge_tail(x_ref[...], conv, o_ref, ga_ref, gb_ref, g1_ref[0], dwb_ref, lng_ref, lnb_ref,
                               wpa_ref, wpb_ref, bpb_ref, wo_ref)


def _merge(x, o, u_args, ga, gb, g1, weights, *, nb, tm, sample):
    n = x.shape[0]
    nt = n // (nb * tm)
    row = lambda b, i: (b * nt + i, 0)
    const2 = lambda b, i: (0, 0)
    const3 = lambda b, i: (0, 0, 0)
    once = dict(pipeline_mode=pl.Buffered(1))
    mod_spec = pl.BlockSpec((1, g1.shape[1], D_MODEL), lambda b, i: (b, 0, 0))
    if sample:
        (uwin,) = u_args
        u_specs = [pl.BlockSpec(uwin.shape, lambda b, i: (0, 0, 0, 0))]
        kernel, scratch = _merge_sample_kernel, []
    else:
        (u,) = u_args
        u_args = (u, u)
        blocks_per_tile = tm // CONV_HALO
        u_specs = [pl.BlockSpec((N_SLABS, tm, LANES), lambda b, i: (0, b * nt + i, 0)),
                   pl.BlockSpec((N_SLABS, CONV_HALO, LANES),
                                lambda b, i: (0, jnp.maximum((b * nt + i) * blocks_per_tile - 1, 0), 0))]
        kernel = _merge_prompt_kernel
        scratch = [pltpu.VMEM((N_SLABS, CONV_HALO + tm, LANES), F32), pltpu.VMEM((tm, CONV_WIDTH), F32)]
    dww, dwb, lng, lnb, wpa, wpb, bpb, wo = weights
    in_specs = ([pl.BlockSpec((tm, D_MODEL), row), pl.BlockSpec((tm, ATT_WIDTH), row)] + u_specs
                + [pl.BlockSpec((tm, D_MODEL), row), pl.BlockSpec((tm, D_MODEL), row), mod_spec,
                   pl.BlockSpec(dww.shape, const3),
                   pl.BlockSpec((1, CONV_WIDTH), const2), pl.BlockSpec((1, CONV_WIDTH), const2),
                   pl.BlockSpec((1, CONV_WIDTH), const2),
                   pl.BlockSpec((ATT_WIDTH, D_MODEL), const2, **once),
                   pl.BlockSpec((CONV_WIDTH, D_MODEL), const2, **once),
                   pl.BlockSpec((1, D_MODEL), const2),
                   pl.BlockSpec((D_MODEL, D_MODEL), const2, **once)])
    return pl.pallas_call(
        kernel,
        out_shape=jax.ShapeDtypeStruct((n, D_MODEL), F32),
        grid=(nb, nt),
        in_specs=in_specs,
        out_specs=pl.BlockSpec((tm, D_MODEL), row),
        scratch_shapes=scratch,
        compiler_params=pltpu.CompilerParams(dimension_semantics=("arbitrary", "arbitrary"),
                                             vmem_limit_bytes=VMEM_LIMIT),
        name="merge_sample" if sample else "merge_prompt",
    )(x, o, *u_args, ga, gb, g1, dww, dwb, lng, lnb, wpa, wpb, bpb, wo)


def _ffn_kernel(x_ref, sh_ref, sc_ref, g2_ref, rg_ref, fg_ref, win_ref, wout_ref, out_ref, *, chunk):
    x = x_ref[...]
    ms = jnp.mean(x * x, axis=-1, keepdims=True)
    h = x * lax.rsqrt(ms + EPS) * rg_ref[...]
    hb = (h * (1.0 + sc_ref[0]) + sh_ref[0]).astype(BF16)
    acc = jnp.zeros(x.shape, F32)
    for lo in range(0, FFN_HIDDEN, chunk):
        gate = jnp.dot(hb, win_ref[:, lo:lo + chunk], preferred_element_type=F32)
        up = jnp.dot(hb, win_ref[:, FFN_HIDDEN + lo:FFN_HIDDEN + lo + chunk], preferred_element_type=F32)
        act = (_silu(gate) * up).astype(BF16)
        acc = acc + jnp.dot(act, wout_ref[lo:lo + chunk, :], preferred_element_type=F32)
    x2 = x + g2_ref[0] * acc
    ms2 = jnp.mean(x2 * x2, axis=-1, keepdims=True)
    out_ref[...] = x2 * lax.rsqrt(ms2 + EPS) * fg_ref[...]


def _ffn(x, sh, sc, g2, rms_g, final_g, w_in, w_out, *, nb, tm, name):
    n = x.shape[0]
    nt = n // (nb * tm)
    row = lambda b, i: (b * nt + i, 0)
    const2 = lambda b, i: (0, 0)
    once = dict(pipeline_mode=pl.Buffered(1))
    mod_spec = pl.BlockSpec((1, sh.shape[1], D_MODEL), lambda b, i: (b, 0, 0))
    return pl.pallas_call(
        functools.partial(_ffn_kernel, chunk=FFN_HIDDEN // 2),
        out_shape=jax.ShapeDtypeStruct((n, D_MODEL), F32),
        grid=(nb, nt),
        in_specs=[pl.BlockSpec((tm, D_MODEL), row), mod_spec, mod_spec, mod_spec,
                  pl.BlockSpec((1, D_MODEL), const2), pl.BlockSpec((1, D_MODEL), const2),
                  pl.BlockSpec((D_MODEL, 2 * FFN_HIDDEN), const2, **once),
                  pl.BlockSpec((FFN_HIDDEN, D_MODEL), const2, **once)],
        out_specs=pl.BlockSpec((tm, D_MODEL), row),
        compiler_params=pltpu.CompilerParams(dimension_semantics=("arbitrary", "arbitrary"),
                                             vmem_limit_bytes=VMEM_LIMIT),
        name=name,
    )(x, sh, sc, g2, rms_g, final_g, w_in, w_out)


def _bias_placement():
    place = np.zeros((LANES, 2 * ATT_WIDTH), np.float32)
    ones = np.zeros((1, 2 * ATT_WIDTH), np.float32)
    for h in range(N_HEADS):
        base = (h // 2) * LANES + (h % 2) * HEAD_DIM
        for piece in range(AUG):
            place[piece * N_HEADS + h, base + piece] = 1.0
            ones[0, base + AUG + piece] = 1.0
            ones[0, ATT_WIDTH + base + piece] = 1.0
            place[piece * N_HEADS + h, ATT_WIDTH + base + AUG + piece] = -1.0
    return jnp.asarray(place, BF16), jnp.asarray(ones, F32)


def _lower_tri(n, block):
    t = np.arange(n)[:, None]
    s = np.arange(n)[None, :]
    return jnp.asarray(((s <= t) & (t // block == s // block)).astype(np.float32), BF16)


def _later_keys(n):
    j = np.arange(n)[:, None]
    s = np.arange(n)[None, :]
    return jnp.asarray(np.concatenate([(j > s).astype(np.float32), np.ones((n, n), np.float32)], axis=1), BF16)


def kernel(x_prompt, x_sample, c_prompt, c_sample, cache_k, cache_v, cache_logf, state_conv, page_table, rms1_g, rms2_g, w_ada, b_ada, w_in, b_in, dw_w, dw_b, ln_g, ln_b, w_pa, w_pb, b_pb, w_o, w_ffn_in, w_ffn_out, final_g):
    nb, seq, _ = x_prompt.shape
    nseq, dec_t, _ = x_sample.shape
    depth = w_in.shape[0]
    assert depth == 1 and dec_t <= SAMPLE_T
    n_prompt = nb * seq
    tm = 512
    n_sample = nseq * SAMPLE_T

    w, b = w_in[0], b_in[0]
    f_off = 3 * ATT_WIDTH
    g_off = f_off + N_HEADS
    zpad = jnp.zeros((D_MODEL, LANES - N_HEADS), F32)
    w_all = jnp.concatenate([w[:, :f_off], w[:, f_off:g_off], zpad, w[:, g_off:]], axis=1).astype(BF16)
    b_all = jnp.concatenate([b[:f_off], b[f_off:g_off], jnp.zeros((LANES - N_HEADS,), F32), b[g_off:]])[None, :]
    dww = jnp.pad(dw_w[0], ((0, CONV_HALO - CONV_K), (0, 0))).reshape(CONV_HALO, N_SLABS, LANES).transpose(1, 0, 2)
    merge_w = (dww, dw_b[0][None, :], ln_g[0][None, :], ln_b[0][None, :], w_pa[0].astype(BF16),
               w_pb[0].astype(BF16), b_pb[0][None, :], w_o[0].astype(BF16))
    wf_in, wf_out = w_ffn_in[0].astype(BF16), w_ffn_out[0].astype(BF16)
    g1w, g2w, gfw = rms1_g[0][None, :], rms2_g[0][None, :], final_g[None, :]
    place, ones_row = _bias_placement()

    n_cond = nb + nseq
    c_all = jnp.pad(jnp.concatenate([c_prompt, c_sample], axis=0), ((0, -n_cond % 8), (0, 0)))
    mod = _modulation(c_all, w_ada[0], b_ada[0][None, :])
    mod_p = mod[:nb].reshape(nb, 1, 6, D_MODEL)
    sh1_p, sc1_p, g1_p, sh2_p, sc2_p, g2_p = (mod_p[:, :, i] for i in range(6))
    mod_s = jnp.repeat(mod[nb:n_cond], SAMPLE_T, axis=0).reshape(1, n_sample, 6, D_MODEL)
    sh1_s, sc1_s, g1_s, sh2_s, sc2_s, g2_s = (mod_s[:, :, i] for i in range(6))

    xp = x_prompt.reshape(n_prompt, D_MODEL)
    (qa, ka, kt_p, vt_p, vb_p, lft_p, u_p, ga_p, gb_p) = _inproj(
        xp, sh1_p, sc1_p, g1w, w_all, b_all, _lower_tri(tm, tm), place, ones_row, nb=nb, tm=tm, sample=False)
    o_p = _prompt_attention(qa, ka, vb_p, nb=nb, seq=seq, tq=2 * tm)
    x1_p = _merge(xp, o_p, (u_p,), ga_p, gb_p, g1_p, merge_w, nb=nb, tm=tm, sample=False)
    y_p = _ffn(x1_p, sh2_p, sc2_p, g2_p, g2w, gfw, wf_in, wf_out, nb=nb, tm=tm, name="ffn_prompt")

    xs = jnp.pad(x_sample, ((0, 0), (0, SAMPLE_T - dec_t), (0, 0))).reshape(n_sample, D_MODEL)
    (q_s, k_s, v_s, lf_s, c_s, u_s, ga_s, gb_s) = _inproj(
        xs, sh1_s, sc1_s, g1w, w_all, b_all, _lower_tri(n_sample, SAMPLE_T), place, ones_row,
        nb=1, tm=n_sample, sample=True)
    n_phys = cache_k.shape[1]
    page_t = lambda c: c[0].transpose(0, 2, 3, 1).reshape(n_phys, ATT_WIDTH, PAGE_SIZE)
    o_s = _sample_attention(
        page_table, q_s.reshape(nseq, SAMPLE_T, ATT_WIDTH), k_s.reshape(nseq, SAMPLE_T, ATT_WIDTH),
        v_s.reshape(nseq, SAMPLE_T, ATT_WIDTH), c_s.reshape(nseq, SAMPLE_T, LANES), _later_keys(PAGE_SIZE),
        page_t(cache_k), page_t(cache_v), cache_logf[0].transpose(0, 2, 1))
    state_slabs = state_conv[0].reshape(nseq, CONV_K - 1, N_SLABS, LANES).transpose(2, 0, 1, 3)
    u_slabs = u_s.reshape(N_SLABS, nseq, SAMPLE_T, LANES)
    uwin = jnp.concatenate(
        [state_slabs, u_slabs, jnp.zeros((N_SLABS, nseq, SAMPLE_WIN - (CONV_K - 1) - SAMPLE_T, LANES), F32)], axis=2)
    x1_s = _merge(xs, o_s.reshape(n_sample, ATT_WIDTH), (uwin,), ga_s, gb_s, g1_s, merge_w,
                  nb=1, tm=n_sample, sample=True)
    y_s = _ffn(x1_s, sh2_s, sc2_s, g2_s, g2w, gfw, wf_in, wf_out, nb=1, tm=n_sample, name="ffn_sample")

    n_pg = seq // PAGE_SIZE
    tail = CONV_K - 1
    u_tail = u_p.reshape(N_SLABS, nb, seq, LANES)[:, :, seq - tail:]
    u_tail = u_tail.transpose(1, 2, 0, 3).reshape(nb, tail, CONV_WIDTH)
    us_rows = u_s.reshape(N_SLABS, nseq, SAMPLE_T, LANES)[:, :, :dec_t]
    us_rows = us_rows.transpose(1, 2, 0, 3).reshape(nseq, dec_t, CONV_WIDTH)
    unpad = lambda a, width: a.reshape(nseq, SAMPLE_T, *width)[:, :dec_t]
    return (y_p.reshape(nb, seq, D_MODEL),
            unpad(y_s, (D_MODEL,)),
            kt_p.reshape(1, nb, n_pg, N_HEADS, HEAD_DIM, PAGE_SIZE).transpose(0, 1, 2, 5, 3, 4),
            vt_p.reshape(1, nb, n_pg, N_HEADS, HEAD_DIM, PAGE_SIZE).transpose(0, 1, 2, 5, 3, 4),
            lft_p.reshape(1, nb, n_pg, N_HEADS, PAGE_SIZE).transpose(0, 1, 2, 4, 3),
            u_tail[None],
            unpad(k_s, (N_HEADS, HEAD_DIM))[None],
            unpad(v_s, (N_HEADS, HEAD_DIM))[None],
            unpad(lf_s, (N_HEADS,))[None],
            jnp.concatenate([state_conv[0][:, dec_t:], us_rows], axis=1)[None])
```

```python
import functools

import numpy as np
import jax
import jax.numpy as jnp
from jax import lax
from jax.experimental import pallas as pl
from jax.experimental.pallas import tpu as pltpu

F32 = jnp.float32
BF16 = jnp.bfloat16

D_MODEL = 1024
N_HEADS = 8
HEAD_DIM = 64
ATT_WIDTH = N_HEADS * HEAD_DIM
CONV_WIDTH = 512
CONV_K = 31
FFN_HIDDEN = 2816
PAGE_SIZE = 128
EPS = 1e-6
NEG_INF = -1e30
SCALE = HEAD_DIM ** -0.5
LOG2E = 1.4426950408889634

LANES = 128
N_PAIRS = N_HEADS // 2
N_SLABS = CONV_WIDTH // LANES
AUG = 3
SAMPLE_T = 8
CONV_HALO = 32
SAMPLE_WIN = 40
PAGES_PER_STEP = 16
VMEM_LIMIT = 56 * 1024 * 1024

C_Q, C_K, C_V, C_F, C_GLA, C_GLB, C_GA, C_GB, C_END = 0, 512, 1024, 1536, 1664, 2176, 2688, 3712, 4736


def _sigmoid(x):
    return 1.0 / (1.0 + jnp.exp(-x))


def _silu(x):
    return x * _sigmoid(x)


def _split3_packed(a, lane):
    a = jnp.where(lane < N_HEADS, a, 0.0)
    hi = a.astype(BF16).astype(F32)
    r1 = a - hi
    mid = r1.astype(BF16).astype(F32)
    lo = (r1 - mid).astype(BF16).astype(F32)
    packed = hi + pltpu.roll(mid, N_HEADS, axis=1) + pltpu.roll(lo, 2 * N_HEADS, axis=1)
    return packed.astype(BF16)


def _unpack3(p, lane):
    s = p + pltpu.roll(p, LANES - N_HEADS, axis=1) + pltpu.roll(p, LANES - 2 * N_HEADS, axis=1)
    return jnp.where(lane < N_HEADS, s, 0.0)


def _mod_kernel(c_ref, w_ref, b_ref, o_ref):
    s = _silu(c_ref[...]).astype(BF16)
    o_ref[...] = jnp.dot(s, w_ref[...].astype(BF16), preferred_element_type=F32) + b_ref[...]


def _modulation(c_all, w_ada, b_ada):
    rows = c_all.shape[0]
    n = w_ada.shape[1]
    tn = 768
    return pl.pallas_call(
        _mod_kernel,
        out_shape=jax.ShapeDtypeStruct((rows, n), F32),
        grid=(n // tn,),
        in_specs=[pl.BlockSpec((rows, D_MODEL), lambda j: (0, 0)),
                  pl.BlockSpec((D_MODEL, tn), lambda j: (0, j)),
                  pl.BlockSpec((1, tn), lambda j: (0, j))],
        out_specs=pl.BlockSpec((rows, tn), lambda j: (0, j)),
        compiler_params=pltpu.CompilerParams(dimension_semantics=("arbitrary",), vmem_limit_bytes=VMEM_LIMIT),
        name="modulation",
    )(c_all, w_ada, b_ada)


def _inproj_kernel(x_ref, sh_ref, sc_ref, g_ref, w_ref, b_ref, tri_ref, place_ref, ones_ref, *refs, sample):
    if sample:
        qp_ref, k_ref, v_ref, lf_ref, c_ref, u_ref, ga_ref, gb_ref, carry_ref = refs
    else:
        qa_ref, ka_ref, kt_ref, vt_ref, vb_ref, lft_ref, u_ref, ga_ref, gb_ref, carry_ref = refs
    tm = x_ref.shape[0]

    @pl.when(pl.program_id(1) == 0)
    def _():
        carry_ref[...] = jnp.zeros_like(carry_ref)

    x = x_ref[...]
    ms = jnp.mean(x * x, axis=-1, keepdims=True)
    h = x * lax.rsqrt(ms + EPS) * g_ref[...]
    h = h * (1.0 + sc_ref[0]) + sh_ref[0]
    hb = h.astype(BF16)

    def seg(lo, hi):
        return jnp.dot(hb, w_ref[:, lo:hi], preferred_element_type=F32) + b_ref[:, lo:hi]

    lane = lax.broadcasted_iota(jnp.int32, (tm, LANES), 1)
    zf = seg(C_F, C_GLA)
    lf = jnp.minimum(zf, 0.0) - jnp.log1p(jnp.exp(-jnp.abs(zf)))
    lf = jnp.where(lane < N_HEADS, lf, 0.0)
    csum = _unpack3(jnp.dot(tri_ref[...], _split3_packed(lf, lane), preferred_element_type=F32), lane)
    csum = csum + carry_ref[...]
    carry_ref[...] = csum[tm - 1:tm, :]

    zq = seg(C_Q, C_K)
    zk = seg(C_K, C_V)
    zv = seg(C_V, C_F)
    if sample:
        qp_ref[...] = (zq * SCALE).astype(BF16)
        k_ref[...] = zk
        v_ref[...] = zv
        lf_ref[...] = lf[:, :N_HEADS]
        c_ref[...] = csum
    else:
        for pg in range(tm // PAGE_SIZE):
            rows = slice(pg * PAGE_SIZE, (pg + 1) * PAGE_SIZE)
            kt_ref[pg] = zk[rows, :].T
            vt_page = zv[rows, :].T
            vt_ref[pg] = vt_page
            vb_ref[:, rows] = vt_page.astype(BF16)
            lft_ref[pg] = lf[rows, :].T[:N_HEADS, :]
        zq = zq * (SCALE * LOG2E)
        aug = jnp.dot(_split3_packed(csum * LOG2E, lane), place_ref[...], preferred_element_type=F32) + ones_ref[...]
        for p in range(N_PAIRS):
            lo, hi = p * LANES, (p + 1) * LANES
            qa_ref[:, 2 * lo:2 * lo + LANES] = zq[:, lo:hi].astype(BF16)
            qa_ref[:, 2 * lo + LANES:2 * hi] = aug[:, lo:hi].astype(BF16)
            ka_ref[:, 2 * lo:2 * lo + LANES] = zk[:, lo:hi].astype(BF16)
            ka_ref[:, 2 * lo + LANES:2 * hi] = aug[:, ATT_WIDTH + lo:ATT_WIDTH + hi].astype(BF16)

    u = seg(C_GLA, C_GLB) * _sigmoid(seg(C_GLB, C_GA))
    for c in range(N_SLABS):
        u_ref[c] = u[:, c * LANES:(c + 1) * LANES]
    ga_ref[...] = _sigmoid(seg(C_GA, C_GB)).astype(BF16)
    gb_ref[...] = _sigmoid(seg(C_GB, C_END)).astype(BF16)


def _inproj(x, sh, sc, g1, w_all, b_all, tri, place, ones_row, *, nb, tm, sample):
    n = x.shape[0]
    nt = n // (nb * tm)
    row = lambda b, i: (b * nt + i, 0)
    const2 = lambda b, i: (0, 0)
    mod_rows = sh.shape[1]
    mod_spec = pl.BlockSpec((1, mod_rows, D_MODEL), lambda b, i: (b, 0, 0))
    once = dict(pipeline_mode=pl.Buffered(1))
    in_specs = [pl.BlockSpec((tm, D_MODEL), row), mod_spec, mod_spec,
                pl.BlockSpec((1, D_MODEL), const2),
                pl.BlockSpec((D_MODEL, C_END), const2, **once),
                pl.BlockSpec((1, C_END), const2),
                pl.BlockSpec((tm, tm), const2, **once),
                pl.BlockSpec((LANES, 2 * ATT_WIDTH), const2, **once),
                pl.BlockSpec((1, 2 * ATT_WIDTH), const2)]
    shared_shape = [jax.ShapeDtypeStruct((N_SLABS, n, LANES), F32),
                    jax.ShapeDtypeStruct((n, D_MODEL), BF16),
                    jax.ShapeDtypeStruct((n, D_MODEL), BF16)]
    shared_specs = [pl.BlockSpec((N_SLABS, tm, LANES), lambda b, i: (0, b * nt + i, 0)),
                    pl.BlockSpec((tm, D_MODEL), row), pl.BlockSpec((tm, D_MODEL), row)]
    if sample:
        out_shape = [jax.ShapeDtypeStruct((n, ATT_WIDTH), BF16),
                     jax.ShapeDtypeStruct((n, ATT_WIDTH), F32),
                     jax.ShapeDtypeStruct((n, ATT_WIDTH), F32),
                     jax.ShapeDtypeStruct((n, N_HEADS), F32),
                     jax.ShapeDtypeStruct((n, LANES), F32)]
        out_specs = [pl.BlockSpec((tm, ATT_WIDTH), row), pl.BlockSpec((tm, ATT_WIDTH), row),
                     pl.BlockSpec((tm, ATT_WIDTH), row), pl.BlockSpec((tm, N_HEADS), row),
                     pl.BlockSpec((tm, LANES), row)]
    else:
        n_pg, pg_tile = n // PAGE_SIZE, tm // PAGE_SIZE
        page = lambda b, i: (b * nt + i, 0, 0)
        out_shape = [jax.ShapeDtypeStruct((n, 2 * ATT_WIDTH), BF16),
                     jax.ShapeDtypeStruct((n, 2 * ATT_WIDTH), BF16),
                     jax.ShapeDtypeStruct((n_pg, ATT_WIDTH, PAGE_SIZE), F32),
                     jax.ShapeDtypeStruct((n_pg, ATT_WIDTH, PAGE_SIZE), F32),
                     jax.ShapeDtypeStruct((nb, ATT_WIDTH, n // nb), BF16),
                     jax.ShapeDtypeStruct((n_pg, N_HEADS, PAGE_SIZE), F32)]
        out_specs = [pl.BlockSpec((tm, 2 * ATT_WIDTH), row), pl.BlockSpec((tm, 2 * ATT_WIDTH), row),
                     pl.BlockSpec((pg_tile, ATT_WIDTH, PAGE_SIZE), page),
                     pl.BlockSpec((pg_tile, ATT_WIDTH, PAGE_SIZE), page),
                     pl.BlockSpec((None, ATT_WIDTH, tm), lambda b, i: (b, 0, i)),
                     pl.BlockSpec((pg_tile, N_HEADS, PAGE_SIZE), page)]
    out_shape += shared_shape
    out_specs += shared_specs
    return pl.pallas_call(
        functools.partial(_inproj_kernel, sample=sample),
        out_shape=out_shape,
        grid=(nb, nt),
        in_specs=in_specs,
        out_specs=out_specs,
        scratch_shapes=[pltpu.VMEM((1, LANES), F32)],
        compiler_params=pltpu.CompilerParams(dimension_semantics=("arbitrary", "arbitrary"),
                                             vmem_limit_bytes=VMEM_LIMIT),
        name="inproj_sample" if sample else "inproj_prompt",
    )(x, sh, sc, g1, w_all, b_all, tri, place, ones_row)


def _attn_kernel(qa_ref, ka_ref, vt_ref, o_ref, s_even, s_odd, mx_even, mx_odd, *, tq, tk):
    qi = pl.program_id(2)
    s_bufs = (s_even, s_odd)
    mx_bufs = (mx_even, mx_odd)
    qt = qa_ref[...].astype(F32).T
    feat = lax.broadcasted_iota(jnp.int32, qt.shape, 0)
    q_heads = [jnp.where((feat // HEAD_DIM) % 2 == hh, qt, 0.0).astype(BF16) for hh in range(2)]
    key = lax.broadcasted_iota(jnp.int32, (tk, tq), 0)
    qry = lax.broadcasted_iota(jnp.int32, (tk, tq), 1)

    def scores(ki, parity, hh, diagonal):
        ka = ka_ref[pl.ds(pl.multiple_of(ki * tk, tk), tk), :]
        st = jnp.dot(ka, q_heads[hh], preferred_element_type=F32)
        if diagonal:
            st = jnp.where(key + parity * tk <= qry, st, NEG_INF)
        s_bufs[parity][hh] = st
        mx_bufs[parity][hh] = jnp.max(st, axis=0, keepdims=True)

    def softmax_pv(ki, parity, hh, carry):
        m_prev, l_prev, acc = carry
        m_new = jnp.maximum(m_prev, mx_bufs[parity][hh])
        alpha = jnp.exp2(m_prev - m_new)
        pt = jnp.exp2(s_bufs[parity][hh] - m_new)
        l_new = alpha * l_prev + jnp.sum(pt, axis=0, keepdims=True)
        vt = vt_ref[hh * HEAD_DIM:(hh + 1) * HEAD_DIM, pl.ds(pl.multiple_of(ki * tk, tk), tk)]
        acc = acc * alpha + jnp.dot(vt, pt.astype(BF16), preferred_element_type=F32)
        return m_new, l_new, acc

    def stage(k_next, p_next, k_cur, p_cur, carry):
        out = []
        for hh in range(2):
            scores(k_next, p_next, hh, False)
            out.append(softmax_pv(k_cur, p_cur, hh, carry[hh]))
        return tuple(out)

    head0 = (jnp.full((1, tq), NEG_INF, F32), jnp.zeros((1, tq), F32), jnp.zeros((HEAD_DIM, tq), F32))
    for hh in range(2):
        scores(2 * qi, 0, hh, True)
        scores(2 * qi + 1, 1, hh, True)
    carry = tuple(softmax_pv(2 * qi, 0, hh, head0) for hh in range(2))

    def pair(j, carry):
        carry = stage(2 * j, 0, jnp.where(j == 0, 2 * qi + 1, 2 * j - 1), 1, carry)
        return stage(2 * j + 1, 1, 2 * j, 0, carry)

    carry = lax.fori_loop(0, qi, pair, carry)
    last = jnp.where(qi == 0, 1, 2 * qi - 1)
    (_, l0, acc0), (_, l1, acc1) = tuple(softmax_pv(last, 1, hh, carry[hh]) for hh in range(2))
    o_ref[...] = jnp.concatenate([acc0 / l0, acc1 / l1], axis=0).T.astype(o_ref.dtype)


def _prompt_attention(qa, ka, vt, *, nb, seq, tq):
    tk = tq // 2
    qa3 = qa.reshape(nb, seq, 2 * ATT_WIDTH)
    ka3 = ka.reshape(nb, seq, 2 * ATT_WIDTH)
    out = pl.pallas_call(
        functools.partial(_attn_kernel, tq=tq, tk=tk),
        out_shape=jax.ShapeDtypeStruct((nb, seq, ATT_WIDTH), BF16),
        grid=(nb, N_PAIRS, seq // tq),
        in_specs=[pl.BlockSpec((None, tq, 2 * LANES), lambda b, p, i: (b, i, p)),
                  pl.BlockSpec((None, seq, 2 * LANES), lambda b, p, i: (b, 0, p)),
                  pl.BlockSpec((None, LANES, seq), lambda b, p, i: (b, p, 0))],
        out_specs=pl.BlockSpec((None, tq, LANES), lambda b, p, i: (b, i, p)),
        scratch_shapes=[pltpu.VMEM((2, tk, tq), F32)] * 2 + [pltpu.VMEM((2, 1, tq), F32)] * 2,
        compiler_params=pltpu.CompilerParams(dimension_semantics=("arbitrary", "arbitrary", "arbitrary"),
                                             vmem_limit_bytes=VMEM_LIMIT),
        name="prompt_attention",
    )(qa3, ka3, vt)
    return out.reshape(nb * seq, ATT_WIDTH)


def _sample_attn_kernel(pt_ref, q_ref, kn_ref, vn_ref, cn_ref, tri_ref, lf_ref, *rest, n_pages):
    npg = PAGES_PER_STEP
    k_refs = rest[:npg]
    v_refs = rest[npg:2 * npg]
    o_ref, q_scr, m_scr, l_scr, acc_scr, carry_scr = rest[2 * npg:]
    step = pl.program_id(1)
    last_page = pl.program_id(0) * n_pages + (n_pages - 1) - step * npg
    rows = N_HEADS * SAMPLE_T
    nt_dims = (((1,), (1,)), ((), ()))

    def per_head_rows(a):
        return jnp.concatenate([jnp.broadcast_to(a[h:h + 1, :], (SAMPLE_T, a.shape[1])) for h in range(N_HEADS)],
                               axis=0)

    def update(s, pv):
        m_prev = m_scr[...]
        m_new = jnp.maximum(m_prev, jnp.max(s, axis=-1, keepdims=True))
        alpha = jnp.exp(m_prev - m_new)
        p = jnp.exp(s - m_new)
        l_scr[...] = alpha * l_scr[...] + jnp.sum(p, axis=-1, keepdims=True)
        acc_scr[...] = alpha * acc_scr[...] + pv(p.astype(BF16))
        m_scr[...] = m_new

    @pl.when(step == 0)
    def _():
        q8 = q_ref[...].astype(F32)
        qt = jnp.concatenate([q8] * N_HEADS, axis=0)
        r_i = lax.broadcasted_iota(jnp.int32, (rows, ATT_WIDTH), 0)
        c_i = lax.broadcasted_iota(jnp.int32, (rows, ATT_WIDTH), 1)
        q_scr[...] = jnp.where(r_i // SAMPLE_T == c_i // HEAD_DIM, qt, 0.0).astype(BF16)
        m_scr[...] = jnp.full_like(m_scr, NEG_INF)
        l_scr[...] = jnp.zeros_like(l_scr)
        acc_scr[...] = jnp.zeros_like(acc_scr)
        carry_scr[...] = jnp.zeros_like(carry_scr)
        pad = jnp.zeros((PAGE_SIZE - SAMPLE_T, ATT_WIDTH), F32)
        k_new = jnp.concatenate([kn_ref[...], pad], axis=0).astype(BF16)
        v_new = jnp.concatenate([vn_ref[...], pad], axis=0).astype(BF16)
        c_new = jnp.concatenate([cn_ref[...], jnp.zeros((PAGE_SIZE - SAMPLE_T, LANES), F32)], axis=0)
        s = lax.dot_general(q_scr[...], k_new, nt_dims, preferred_element_type=F32)
        s = s - per_head_rows(c_new.T[:N_HEADS, :])
        r_s = lax.broadcasted_iota(jnp.int32, (rows, PAGE_SIZE), 0)
        c_s = lax.broadcasted_iota(jnp.int32, (rows, PAGE_SIZE), 1)
        update(jnp.where(c_s <= r_s % SAMPLE_T, s, NEG_INF),
               lambda p: jnp.dot(p, v_new, preferred_element_type=F32))

    pieces = []
    for i in range(npg):
        lf = lf_ref[pt_ref[last_page - i]]
        hi = lf.astype(BF16).astype(F32)
        mid = (lf - hi).astype(BF16).astype(F32)
        lo = (lf - hi - mid).astype(BF16).astype(F32)
        pieces += [hi, mid, lo]
    sums = jnp.dot(jnp.concatenate(pieces, axis=0).astype(BF16), tri_ref[...], preferred_element_type=F32)
    carry = carry_scr[...]
    bias = []
    for i in range(npg):
        part = [sums[(3 * i + j) * N_HEADS:(3 * i + j + 1) * N_HEADS, :] for j in range(AUG)]
        local = part[0] + part[1] + part[2]
        bias.append(local[:, :PAGE_SIZE] + carry)
        carry = carry + local[:, PAGE_SIZE:]
    carry_scr[...] = carry
    kt = jnp.concatenate([k_refs[i][...].astype(BF16) for i in range(npg)], axis=1)
    vt = jnp.concatenate([v_refs[i][...].astype(BF16) for i in range(npg)], axis=1)
    s = jnp.dot(q_scr[...], kt, preferred_element_type=F32) + per_head_rows(jnp.concatenate(bias, axis=1))
    update(s, lambda p: lax.dot_general(p, vt, nt_dims, preferred_element_type=F32))

    @pl.when(step == pl.num_programs(1) - 1)
    def _():
        o = acc_scr[...] / l_scr[...]
        r_i = lax.broadcasted_iota(jnp.int32, (rows, ATT_WIDTH), 0)
        c_i = lax.broadcasted_iota(jnp.int32, (rows, ATT_WIDTH), 1)
        o = jnp.where(r_i // SAMPLE_T == c_i // HEAD_DIM, o, 0.0).astype(BF16)
        t_i = lax.broadcasted_iota(jnp.int32, (SAMPLE_T, rows), 0)
        r_j = lax.broadcasted_iota(jnp.int32, (SAMPLE_T, rows), 1)
        sel = jnp.where(r_j % SAMPLE_T == t_i, 1.0, 0.0).astype(BF16)
        o_ref[...] = jnp.dot(sel, o, preferred_element_type=F32).astype(o_ref.dtype)


def _sample_attention(page_table, q, k_new, v_new, c_new, tri_page, cache_k, cache_v, cache_logf):
    nseq, n_pages = page_table.shape
    npg = PAGES_PER_STEP
    steps = n_pages // npg
    rows = N_HEADS * SAMPLE_T
    seq_spec = lambda width: pl.BlockSpec((None, SAMPLE_T, width), lambda b, s, pt: (b, 0, 0))

    def page_spec(height, i):
        def index(b, s, pt):
            return (pt[b * n_pages + (n_pages - 1) - (s * npg + i)], 0, 0)
        return pl.BlockSpec((None, height, PAGE_SIZE), index)

    in_specs = ([seq_spec(ATT_WIDTH), seq_spec(ATT_WIDTH), seq_spec(ATT_WIDTH), seq_spec(LANES),
                 pl.BlockSpec((PAGE_SIZE, 2 * PAGE_SIZE), lambda b, s, pt: (0, 0)),
                 pl.BlockSpec(cache_logf.shape, lambda b, s, pt: (0, 0, 0), pipeline_mode=pl.Buffered(1))]
                + [page_spec(ATT_WIDTH, i) for i in range(npg)]
                + [page_spec(ATT_WIDTH, i) for i in range(npg)])
    grid_spec = pltpu.PrefetchScalarGridSpec(
        num_scalar_prefetch=1,
        grid=(nseq, steps),
        in_specs=in_specs,
        out_specs=seq_spec(ATT_WIDTH),
        scratch_shapes=[pltpu.VMEM((rows, ATT_WIDTH), BF16),
                        pltpu.VMEM((rows, 1), F32), pltpu.VMEM((rows, 1), F32),
                        pltpu.VMEM((rows, ATT_WIDTH), F32),
                        pltpu.VMEM((N_HEADS, LANES), F32)])
    return pl.pallas_call(
        functools.partial(_sample_attn_kernel, n_pages=n_pages),
        out_shape=jax.ShapeDtypeStruct((nseq, SAMPLE_T, ATT_WIDTH), BF16),
        grid_spec=grid_spec,
        compiler_params=pltpu.CompilerParams(dimension_semantics=("arbitrary", "arbitrary"),
                                             vmem_limit_bytes=VMEM_LIMIT),
        name="sample_attention",
    )(page_table.reshape(-1), q, k_new, v_new, c_new, tri_page, cache_logf,
      *([cache_k] * npg), *([cache_v] * npg))


def _merge_tail(x, conv, o_ref, ga_ref, gb_ref, g1, dwb_ref, lng_ref, lnb_ref, wpa_ref, wpb_ref, bpb_ref, wo_ref):
    yb = conv + dwb_ref[...]
    mu = jnp.mean(yb, axis=-1, keepdims=True)
    var = jnp.mean(jnp.square(yb - mu), axis=-1, keepdims=True)
    yb = (yb - mu) * lax.rsqrt(var + EPS) * lng_ref[...] + lnb_ref[...]
    yb = jnp.dot(_silu(yb).astype(BF16), wpb_ref[...], preferred_element_type=F32) + bpb_ref[...]
    ya = jnp.dot(o_ref[...], wpa_ref[...], preferred_element_type=F32)
    m = ga_ref[...].astype(F32) * ya + gb_ref[...].astype(F32) * yb
    return x + g1 * jnp.dot(m.astype(BF16), wo_ref[...], preferred_element_type=F32)


def _merge_prompt_kernel(x_ref, o_ref, ucur_ref, uprev_ref, ga_ref, gb_ref, g1_ref, dww_ref, dwb_ref, lng_ref,
                         lnb_ref, wpa_ref, wpb_ref, bpb_ref, wo_ref, out_ref, ubuf, ybuf):
    tm = x_ref.shape[0]
    first = pl.program_id(1) == 0
    prev = uprev_ref[...]
    ubuf[:, 0:CONV_HALO, :] = jnp.where(first, jnp.zeros_like(prev), prev)
    ubuf[:, CONV_HALO:, :] = ucur_ref[...]
    chunk = 64
    for c in range(N_SLABS):
        for r0 in range(0, tm, chunk):
            acc = jnp.zeros((chunk, LANES), F32)
            for j in range(CONV_K):
                off = r0 + CONV_HALO - (CONV_K - 1) + j
                acc = acc + dww_ref[c, j:j + 1, :] * ubuf[c, off:off + chunk, :]
            ybuf[r0:r0 + chunk, c * LANES:(c + 1) * LANES] = acc
    out_ref[...] = _merge_tail(x_ref[...], ybuf[...], o_ref, ga_ref, gb_ref, g1_ref[0], dwb_ref, lng_ref, lnb_ref,
                               wpa_ref, wpb_ref, bpb_ref, wo_ref)


def _merge_sample_kernel(x_ref, o_ref, uwin_ref, ga_ref, gb_ref, g1_ref, dww_ref, dwb_ref, lng_ref,
                         lnb_ref, wpa_ref, wpb_ref, bpb_ref, wo_ref, out_ref):
    nseq = uwin_ref.shape[1]
    slabs = []
    for c in range(N_SLABS):
        acc = jnp.zeros((nseq, SAMPLE_T, LANES), F32)
        for j in range(CONV_K):
            acc = acc + dww_ref[c, j:j + 1, :] * uwin_ref[c, :, j:j + SAMPLE_T, :]
        slabs.append(acc.reshape(nseq * SAMPLE_T, LANES))
    conv = jnp.concatenate(slabs, axis=1)
    out_ref[...] = _merge_tail(x_ref[...], conv, o_ref, ga_ref, gb_ref, g1_ref[0], dwb_ref, lng_ref, lnb_ref,
                               wpa_ref, wpb_ref, bpb_ref, wo_ref)


def _merge(x, o, u_args, ga, gb, g1, weights, *, nb, tm, sample):
    n = x.shape[0]
    nt = n // (nb * tm)
    row = lambda b, i: (b * nt + i, 0)
    const2 = lambda b, i: (0, 0)
    const3 = lambda b, i: (0, 0, 0)
    once = dict(pipeline_mode=pl.Buffered(1))
    mod_spec = pl.BlockSpec((1, g1.shape[1], D_MODEL), lambda b, i: (b, 0, 0))
    if sample:
        (uwin,) = u_args
        u_specs = [pl.BlockSpec(uwin.shape, lambda b, i: (0, 0, 0, 0))]
        kernel, scratch = _merge_sample_kernel, []
    else:
        (u,) = u_args
        u_args = (u, u)
        blocks_per_tile = tm // CONV_HALO
        u_specs = [pl.BlockSpec((N_SLABS, tm, LANES), lambda b, i: (0, b * nt + i, 0)),
                   pl.BlockSpec((N_SLABS, CONV_HALO, LANES),
                                lambda b, i: (0, jnp.maximum((b * nt + i) * blocks_per_tile - 1, 0), 0))]
        kernel = _merge_prompt_kernel
        scratch = [pltpu.VMEM((N_SLABS, CONV_HALO + tm, LANES), F32), pltpu.VMEM((tm, CONV_WIDTH), F32)]
    dww, dwb, lng, lnb, wpa, wpb, bpb, wo = weights
    in_specs = ([pl.BlockSpec((tm, D_MODEL), row), pl.BlockSpec((tm, ATT_WIDTH), row)] + u_specs
                + [pl.BlockSpec((tm, D_MODEL), row), pl.BlockSpec((tm, D_MODEL), row), mod_spec,
                   pl.BlockSpec(dww.shape, const3),
                   pl.BlockSpec((1, CONV_WIDTH), const2), pl.BlockSpec((1, CONV_WIDTH), const2),
                   pl.BlockSpec((1, CONV_WIDTH), const2),
                   pl.BlockSpec((ATT_WIDTH, D_MODEL), const2, **once),
                   pl.BlockSpec((CONV_WIDTH, D_MODEL), const2, **once),
                   pl.BlockSpec((1, D_MODEL), const2),
                   pl.BlockSpec((D_MODEL, D_MODEL), const2, **once)])
    return pl.pallas_call(
        kernel,
        out_shape=jax.ShapeDtypeStruct((n, D_MODEL), F32),
        grid=(nb, nt),
        in_specs=in_specs,
        out_specs=pl.BlockSpec((tm, D_MODEL), row),
        scratch_shapes=scratch,
        compiler_params=pltpu.CompilerParams(dimension_semantics=("arbitrary", "arbitrary"),
                                             vmem_limit_bytes=VMEM_LIMIT),
        name="merge_sample" if sample else "merge_prompt",
    )(x, o, *u_args, ga, gb, g1, dww, dwb, lng, lnb, wpa, wpb, bpb, wo)


def _ffn_kernel(x_ref, sh_ref, sc_ref, g2_ref, rg_ref, fg_ref, win_ref, wout_ref, out_ref, *, chunk):
    x = x_ref[...]
    ms = jnp.mean(x * x, axis=-1, keepdims=True)
    h = x * lax.rsqrt(ms + EPS) * rg_ref[...]
    hb = (h * (1.0 + sc_ref[0]) + sh_ref[0]).astype(BF16)
    acc = jnp.zeros(x.shape, F32)
    for lo in range(0, FFN_HIDDEN, chunk):
        gate = jnp.dot(hb, win_ref[:, lo:lo + chunk], preferred_element_type=F32)
        up = jnp.dot(hb, win_ref[:, FFN_HIDDEN + lo:FFN_HIDDEN + lo + chunk], preferred_element_type=F32)
        act = (_silu(gate) * up).astype(BF16)
        acc = acc + jnp.dot(act, wout_ref[lo:lo + chunk, :], preferred_element_type=F32)
    x2 = x + g2_ref[0] * acc
    ms2 = jnp.mean(x2 * x2, axis=-1, keepdims=True)
    out_ref[...] = x2 * lax.rsqrt(ms2 + EPS) * fg_ref[...]


def _ffn(x, sh, sc, g2, rms_g, final_g, w_in, w_out, *, nb, tm, name):
    n = x.shape[0]
    nt = n // (nb * tm)
    row = lambda b, i: (b * nt + i, 0)
    const2 = lambda b, i: (0, 0)
    once = dict(pipeline_mode=pl.Buffered(1))
    mod_spec = pl.BlockSpec((1, sh.shape[1], D_MODEL), lambda b, i: (b, 0, 0))
    return pl.pallas_call(
        functools.partial(_ffn_kernel, chunk=FFN_HIDDEN // 2),
        out_shape=jax.ShapeDtypeStruct((n, D_MODEL), F32),
        grid=(nb, nt),
        in_specs=[pl.BlockSpec((tm, D_MODEL), row), mod_spec, mod_spec, mod_spec,
                  pl.BlockSpec((1, D_MODEL), const2), pl.BlockSpec((1, D_MODEL), const2),
                  pl.BlockSpec((D_MODEL, 2 * FFN_HIDDEN), const2, **once),
                  pl.BlockSpec((FFN_HIDDEN, D_MODEL), const2, **once)],
        out_specs=pl.BlockSpec((tm, D_MODEL), row),
        compiler_params=pltpu.CompilerParams(dimension_semantics=("arbitrary", "arbitrary"),
                                             vmem_limit_bytes=VMEM_LIMIT),
        name=name,
    )(x, sh, sc, g2, rms_g, final_g, w_in, w_out)


def _bias_placement():
    place = np.zeros((LANES, 2 * ATT_WIDTH), np.float32)
    ones = np.zeros((1, 2 * ATT_WIDTH), np.float32)
    for h in range(N_HEADS):
        base = (h // 2) * LANES + (h % 2) * HEAD_DIM
        for piece in range(AUG):
            place[piece * N_HEADS + h, base + piece] = 1.0
            ones[0, base + AUG + piece] = 1.0
            ones[0, ATT_WIDTH + base + piece] = 1.0
            place[piece * N_HEADS + h, ATT_WIDTH + base + AUG + piece] = -1.0
    return jnp.asarray(place, BF16), jnp.asarray(ones, F32)


def _lower_tri(n, block):
    t = np.arange(n)[:, None]
    s = np.arange(n)[None, :]
    return jnp.asarray(((s <= t) & (t // block == s // block)).astype(np.float32), BF16)


def _later_keys(n):
    j = np.arange(n)[:, None]
    s = np.arange(n)[None, :]
    return jnp.asarray(np.concatenate([(j > s).astype(np.float32), np.ones((n, n), np.float32)], axis=1), BF16)


def kernel(x_prompt, x_sample, c_prompt, c_sample, cache_k, cache_v, cache_logf, state_conv, page_table, rms1_g, rms2_g, w_ada, b_ada, w_in, b_in, dw_w, dw_b, ln_g, ln_b, w_pa, w_pb, b_pb, w_o, w_ffn_in, w_ffn_out, final_g):
    nb, seq, _ = x_prompt.shape
    nseq, dec_t, _ = x_sample.shape
    depth = w_in.shape[0]
    assert depth == 1 and dec_t <= SAMPLE_T
    n_prompt = nb * seq
    tm = 512
    n_sample = nseq * SAMPLE_T

    w, b = w_in[0], b_in[0]
    f_off = 3 * ATT_WIDTH
    g_off = f_off + N_HEADS
    zpad = jnp.zeros((D_MODEL, LANES - N_HEADS), F32)
    w_all = jnp.concatenate([w[:, :f_off], w[:, f_off:g_off], zpad, w[:, g_off:]], axis=1).astype(BF16)
    b_all = jnp.concatenate([b[:f_off], b[f_off:g_off], jnp.zeros((LANES - N_HEADS,), F32), b[g_off:]])[None, :]
    dww = jnp.pad(dw_w[0], ((0, CONV_HALO - CONV_K), (0, 0))).reshape(CONV_HALO, N_SLABS, LANES).transpose(1, 0, 2)
    merge_w = (dww, dw_b[0][None, :], ln_g[0][None, :], ln_b[0][None, :], w_pa[0].astype(BF16),
               w_pb[0].astype(BF16), b_pb[0][None, :], w_o[0].astype(BF16))
    wf_in, wf_out = w_ffn_in[0].astype(BF16), w_ffn_out[0].astype(BF16)
    g1w, g2w, gfw = rms1_g[0][None, :], rms2_g[0][None, :], final_g[None, :]
    place, ones_row = _bias_placement()

    n_cond = nb + nseq
    c_all = jnp.pad(jnp.concatenate([c_prompt, c_sample], axis=0), ((0, -n_cond % 8), (0, 0)))
    mod = _modulation(c_all, w_ada[0], b_ada[0][None, :])
    mod_p = mod[:nb].reshape(nb, 1, 6, D_MODEL)
    sh1_p, sc1_p, g1_p, sh2_p, sc2_p, g2_p = (mod_p[:, :, i] for i in range(6))
    mod_s = jnp.repeat(mod[nb:n_cond], SAMPLE_T, axis=0).reshape(1, n_sample, 6, D_MODEL)
    sh1_s, sc1_s, g1_s, sh2_s, sc2_s, g2_s = (mod_s[:, :, i] for i in range(6))

    xp = x_prompt.reshape(n_prompt, D_MODEL)
    (qa, ka, kt_p, vt_p, vb_p, lft_p, u_p, ga_p, gb_p) = _inproj(
        xp, sh1_p, sc1_p, g1w, w_all, b_all, _lower_tri(tm, tm), place, ones_row, nb=nb, tm=tm, sample=False)
    o_p = _prompt_attention(qa, ka, vb_p, nb=nb, seq=seq, tq=2 * tm)
    x1_p = _merge(xp, o_p, (u_p,), ga_p, gb_p, g1_p, merge_w, nb=nb, tm=tm, sample=False)
    y_p = _ffn(x1_p, sh2_p, sc2_p, g2_p, g2w, gfw, wf_in, wf_out, nb=nb, tm=tm, name="ffn_prompt")

    xs = jnp.pad(x_sample, ((0, 0), (0, SAMPLE_T - dec_t), (0, 0))).reshape(n_sample, D_MODEL)
    (q_s, k_s, v_s, lf_s, c_s, u_s, ga_s, gb_s) = _inproj(
        xs, sh1_s, sc1_s, g1w, w_all, b_all, _lower_tri(n_sample, SAMPLE_T), place, ones_row,
        nb=1, tm=n_sample, sample=True)
    n_phys = cache_k.shape[1]
    page_t = lambda c: c[0].transpose(0, 2, 3, 1).reshape(n_phys, ATT_WIDTH, PAGE_SIZE)
    o_s = _sample_attention(
        page_table, q_s.reshape(nseq, SAMPLE_T, ATT_WIDTH), k_s.reshape(nseq, SAMPLE_T, ATT_WIDTH),
        v_s.reshape(nseq, SAMPLE_T, ATT_WIDTH), c_s.reshape(nseq, SAMPLE_T, LANES), _later_keys(PAGE_SIZE),
        page_t(cache_k), page_t(cache_v), cache_logf[0].transpose(0, 2, 1))
    state_slabs = state_conv[0].reshape(nseq, CONV_K - 1, N_SLABS, LANES).transpose(2, 0, 1, 3)
    u_slabs = u_s.reshape(N_SLABS, nseq, SAMPLE_T, LANES)
    uwin = jnp.concatenate(
        [state_slabs, u_slabs, jnp.zeros((N_SLABS, nseq, SAMPLE_WIN - (CONV_K - 1) - SAMPLE_T, LANES), F32)], axis=2)
    x1_s = _merge(xs, o_s.reshape(n_sample, ATT_WIDTH), (uwin,), ga_s, gb_s, g1_s, merge_w,
                  nb=1, tm=n_sample, sample=True)
    y_s = _ffn(x1_s, sh2_s, sc2_s, g2_s, g2w, gfw, wf_in, wf_out, nb=1, tm=n_sample, name="ffn_sample")

    n_pg = seq // PAGE_SIZE
    tail = CONV_K - 1
    u_tail = u_p.reshape(N_SLABS, nb, seq, LANES)[:, :, seq - tail:]
    u_tail = u_tail.transpose(1, 2, 0, 3).reshape(nb, tail, CONV_WIDTH)
    us_rows = u_s.reshape(N_SLABS, nseq, SAMPLE_T, LANES)[:, :, :dec_t]
    us_rows = us_rows.transpose(1, 2, 0, 3).reshape(nseq, dec_t, CONV_WIDTH)
    unpad = lambda a, width: a.reshape(nseq, SAMPLE_T, *width)[:, :dec_t]
    return (y_p.reshape(nb, seq, D_MODEL),
            unpad(y_s, (D_MODEL,)),
            kt_p.reshape(1, nb, n_pg, N_HEADS, HEAD_DIM, PAGE_SIZE).transpose(0, 1, 2, 5, 3, 4),
            vt_p.reshape(1, nb, n_pg, N_HEADS, HEAD_DIM, PAGE_SIZE).transpose(0, 1, 2, 5, 3, 4),
            lft_p.reshape(1, nb, n_pg, N_HEADS, PAGE_SIZE).transpose(0, 1, 2, 4, 3),
            u_tail[None],
            unpad(k_s, (N_HEADS, HEAD_DIM))[None],
            unpad(v_s, (N_HEADS, HEAD_DIM))[None],
            unpad(lf_s, (N_HEADS,))[None],
            jnp.concatenate([state_conv[0][:, dec_t:], us_rows], axis=1)[None])
```

```python
import functools

import numpy as np
import jax
import jax.numpy as jnp
from jax import lax
from jax.experimental import pallas as pl
from jax.experimental.pallas import tpu as pltpu

F32 = jnp.float32
BF16 = jnp.bfloat16

D_MODEL = 1024
N_HEADS = 8
HEAD_DIM = 64
ATT_WIDTH = N_HEADS * HEAD_DIM
CONV_WIDTH = 512
CONV_K = 31
FFN_HIDDEN = 2816
PAGE_SIZE = 128
EPS = 1e-6
NEG_INF = -1e30
SCALE = HEAD_DIM ** -0.5
LOG2E = 1.4426950408889634

LANES = 128
N_PAIRS = N_HEADS // 2
N_SLABS = CONV_WIDTH // LANES
AUG = 3
SAMPLE_T = 8
CONV_HALO = 32
SAMPLE_WIN = 40
PAGES_PER_STEP = 16
VMEM_LIMIT = 56 * 1024 * 1024

C_Q, C_K, C_V, C_F, C_GLA, C_GLB, C_GA, C_GB, C_END = 0, 512, 1024, 1536, 1664, 2176, 2688, 3712, 4736


def _sigmoid(x):
    return 1.0 / (1.0 + jnp.exp(-x))


def _silu(x):
    return x * _sigmoid(x)


def _split3_packed(a, lane):
    a = jnp.where(lane < N_HEADS, a, 0.0)
    hi = a.astype(BF16).astype(F32)
    r1 = a - hi
    mid = r1.astype(BF16).astype(F32)
    lo = (r1 - mid).astype(BF16).astype(F32)
    packed = hi + pltpu.roll(mid, N_HEADS, axis=1) + pltpu.roll(lo, 2 * N_HEADS, axis=1)
    return packed.astype(BF16)


def _unpack3(p, lane):
    s = p + pltpu.roll(p, LANES - N_HEADS, axis=1) + pltpu.roll(p, LANES - 2 * N_HEADS, axis=1)
    return jnp.where(lane < N_HEADS, s, 0.0)


def _mod_kernel(c_ref, w_ref, b_ref, o_ref):
    s = _silu(c_ref[...]).astype(BF16)
    o_ref[...] = jnp.dot(s, w_ref[...].astype(BF16), preferred_element_type=F32) + b_ref[...]


def _modulation(c_all, w_ada, b_ada):
    rows = c_all.shape[0]
    n_mod = w_ada.shape[1] // D_MODEL
    return pl.pallas_call(
        _mod_kernel,
        out_shape=jax.ShapeDtypeStruct((n_mod, rows, D_MODEL), F32),
        grid=(n_mod,),
        in_specs=[pl.BlockSpec((rows, D_MODEL), lambda j: (0, 0)),
                  pl.BlockSpec((D_MODEL, D_MODEL), lambda j: (0, j)),
                  pl.BlockSpec((1, D_MODEL), lambda j: (0, j))],
        out_specs=pl.BlockSpec((None, rows, D_MODEL), lambda j: (j, 0, 0)),
        compiler_params=pltpu.CompilerParams(dimension_semantics=("arbitrary",), vmem_limit_bytes=VMEM_LIMIT),
        name="modulation",
    )(c_all, w_ada, b_ada)


def _mod_rows(ref, nseq, sample):
    if sample:
        m = ref[0:nseq, :]
        return jnp.broadcast_to(m[:, None, :], (nseq, SAMPLE_T, D_MODEL)).reshape(nseq * SAMPLE_T, D_MODEL)
    return ref[pl.ds(nseq + pl.program_id(0), 1), :]


def _mod_spec(mod, kind):
    return pl.BlockSpec((None, mod.shape[1], D_MODEL), lambda b, i: (kind, 0, 0))


def _inproj_kernel(x_ref, sh_ref, sc_ref, g_ref, w_ref, b_ref, tri_ref, place_ref, ones_ref, *refs, sample, nseq):
    if sample:
        qp_ref, k_ref, v_ref, lf_ref, c_ref, u_ref, ga_ref, gb_ref, carry_ref = refs
    else:
        qa_ref, ka_ref, kt_ref, vt_ref, vb_ref, lft_ref, u_ref, ga_ref, gb_ref, carry_ref = refs
    tm = x_ref.shape[0]

    @pl.when(pl.program_id(1) == 0)
    def _():
        carry_ref[...] = jnp.zeros_like(carry_ref)

    x = x_ref[...]
    ms = jnp.mean(x * x, axis=-1, keepdims=True)
    h = x * lax.rsqrt(ms + EPS) * g_ref[...]
    h = h * (1.0 + _mod_rows(sc_ref, nseq, sample)) + _mod_rows(sh_ref, nseq, sample)
    hb = h.astype(BF16)

    def seg(lo, hi):
        z = lax.dot_general(hb, w_ref[lo:hi, :], (((1,), (1,)), ((), ())), preferred_element_type=F32)
        return z + b_ref[:, lo:hi]

    lane = lax.broadcasted_iota(jnp.int32, (tm, LANES), 1)
    zf = seg(C_F, C_GLA)
    lf = jnp.minimum(zf, 0.0) - jnp.log1p(jnp.exp(-jnp.abs(zf)))
    lf = jnp.where(lane < N_HEADS, lf, 0.0)
    csum = _unpack3(jnp.dot(tri_ref[...], _split3_packed(lf, lane), preferred_element_type=F32), lane)
    csum = csum + carry_ref[...]
    carry_ref[...] = csum[tm - 1:tm, :]

    zq = seg(C_Q, C_K)
    zk = seg(C_K, C_V)
    zv = seg(C_V, C_F)
    if sample:
        qp_ref[...] = (zq * SCALE).astype(BF16)
        k_ref[...] = zk
        v_ref[...] = zv
        lf_ref[...] = lf[:, :N_HEADS]
        c_ref[...] = csum
    else:
        for pg in range(tm // PAGE_SIZE):
            rows = slice(pg * PAGE_SIZE, (pg + 1) * PAGE_SIZE)
            kt_ref[pg] = zk[rows, :].T
            vt_page = zv[rows, :].T
            vt_ref[pg] = vt_page
            vb_ref[:, rows] = vt_page.astype(BF16)
            lft_ref[pg] = lf[rows, :].T[:N_HEADS, :]
        zq = zq * (SCALE * LOG2E)
        aug = jnp.dot(_split3_packed(csum * LOG2E, lane), place_ref[...], preferred_element_type=F32) + ones_ref[...]
        for p in range(N_PAIRS):
            lo, hi = p * LANES, (p + 1) * LANES
            qa_ref[:, 2 * lo:2 * lo + LANES] = zq[:, lo:hi].astype(BF16)
            qa_ref[:, 2 * lo + LANES:2 * hi] = aug[:, lo:hi].astype(BF16)
            ka_ref[:, 2 * lo:2 * lo + LANES] = zk[:, lo:hi].astype(BF16)
            ka_ref[:, 2 * lo + LANES:2 * hi] = aug[:, ATT_WIDTH + lo:ATT_WIDTH + hi].astype(BF16)

    u = seg(C_GLA, C_GLB) * _sigmoid(seg(C_GLB, C_GA))
    for c in range(N_SLABS):
        u_ref[c] = u[:, c * LANES:(c + 1) * LANES]
    ga_ref[...] = _sigmoid(seg(C_GA, C_GB)).astype(BF16)
    gb_ref[...] = _sigmoid(seg(C_GB, C_END)).astype(BF16)


def _inproj(x, mod, g1, w_all, b_all, tri, place, ones_row, *, nb, tm, sample, nseq):
    n = x.shape[0]
    nt = n // (nb * tm)
    row = lambda b, i: (b * nt + i, 0)
    const2 = lambda b, i: (0, 0)
    once = dict(pipeline_mode=pl.Buffered(1))
    in_specs = [pl.BlockSpec((tm, D_MODEL), row), _mod_spec(mod, 0), _mod_spec(mod, 1),
                pl.BlockSpec((1, D_MODEL), const2),
                pl.BlockSpec((C_END, D_MODEL), const2, **once),
                pl.BlockSpec((1, C_END), const2),
                pl.BlockSpec((tm, tm), const2, **once),
                pl.BlockSpec((LANES, 2 * ATT_WIDTH), const2, **once),
                pl.BlockSpec((1, 2 * ATT_WIDTH), const2)]
    shared_shape = [jax.ShapeDtypeStruct((N_SLABS, n, LANES), F32),
                    jax.ShapeDtypeStruct((n, D_MODEL), BF16),
                    jax.ShapeDtypeStruct((n, D_MODEL), BF16)]
    shared_specs = [pl.BlockSpec((N_SLABS, tm, LANES), lambda b, i: (0, b * nt + i, 0)),
                    pl.BlockSpec((tm, D_MODEL), row), pl.BlockSpec((tm, D_MODEL), row)]
    if sample:
        out_shape = [jax.ShapeDtypeStruct((n, ATT_WIDTH), BF16),
                     jax.ShapeDtypeStruct((n, ATT_WIDTH), F32),
                     jax.ShapeDtypeStruct((n, ATT_WIDTH), F32),
                     jax.ShapeDtypeStruct((n, N_HEADS), F32),
                     jax.ShapeDtypeStruct((n, LANES), F32)]
        out_specs = [pl.BlockSpec((tm, ATT_WIDTH), row), pl.BlockSpec((tm, ATT_WIDTH), row),
                     pl.BlockSpec((tm, ATT_WIDTH), row), pl.BlockSpec((tm, N_HEADS), row),
                     pl.BlockSpec((tm, LANES), row)]
    else:
        n_pg, pg_tile = n // PAGE_SIZE, tm // PAGE_SIZE
        page = lambda b, i: (b * nt + i, 0, 0)
        out_shape = [jax.ShapeDtypeStruct((n, 2 * ATT_WIDTH), BF16),
                     jax.ShapeDtypeStruct((n, 2 * ATT_WIDTH), BF16),
                     jax.ShapeDtypeStruct((n_pg, ATT_WIDTH, PAGE_SIZE), F32),
                     jax.ShapeDtypeStruct((n_pg, ATT_WIDTH, PAGE_SIZE), F32),
                     jax.ShapeDtypeStruct((nb, ATT_WIDTH, n // nb), BF16),
                     jax.ShapeDtypeStruct((n_pg, N_HEADS, PAGE_SIZE), F32)]
        out_specs = [pl.BlockSpec((tm, 2 * ATT_WIDTH), row), pl.BlockSpec((tm, 2 * ATT_WIDTH), row),
                     pl.BlockSpec((pg_tile, ATT_WIDTH, PAGE_SIZE), page),
                     pl.BlockSpec((pg_tile, ATT_WIDTH, PAGE_SIZE), page),
                     pl.BlockSpec((None, ATT_WIDTH, tm), lambda b, i: (b, 0, i)),
                     pl.BlockSpec((pg_tile, N_HEADS, PAGE_SIZE), page)]
    out_shape += shared_shape
    out_specs += shared_specs
    return pl.pallas_call(
        functools.partial(_inproj_kernel, sample=sample, nseq=nseq),
        out_shape=out_shape,
        grid=(nb, nt),
        in_specs=in_specs,
        out_specs=out_specs,
        scratch_shapes=[pltpu.VMEM((1, LANES), F32)],
        compiler_params=pltpu.CompilerParams(dimension_semantics=("arbitrary", "arbitrary"),
                                             vmem_limit_bytes=VMEM_LIMIT),
        name="inproj_sample" if sample else "inproj_prompt",
    )(x, mod, mod, g1, w_all, b_all, tri, place, ones_row)


def _attn_kernel(qa_ref, ka_ref, vt_ref, o_ref, s_even, s_odd, mx_even, mx_odd, *, tq, tk):
    qi = pl.program_id(2)
    s_bufs = (s_even, s_odd)
    mx_bufs = (mx_even, mx_odd)
    qt = qa_ref[...].astype(F32).T
    feat = lax.broadcasted_iota(jnp.int32, qt.shape, 0)
    q_heads = [jnp.where((feat // HEAD_DIM) % 2 == hh, qt, 0.0).astype(BF16) for hh in range(2)]
    key = lax.broadcasted_iota(jnp.int32, (tk, tq), 0)
    qry = lax.broadcasted_iota(jnp.int32, (tk, tq), 1)

    def scores(ki, parity, hh, diagonal):
        ka = ka_ref[pl.ds(pl.multiple_of(ki * tk, tk), tk), :]
        st = jnp.dot(ka, q_heads[hh], preferred_element_type=F32)
        if diagonal:
            st = jnp.where(key + parity * tk <= qry, st, NEG_INF)
        s_bufs[parity][hh] = st
        mx_bufs[parity][hh] = jnp.max(st, axis=0, keepdims=True)

    def softmax_pv(ki, parity, hh, carry):
        m_prev, l_prev, acc = carry
        m_new = jnp.maximum(m_prev, mx_bufs[parity][hh])
        alpha = jnp.exp2(m_prev - m_new)
        pt = jnp.exp2(s_bufs[parity][hh] - m_new)
        l_new = alpha * l_prev + jnp.sum(pt, axis=0, keepdims=True)
        vt = vt_ref[hh * HEAD_DIM:(hh + 1) * HEAD_DIM, pl.ds(pl.multiple_of(ki * tk, tk), tk)]
        acc = acc * alpha + jnp.dot(vt, pt.astype(BF16), preferred_element_type=F32)
        return m_new, l_new, acc

    def stage(k_next, p_next, k_cur, p_cur, carry):
        out = []
        for hh in range(2):
            scores(k_next, p_next, hh, False)
            out.append(softmax_pv(k_cur, p_cur, hh, carry[hh]))
        return tuple(out)

    head0 = (jnp.full((1, tq), NEG_INF, F32), jnp.zeros((1, tq), F32), jnp.zeros((HEAD_DIM, tq), F32))
    for hh in range(2):
        scores(2 * qi, 0, hh, True)
        scores(2 * qi + 1, 1, hh, True)
    carry = tuple(softmax_pv(2 * qi, 0, hh, head0) for hh in range(2))

    def pair(j, carry):
        carry = stage(2 * j, 0, jnp.where(j == 0, 2 * qi + 1, 2 * j - 1), 1, carry)
        return stage(2 * j + 1, 1, 2 * j, 0, carry)

    carry = lax.fori_loop(0, qi, pair, carry)
    last = jnp.where(qi == 0, 1, 2 * qi - 1)
    (_, l0, acc0), (_, l1, acc1) = tuple(softmax_pv(last, 1, hh, carry[hh]) for hh in range(2))
    o_ref[...] = jnp.concatenate([acc0 / l0, acc1 / l1], axis=0).T.astype(o_ref.dtype)


def _prompt_attention(qa, ka, vt, *, nb, seq, tq):
    tk = tq // 2
    qa3 = qa.reshape(nb, seq, 2 * ATT_WIDTH)
    ka3 = ka.reshape(nb, seq, 2 * ATT_WIDTH)
    out = pl.pallas_call(
        functools.partial(_attn_kernel, tq=tq, tk=tk),
        out_shape=jax.ShapeDtypeStruct((nb, seq, ATT_WIDTH), BF16),
        grid=(nb, N_PAIRS, seq // tq),
        in_specs=[pl.BlockSpec((None, tq, 2 * LANES), lambda b, p, i: (b, i, p)),
                  pl.BlockSpec((None, seq, 2 * LANES), lambda b, p, i: (b, 0, p)),
                  pl.BlockSpec((None, LANES, seq), lambda b, p, i: (b, p, 0))],
        out_specs=pl.BlockSpec((None, tq, LANES), lambda b, p, i: (b, i, p)),
        scratch_shapes=[pltpu.VMEM((2, tk, tq), F32)] * 2 + [pltpu.VMEM((2, 1, tq), F32)] * 2,
        compiler_params=pltpu.CompilerParams(dimension_semantics=("arbitrary", "arbitrary", "arbitrary"),
                                             vmem_limit_bytes=VMEM_LIMIT),
        name="prompt_attention",
    )(qa3, ka3, vt)
    return out.reshape(nb * seq, ATT_WIDTH)


def _sample_attn_kernel(pt_ref, q_ref, kn_ref, vn_ref, cn_ref, tri_ref, lf_ref, *rest, n_pages):
    npg = PAGES_PER_STEP
    k_refs = rest[:npg]
    v_refs = rest[npg:2 * npg]
    o_ref, q_scr, m_scr, l_scr, acc_scr, carry_scr = rest[2 * npg:]
    step = pl.program_id(1)
    last_page = pl.program_id(0) * n_pages + (n_pages - 1) - step * npg
    rows = N_HEADS * SAMPLE_T
    nt_dims = (((1,), (1,)), ((), ()))

    def per_head_rows(a):
        return jnp.concatenate([jnp.broadcast_to(a[h:h + 1, :], (SAMPLE_T, a.shape[1])) for h in range(N_HEADS)],
                               axis=0)

    def update(s, pv):
        m_prev = m_scr[...]
        m_new = jnp.maximum(m_prev, jnp.max(s, axis=-1, keepdims=True))
        alpha = jnp.exp(m_prev - m_new)
        p = jnp.exp(s - m_new)
        l_scr[...] = alpha * l_scr[...] + jnp.sum(p, axis=-1, keepdims=True)
        acc_scr[...] = alpha * acc_scr[...] + pv(p.astype(BF16))
        m_scr[...] = m_new

    @pl.when(step == 0)
    def _():
        q8 = q_ref[...].astype(F32)
        qt = jnp.concatenate([q8] * N_HEADS, axis=0)
        r_i = lax.broadcasted_iota(jnp.int32, (rows, ATT_WIDTH), 0)
        c_i = lax.broadcasted_iota(jnp.int32, (rows, ATT_WIDTH), 1)
        q_scr[...] = jnp.where(r_i // SAMPLE_T == c_i // HEAD_DIM, qt, 0.0).astype(BF16)
        m_scr[...] = jnp.full_like(m_scr, NEG_INF)
        l_scr[...] = jnp.zeros_like(l_scr)
        acc_scr[...] = jnp.zeros_like(acc_scr)
        carry_scr[...] = jnp.zeros_like(carry_scr)
        pad = jnp.zeros((PAGE_SIZE - SAMPLE_T, ATT_WIDTH), F32)
        k_new = jnp.concatenate([kn_ref[...], pad], axis=0).astype(BF16)
        v_new = jnp.concatenate([vn_ref[...], pad], axis=0).astype(BF16)
        c_new = jnp.concatenate([cn_ref[...], jnp.zeros((PAGE_SIZE - SAMPLE_T, LANES), F32)], axis=0)
        s = lax.dot_general(q_scr[...], k_new, nt_dims, preferred_element_type=F32)
        s = s - per_head_rows(c_new.T[:N_HEADS, :])
        r_s = lax.broadcasted_iota(jnp.int32, (rows, PAGE_SIZE), 0)
        c_s = lax.broadcasted_iota(jnp.int32, (rows, PAGE_SIZE), 1)
        update(jnp.where(c_s <= r_s % SAMPLE_T, s, NEG_INF),
               lambda p: jnp.dot(p, v_new, preferred_element_type=F32))

    pieces = []
    for i in range(npg):
        lf = lf_ref[pt_ref[last_page - i]]
        hi = lf.astype(BF16).astype(F32)
        mid = (lf - hi).astype(BF16).astype(F32)
        lo = (lf - hi - mid).astype(BF16).astype(F32)
        pieces += [hi, mid, lo]
    sums = jnp.dot(jnp.concatenate(pieces, axis=0).astype(BF16), tri_ref[...], preferred_element_type=F32)
    carry = carry_scr[...]
    bias = []
    for i in range(npg):
        part = [sums[(3 * i + j) * N_HEADS:(3 * i + j + 1) * N_HEADS, :] for j in range(AUG)]
        local = part[0] + part[1] + part[2]
        bias.append(local[:, :PAGE_SIZE] + carry)
        carry = carry + local[:, PAGE_SIZE:]
    carry_scr[...] = carry
    kt = jnp.concatenate([k_refs[i][...].astype(BF16) for i in range(npg)], axis=1)
    vt = jnp.concatenate([v_refs[i][...].astype(BF16) for i in range(npg)], axis=1)
    s = jnp.dot(q_scr[...], kt, preferred_element_type=F32) + per_head_rows(jnp.concatenate(bias, axis=1))
    update(s, lambda p: lax.dot_general(p, vt, nt_dims, preferred_element_type=F32))

    @pl.when(step == pl.num_programs(1) - 1)
    def _():
        o = acc_scr[...] / l_scr[...]
        r_i = lax.broadcasted_iota(jnp.int32, (rows, ATT_WIDTH), 0)
        c_i = lax.broadcasted_iota(jnp.int32, (rows, ATT_WIDTH), 1)
        o = jnp.where(r_i // SAMPLE_T == c_i // HEAD_DIM, o, 0.0).astype(BF16)
        t_i = lax.broadcasted_iota(jnp.int32, (SAMPLE_T, rows), 0)
        r_j = lax.broadcasted_iota(jnp.int32, (SAMPLE_T, rows), 1)
        sel = jnp.where(r_j % SAMPLE_T == t_i, 1.0, 0.0).astype(BF16)
        o_ref[...] = jnp.dot(sel, o, preferred_element_type=F32).astype(o_ref.dtype)


def _sample_attention(page_table, q, k_new, v_new, c_new, tri_page, cache_k, cache_v, cache_logf):
    nseq, n_pages = page_table.shape
    npg = PAGES_PER_STEP
    steps = n_pages // npg
    rows = N_HEADS * SAMPLE_T
    seq_spec = lambda width: pl.BlockSpec((None, SAMPLE_T, width), lambda b, s, pt: (b, 0, 0))

    def page_spec(height, i):
        def index(b, s, pt):
            return (pt[b * n_pages + (n_pages - 1) - (s * npg + i)], 0, 0)
        return pl.BlockSpec((None, height, PAGE_SIZE), index)

    in_specs = ([seq_spec(ATT_WIDTH), seq_spec(ATT_WIDTH), seq_spec(ATT_WIDTH), seq_spec(LANES),
                 pl.BlockSpec((PAGE_SIZE, 2 * PAGE_SIZE), lambda b, s, pt: (0, 0)),
                 pl.BlockSpec(cache_logf.shape, lambda b, s, pt: (0, 0, 0), pipeline_mode=pl.Buffered(1))]
                + [page_spec(ATT_WIDTH, i) for i in range(npg)]
                + [page_spec(ATT_WIDTH, i) for i in range(npg)])
    grid_spec = pltpu.PrefetchScalarGridSpec(
        num_scalar_prefetch=1,
        grid=(nseq, steps),
        in_specs=in_specs,
        out_specs=seq_spec(ATT_WIDTH),
        scratch_shapes=[pltpu.VMEM((rows, ATT_WIDTH), BF16),
                        pltpu.VMEM((rows, 1), F32), pltpu.VMEM((rows, 1), F32),
                        pltpu.VMEM((rows, ATT_WIDTH), F32),
                        pltpu.VMEM((N_HEADS, LANES), F32)])
    return pl.pallas_call(
        functools.partial(_sample_attn_kernel, n_pages=n_pages),
        out_shape=jax.ShapeDtypeStruct((nseq, SAMPLE_T, ATT_WIDTH), BF16),
        grid_spec=grid_spec,
        compiler_params=pltpu.CompilerParams(dimension_semantics=("arbitrary", "arbitrary"),
                                             vmem_limit_bytes=VMEM_LIMIT),
        name="sample_attention",
    )(page_table.reshape(-1), q, k_new, v_new, c_new, tri_page, cache_logf,
      *([cache_k] * npg), *([cache_v] * npg))


def _merge_tail(rows, conv, g1, x_ref, o_ref, ga_ref, gb_ref, dwb_ref, lng_ref, lnb_ref, wpa_ref, wpb_ref, bpb_ref,
                wo_ref, out_ref):
    yb = conv + dwb_ref[...]
    mu = jnp.mean(yb, axis=-1, keepdims=True)
    var = jnp.mean(jnp.square(yb - mu), axis=-1, keepdims=True)
    yb = (yb - mu) * lax.rsqrt(var + EPS) * lng_ref[...] + lnb_ref[...]
    yb = jnp.dot(_silu(yb).astype(BF16), wpb_ref[...], preferred_element_type=F32) + bpb_ref[...]
    ya = jnp.dot(o_ref[rows, :], wpa_ref[...], preferred_element_type=F32)
    m = ga_ref[rows, :].astype(F32) * ya + gb_ref[rows, :].astype(F32) * yb
    out_ref[rows, :] = x_ref[rows, :] + g1 * jnp.dot(m.astype(BF16), wo_ref[...], preferred_element_type=F32)


def _merge_prompt_kernel(x_ref, o_ref, ucur_ref, uprev_ref, ga_ref, gb_ref, g1_ref, dww_ref, dwb_ref, lng_ref,
                         lnb_ref, wpa_ref, wpb_ref, bpb_ref, wo_ref, out_ref, ubuf, *, nseq, row_block):
    tm = x_ref.shape[0]
    first = pl.program_id(1) == 0
    prev = uprev_ref[...]
    ubuf[:, 0:CONV_HALO, :] = jnp.where(first, jnp.zeros_like(prev), prev)
    ubuf[:, CONV_HALO:, :] = ucur_ref[...]
    g1 = _mod_rows(g1_ref, nseq, False)
    chunk = 64
    for b0 in range(0, tm, row_block):
        slabs = []
        for c in range(N_SLABS):
            pieces = []
            for r0 in range(b0, b0 + row_block, chunk):
                acc = jnp.zeros((chunk, LANES), F32)
                for j in range(CONV_K):
                    off = r0 + CONV_HALO - (CONV_K - 1) + j
                    acc = acc + dww_ref[c, j:j + 1, :] * ubuf[c, off:off + chunk, :]
                pieces.append(acc)
            slabs.append(jnp.concatenate(pieces, axis=0))
        _merge_tail(slice(b0, b0 + row_block), jnp.concatenate(slabs, axis=1), g1, x_ref, o_ref, ga_ref, gb_ref,
                    dwb_ref, lng_ref, lnb_ref, wpa_ref, wpb_ref, bpb_ref, wo_ref, out_ref)


def _merge_sample_kernel(x_ref, o_ref, uwin_ref, ga_ref, gb_ref, g1_ref, dww_ref, dwb_ref, lng_ref,
                         lnb_ref, wpa_ref, wpb_ref, bpb_ref, wo_ref, out_ref, *, nseq):
    slabs = []
    for c in range(N_SLABS):
        acc = jnp.zeros((nseq, SAMPLE_T, LANES), F32)
        for j in range(CONV_K):
            acc = acc + dww_ref[c, j:j + 1, :] * uwin_ref[c, :, j:j + SAMPLE_T, :]
        slabs.append(acc.reshape(nseq * SAMPLE_T, LANES))
    _merge_tail(slice(None), jnp.concatenate(slabs, axis=1), _mod_rows(g1_ref, nseq, True), x_ref, o_ref, ga_ref,
                gb_ref, dwb_ref, lng_ref, lnb_ref, wpa_ref, wpb_ref, bpb_ref, wo_ref, out_ref)


def _merge(x, o, u_args, ga, gb, mod, weights, *, nb, tm, sample, nseq):
    n = x.shape[0]
    nt = n // (nb * tm)
    row = lambda b, i: (b * nt + i, 0)
    const2 = lambda b, i: (0, 0)
    const3 = lambda b, i: (0, 0, 0)
    once = dict(pipeline_mode=pl.Buffered(1))
    mod_spec = _mod_spec(mod, 2)
    if sample:
        (uwin,) = u_args
        u_specs = [pl.BlockSpec(uwin.shape, lambda b, i: (0, 0, 0, 0))]
        kernel, scratch = functools.partial(_merge_sample_kernel, nseq=nseq), []
    else:
        (u,) = u_args
        u_args = (u, u)
        blocks_per_tile = tm // CONV_HALO
        u_specs = [pl.BlockSpec((N_SLABS, tm, LANES), lambda b, i: (0, b * nt + i, 0)),
                   pl.BlockSpec((N_SLABS, CONV_HALO, LANES),
                                lambda b, i: (0, jnp.maximum((b * nt + i) * blocks_per_tile - 1, 0), 0))]
        kernel = functools.partial(_merge_prompt_kernel, nseq=nseq, row_block=tm // 2)
        scratch = [pltpu.VMEM((N_SLABS, CONV_HALO + tm, LANES), F32)]
    dww, dwb, lng, lnb, wpa, wpb, bpb, wo = weights
    in_specs = ([pl.BlockSpec((tm, D_MODEL), row), pl.BlockSpec((tm, ATT_WIDTH), row)] + u_specs
                + [pl.BlockSpec((tm, D_MODEL), row), pl.BlockSpec((tm, D_MODEL), row), mod_spec,
                   pl.BlockSpec(dww.shape, const3),
                   pl.BlockSpec((1, CONV_WIDTH), const2), pl.BlockSpec((1, CONV_WIDTH), const2),
                   pl.BlockSpec((1, CONV_WIDTH), const2),
                   pl.BlockSpec((ATT_WIDTH, D_MODEL), const2, **once),
                   pl.BlockSpec((CONV_WIDTH, D_MODEL), const2, **once),
                   pl.BlockSpec((1, D_MODEL), const2),
                   pl.BlockSpec((D_MODEL, D_MODEL), const2, **once)])
    return pl.pallas_call(
        kernel,
        out_shape=jax.ShapeDtypeStruct((n, D_MODEL), F32),
        grid=(nb, nt),
        in_specs=in_specs,
        out_specs=pl.BlockSpec((tm, D_MODEL), row),
        scratch_shapes=scratch,
        compiler_params=pltpu.CompilerParams(dimension_semantics=("arbitrary", "arbitrary"),
                                             vmem_limit_bytes=VMEM_LIMIT),
        name="merge_sample" if sample else "merge_prompt",
    )(x, o, *u_args, ga, gb, mod, dww, dwb, lng, lnb, wpa, wpb, bpb, wo)


def _ffn_kernel(x_ref, sh_ref, sc_ref, g2_ref, rg_ref, fg_ref, win_ref, wout_ref, out_ref, *, chunk, sample, nseq):
    x = x_ref[...]
    ms = jnp.mean(x * x, axis=-1, keepdims=True)
    h = x * lax.rsqrt(ms + EPS) * rg_ref[...]
    hb = (h * (1.0 + _mod_rows(sc_ref, nseq, sample)) + _mod_rows(sh_ref, nseq, sample)).astype(BF16)
    acc = jnp.zeros(x.shape, F32)
    for lo in range(0, FFN_HIDDEN, chunk):
        gate = jnp.dot(hb, win_ref[:, lo:lo + chunk], preferred_element_type=F32)
        up = jnp.dot(hb, win_ref[:, FFN_HIDDEN + lo:FFN_HIDDEN + lo + chunk], preferred_element_type=F32)
        act = (_silu(gate) * up).astype(BF16)
        acc = acc + jnp.dot(act, wout_ref[lo:lo + chunk, :], preferred_element_type=F32)
    x2 = x + _mod_rows(g2_ref, nseq, sample) * acc
    ms2 = jnp.mean(x2 * x2, axis=-1, keepdims=True)
    out_ref[...] = x2 * lax.rsqrt(ms2 + EPS) * fg_ref[...]


def _ffn(x, mod, rms_g, final_g, w_in, w_out, *, nb, tm, sample, nseq):
    n = x.shape[0]
    nt = n // (nb * tm)
    row = lambda b, i: (b * nt + i, 0)
    const2 = lambda b, i: (0, 0)
    once = dict(pipeline_mode=pl.Buffered(1))
    return pl.pallas_call(
        functools.partial(_ffn_kernel, chunk=FFN_HIDDEN // 2, sample=sample, nseq=nseq),
        out_shape=jax.ShapeDtypeStruct((n, D_MODEL), F32),
        grid=(nb, nt),
        in_specs=[pl.BlockSpec((tm, D_MODEL), row), _mod_spec(mod, 3), _mod_spec(mod, 4), _mod_spec(mod, 5),
                  pl.BlockSpec((1, D_MODEL), const2), pl.BlockSpec((1, D_MODEL), const2),
                  pl.BlockSpec((D_MODEL, 2 * FFN_HIDDEN), const2, **once),
                  pl.BlockSpec((FFN_HIDDEN, D_MODEL), const2, **once)],
        out_specs=pl.BlockSpec((tm, D_MODEL), row),
        compiler_params=pltpu.CompilerParams(dimension_semantics=("arbitrary", "arbitrary"),
                                             vmem_limit_bytes=VMEM_LIMIT),
        name="ffn_sample" if sample else "ffn_prompt",
    )(x, mod, mod, mod, rms_g, final_g, w_in, w_out)


def _bias_placement():
    place = np.zeros((LANES, 2 * ATT_WIDTH), np.float32)
    ones = np.zeros((1, 2 * ATT_WIDTH), np.float32)
    for h in range(N_HEADS):
        base = (h // 2) * LANES + (h % 2) * HEAD_DIM
        for piece in range(AUG):
            place[piece * N_HEADS + h, base + piece] = 1.0
            ones[0, base + AUG + piece] = 1.0
            ones[0, ATT_WIDTH + base + piece] = 1.0
            place[piece * N_HEADS + h, ATT_WIDTH + base + AUG + piece] = -1.0
    return jnp.asarray(place, BF16), jnp.asarray(ones, F32)


def _lower_tri(n, block):
    t = np.arange(n)[:, None]
    s = np.arange(n)[None, :]
    return jnp.asarray(((s <= t) & (t // block == s // block)).astype(np.float32), BF16)


def _later_keys(n):
    j = np.arange(n)[:, None]
    s = np.arange(n)[None, :]
    return jnp.asarray(np.concatenate([(j > s).astype(np.float32), np.ones((n, n), np.float32)], axis=1), BF16)


def kernel(x_prompt, x_sample, c_prompt, c_sample, cache_k, cache_v, cache_logf, state_conv, page_table, rms1_g, rms2_g, w_ada, b_ada, w_in, b_in, dw_w, dw_b, ln_g, ln_b, w_pa, w_pb, b_pb, w_o, w_ffn_in, w_ffn_out, final_g):
    nb, seq, _ = x_prompt.shape
    nseq, dec_t, _ = x_sample.shape
    depth = w_in.shape[0]
    assert depth == 1 and dec_t <= SAMPLE_T
    n_prompt = nb * seq
    tm = 512
    n_sample = nseq * SAMPLE_T

    wt, b = w_in[0].T, b_in[0]
    g_off = 3 * ATT_WIDTH + N_HEADS
    w_all = jnp.concatenate([wt[:g_off], jnp.zeros((LANES - N_HEADS, D_MODEL), F32), wt[g_off:]], axis=0).astype(BF16)
    b_all = jnp.concatenate([b[:g_off], jnp.zeros((LANES - N_HEADS,), F32), b[g_off:]])[None, :]
    dww = jnp.pad(dw_w[0], ((0, CONV_HALO - CONV_K), (0, 0))).reshape(CONV_HALO, N_SLABS, LANES).transpose(1, 0, 2)
    merge_w = (dww, dw_b[0][None, :], ln_g[0][None, :], ln_b[0][None, :], w_pa[0].astype(BF16),
               w_pb[0].astype(BF16), b_pb[0][None, :], w_o[0].astype(BF16))
    wf_in, wf_out = w_ffn_in[0].astype(BF16), w_ffn_out[0].astype(BF16)
    g1w, g2w, gfw = rms1_g[0][None, :], rms2_g[0][None, :], final_g[None, :]
    place, ones_row = _bias_placement()

    n_cond = nb + nseq
    c_all = jnp.pad(jnp.concatenate([c_sample, c_prompt], axis=0), ((0, -n_cond % 8), (0, 0)))
    mod = _modulation(c_all, w_ada[0], b_ada[0][None, :])

    xp = x_prompt.reshape(n_prompt, D_MODEL)
    (qa, ka, kt_p, vt_p, vb_p, lft_p, u_p, ga_p, gb_p) = _inproj(
        xp, mod, g1w, w_all, b_all, _lower_tri(tm, tm), place, ones_row, nb=nb, tm=tm, sample=False, nseq=nseq)
    o_p = _prompt_attention(qa, ka, vb_p, nb=nb, seq=seq, tq=2 * tm)
    x1_p = _merge(xp, o_p, (u_p,), ga_p, gb_p, mod, merge_w, nb=nb, tm=tm, sample=False, nseq=nseq)
    y_p = _ffn(x1_p, mod, g2w, gfw, wf_in, wf_out, nb=nb, tm=tm, sample=False, nseq=nseq)

    xs = jnp.pad(x_sample, ((0, 0), (0, SAMPLE_T - dec_t), (0, 0))).reshape(n_sample, D_MODEL)
    (q_s, k_s, v_s, lf_s, c_s, u_s, ga_s, gb_s) = _inproj(
        xs, mod, g1w, w_all, b_all, _lower_tri(n_sample, SAMPLE_T), place, ones_row,
        nb=1, tm=n_sample, sample=True, nseq=nseq)
    n_phys = cache_k.shape[1]
    page_t = lambda c: c[0].transpose(0, 2, 3, 1).reshape(n_phys, ATT_WIDTH, PAGE_SIZE)
    o_s = _sample_attention(
        page_table, q_s.reshape(nseq, SAMPLE_T, ATT_WIDTH), k_s.reshape(nseq, SAMPLE_T, ATT_WIDTH),
        v_s.reshape(nseq, SAMPLE_T, ATT_WIDTH), c_s.reshape(nseq, SAMPLE_T, LANES), _later_keys(PAGE_SIZE),
        page_t(cache_k), page_t(cache_v), cache_logf[0].transpose(0, 2, 1))
    state_slabs = state_conv[0].reshape(nseq, CONV_K - 1, N_SLABS, LANES).transpose(2, 0, 1, 3)
    u_slabs = u_s.reshape(N_SLABS, nseq, SAMPLE_T, LANES)
    uwin = jnp.concatenate(
        [state_slabs, u_slabs, jnp.zeros((N_SLABS, nseq, SAMPLE_WIN - (CONV_K - 1) - SAMPLE_T, LANES), F32)], axis=2)
    x1_s = _merge(xs, o_s.reshape(n_sample, ATT_WIDTH), (uwin,), ga_s, gb_s, mod, merge_w,
                  nb=1, tm=n_sample, sample=True, nseq=nseq)
    y_s = _ffn(x1_s, mod, g2w, gfw, wf_in, wf_out, nb=1, tm=n_sample, sample=True, nseq=nseq)

    n_pg = seq // PAGE_SIZE
    tail = CONV_K - 1
    u_tail = u_p.reshape(N_SLABS, nb, seq, LANES)[:, :, seq - tail:]
    u_tail = u_tail.transpose(1, 2, 0, 3).reshape(nb, tail, CONV_WIDTH)
    us_rows = u_s.reshape(N_SLABS, nseq, SAMPLE_T, LANES)[:, :, :dec_t]
    us_rows = us_rows.transpose(1, 2, 0, 3).reshape(nseq, dec_t, CONV_WIDTH)
    unpad = lambda a, width: a.reshape(nseq, SAMPLE_T, *width)[:, :dec_t]
    return (y_p.reshape(nb, seq, D_MODEL),
            unpad(y_s, (D_MODEL,)),
            kt_p.reshape(1, nb, n_pg, N_HEADS, HEAD_DIM, PAGE_SIZE).transpose(0, 1, 2, 5, 3, 4),
            vt_p.reshape(1, nb, n_pg, N_HEADS, HEAD_DIM, PAGE_SIZE).transpose(0, 1, 2, 5, 3, 4),
            lft_p.reshape(1, nb, n_pg, N_HEADS, PAGE_SIZE).transpose(0, 1, 2, 4, 3),
            u_tail[None],
            unpad(k_s, (N_HEADS, HEAD_DIM))[None],
            unpad(v_s, (N_HEADS, HEAD_DIM))[None],
            unpad(lf_s, (N_HEADS,))[None],
            jnp.concatenate([state_conv[0][:, dec_t:], us_rows], axis=1)[None])
```

```python
import functools

import numpy as np
import jax
import jax.numpy as jnp
from jax import lax
from jax.experimental import pallas as pl
from jax.experimental.pallas import tpu as pltpu

F32 = jnp.float32
BF16 = jnp.bfloat16

D_MODEL = 1024
N_HEADS = 8
HEAD_DIM = 64
ATT_WIDTH = N_HEADS * HEAD_DIM
CONV_WIDTH = 512
CONV_K = 31
FFN_HIDDEN = 2816
PAGE_SIZE = 128
EPS = 1e-6
NEG_INF = -1e30
SCALE = HEAD_DIM ** -0.5
LOG2E = 1.4426950408889634

LANES = 128
N_PAIRS = N_HEADS // 2
N_SLABS = CONV_WIDTH // LANES
AUG = 3
SUM_ROWS = 16
ATTN_QCHUNK = 256
SAMPLE_T = 8
CONV_HALO = 32
SAMPLE_WIN = 40
PAGES_PER_STEP = 16
VMEM_LIMIT = 56 * 1024 * 1024

C_Q, C_K, C_V, C_F, C_GLA, C_GLB, C_GA, C_GB, C_END = 0, 512, 1024, 1536, 1664, 2176, 2688, 3712, 4736


def _sigmoid(x):
    return 1.0 / (1.0 + jnp.exp(-x))


def _silu(x):
    return x * _sigmoid(x)


def _split3_packed(a, lane):
    a = jnp.where(lane < N_HEADS, a, 0.0)
    hi = a.astype(BF16).astype(F32)
    r1 = a - hi
    mid = r1.astype(BF16).astype(F32)
    lo = (r1 - mid).astype(BF16).astype(F32)
    packed = hi + pltpu.roll(mid, N_HEADS, axis=1) + pltpu.roll(lo, 2 * N_HEADS, axis=1)
    return packed.astype(BF16)


def _unpack3(p, lane):
    s = p + pltpu.roll(p, LANES - N_HEADS, axis=1) + pltpu.roll(p, LANES - 2 * N_HEADS, axis=1)
    return jnp.where(lane < N_HEADS, s, 0.0)


def _mod_kernel(c_ref, w_ref, b_ref, o_ref):
    s = _silu(c_ref[...]).astype(BF16)
    o_ref[...] = jnp.dot(s, w_ref[...].astype(BF16), preferred_element_type=F32) + b_ref[...]


def _modulation(c_all, w_ada, b_ada):
    rows = c_all.shape[0]
    n_mod = w_ada.shape[1] // D_MODEL
    return pl.pallas_call(
        _mod_kernel,
        out_shape=jax.ShapeDtypeStruct((n_mod, rows, D_MODEL), F32),
        grid=(n_mod,),
        in_specs=[pl.BlockSpec((rows, D_MODEL), lambda j: (0, 0)),
                  pl.BlockSpec((D_MODEL, D_MODEL), lambda j: (0, j)),
                  pl.BlockSpec((1, D_MODEL), lambda j: (0, j))],
        out_specs=pl.BlockSpec((None, rows, D_MODEL), lambda j: (j, 0, 0)),
        compiler_params=pltpu.CompilerParams(dimension_semantics=("arbitrary",), vmem_limit_bytes=VMEM_LIMIT),
        name="modulation",
    )(c_all, w_ada, b_ada)


def _mod_rows(ref, nseq, sample):
    if sample:
        m = ref[0:nseq, :]
        return jnp.broadcast_to(m[:, None, :], (nseq, SAMPLE_T, D_MODEL)).reshape(nseq * SAMPLE_T, D_MODEL)
    return ref[pl.ds(nseq + pl.program_id(0), 1), :]


def _mod_spec(mod, kind):
    return pl.BlockSpec((None, mod.shape[1], D_MODEL), lambda b, i: (kind, 0, 0))


def _inproj_kernel(x_ref, sh_ref, sc_ref, g_ref, w_ref, b_ref, tri_ref, place_ref, ones_ref, *refs, sample, nseq):
    if sample:
        qp_ref, k_ref, v_ref, lf_ref, c_ref, u_ref, ga_ref, gb_ref, carry_ref = refs
    else:
        qa_ref, ka_ref, kt_ref, vt_ref, vb_ref, lft_ref, u_ref, ga_ref, gb_ref, carry_ref = refs
    tm = x_ref.shape[0]

    @pl.when(pl.program_id(1) == 0)
    def _():
        carry_ref[...] = jnp.zeros_like(carry_ref)

    x = x_ref[...]
    ms = jnp.mean(x * x, axis=-1, keepdims=True)
    h = x * lax.rsqrt(ms + EPS) * g_ref[...]
    h = h * (1.0 + _mod_rows(sc_ref, nseq, sample)) + _mod_rows(sh_ref, nseq, sample)
    hb = h.astype(BF16)

    def seg(lo, hi):
        z = lax.dot_general(hb, w_ref[lo:hi, :], (((1,), (1,)), ((), ())), preferred_element_type=F32)
        return z + b_ref[:, lo:hi]

    lane = lax.broadcasted_iota(jnp.int32, (tm, LANES), 1)
    zf = seg(C_F, C_GLA)
    lf = jnp.minimum(zf, 0.0) - jnp.log1p(jnp.exp(-jnp.abs(zf)))
    lf = jnp.where(lane < N_HEADS, lf, 0.0)
    csum = _unpack3(jnp.dot(tri_ref[...], _split3_packed(lf, lane), preferred_element_type=F32), lane)
    csum = csum + carry_ref[...]
    carry_ref[...] = csum[tm - 1:tm, :]

    zq = seg(C_Q, C_K)
    zk = seg(C_K, C_V)
    zv = seg(C_V, C_F)
    if sample:
        qp_ref[...] = (zq * SCALE).astype(BF16)
        k_ref[...] = zk
        v_ref[...] = zv
        lf_ref[...] = lf[:, :N_HEADS]
        c_ref[...] = csum
    else:
        for pg in range(tm // PAGE_SIZE):
            rows = slice(pg * PAGE_SIZE, (pg + 1) * PAGE_SIZE)
            kt_ref[pg] = zk[rows, :].T
            vt_page = zv[rows, :].T
            vt_ref[pg] = vt_page
            vb_ref[:, rows] = vt_page.astype(BF16)
            lft_ref[pg] = lf[rows, :].T[:N_HEADS, :]
        zq = zq * (SCALE * LOG2E)
        aug = jnp.dot(_split3_packed(csum * LOG2E, lane), place_ref[...], preferred_element_type=F32) + ones_ref[...]
        for p in range(N_PAIRS):
            lo, hi = p * LANES, (p + 1) * LANES
            qa_ref[:, 2 * lo:2 * lo + LANES] = zq[:, lo:hi].astype(BF16)
            qa_ref[:, 2 * lo + LANES:2 * hi] = aug[:, lo:hi].astype(BF16)
            ka_ref[:, 2 * lo:2 * lo + LANES] = zk[:, lo:hi].astype(BF16)
            ka_ref[:, 2 * lo + LANES:2 * hi] = aug[:, ATT_WIDTH + lo:ATT_WIDTH + hi].astype(BF16)

    u = seg(C_GLA, C_GLB) * _sigmoid(seg(C_GLB, C_GA))
    for c in range(N_SLABS):
        u_ref[c] = u[:, c * LANES:(c + 1) * LANES]
    ga_ref[...] = _sigmoid(seg(C_GA, C_GB)).astype(BF16)
    gb_ref[...] = _sigmoid(seg(C_GB, C_END)).astype(BF16)


def _inproj(x, mod, g1, w_all, b_all, tri, place, ones_row, *, nb, tm, sample, nseq):
    n = x.shape[0]
    nt = n // (nb * tm)
    row = lambda b, i: (b * nt + i, 0)
    const2 = lambda b, i: (0, 0)
    once = dict(pipeline_mode=pl.Buffered(1))
    in_specs = [pl.BlockSpec((tm, D_MODEL), row), _mod_spec(mod, 0), _mod_spec(mod, 1),
                pl.BlockSpec((1, D_MODEL), const2),
                pl.BlockSpec((C_END, D_MODEL), const2, **once),
                pl.BlockSpec((1, C_END), const2),
                pl.BlockSpec((tm, tm), const2, **once),
                pl.BlockSpec((LANES, 2 * ATT_WIDTH), const2, **once),
                pl.BlockSpec((1, 2 * ATT_WIDTH), const2)]
    shared_shape = [jax.ShapeDtypeStruct((N_SLABS, n, LANES), F32),
                    jax.ShapeDtypeStruct((n, D_MODEL), BF16),
                    jax.ShapeDtypeStruct((n, D_MODEL), BF16)]
    shared_specs = [pl.BlockSpec((N_SLABS, tm, LANES), lambda b, i: (0, b * nt + i, 0)),
                    pl.BlockSpec((tm, D_MODEL), row), pl.BlockSpec((tm, D_MODEL), row)]
    if sample:
        out_shape = [jax.ShapeDtypeStruct((n, ATT_WIDTH), BF16),
                     jax.ShapeDtypeStruct((n, ATT_WIDTH), F32),
                     jax.ShapeDtypeStruct((n, ATT_WIDTH), F32),
                     jax.ShapeDtypeStruct((n, N_HEADS), F32),
                     jax.ShapeDtypeStruct((n, LANES), F32)]
        out_specs = [pl.BlockSpec((tm, ATT_WIDTH), row), pl.BlockSpec((tm, ATT_WIDTH), row),
                     pl.BlockSpec((tm, ATT_WIDTH), row), pl.BlockSpec((tm, N_HEADS), row),
                     pl.BlockSpec((tm, LANES), row)]
    else:
        n_pg, pg_tile = n // PAGE_SIZE, tm // PAGE_SIZE
        page = lambda b, i: (b * nt + i, 0, 0)
        out_shape = [jax.ShapeDtypeStruct((n, 2 * ATT_WIDTH), BF16),
                     jax.ShapeDtypeStruct((n, 2 * ATT_WIDTH), BF16),
                     jax.ShapeDtypeStruct((n_pg, ATT_WIDTH, PAGE_SIZE), F32),
                     jax.ShapeDtypeStruct((n_pg, ATT_WIDTH, PAGE_SIZE), F32),
                     jax.ShapeDtypeStruct((nb, ATT_WIDTH, n // nb), BF16),
                     jax.ShapeDtypeStruct((n_pg, N_HEADS, PAGE_SIZE), F32)]
        out_specs = [pl.BlockSpec((tm, 2 * ATT_WIDTH), row), pl.BlockSpec((tm, 2 * ATT_WIDTH), row),
                     pl.BlockSpec((pg_tile, ATT_WIDTH, PAGE_SIZE), page),
                     pl.BlockSpec((pg_tile, ATT_WIDTH, PAGE_SIZE), page),
                     pl.BlockSpec((None, ATT_WIDTH, tm), lambda b, i: (b, 0, i)),
                     pl.BlockSpec((pg_tile, N_HEADS, PAGE_SIZE), page)]
    out_shape += shared_shape
    out_specs += shared_specs
    return pl.pallas_call(
        functools.partial(_inproj_kernel, sample=sample, nseq=nseq),
        out_shape=out_shape,
        grid=(nb, nt),
        in_specs=in_specs,
        out_specs=out_specs,
        scratch_shapes=[pltpu.VMEM((1, LANES), F32)],
        compiler_params=pltpu.CompilerParams(dimension_semantics=("arbitrary", "arbitrary"),
                                             vmem_limit_bytes=VMEM_LIMIT),
        name="inproj_sample" if sample else "inproj_prompt",
    )(x, mod, mod, g1, w_all, b_all, tri, place, ones_row)


def _attn_kernel(qa_ref, ka_ref, vt_ref, o_ref, s_even, s_odd, mx_even, mx_odd, *, tq, tk):
    qi = pl.program_id(2)
    s_bufs = (s_even, s_odd)
    mx_bufs = (mx_even, mx_odd)
    qt = qa_ref[...].astype(F32).T
    feat = lax.broadcasted_iota(jnp.int32, qt.shape, 0)
    q_heads = [jnp.where((feat // HEAD_DIM) % 2 == hh, qt, 0.0).astype(BF16) for hh in range(2)]
    key = lax.broadcasted_iota(jnp.int32, (tk, ATTN_QCHUNK), 0)
    qry = lax.broadcasted_iota(jnp.int32, (tk, ATTN_QCHUNK), 1)

    units = [(hh, slice(c, c + ATTN_QCHUNK), hh * (tq // ATTN_QCHUNK) + c // ATTN_QCHUNK)
             for hh in range(2) for c in range(0, tq, ATTN_QCHUNK)]

    def scores(ki, parity, unit, diagonal):
        hh, cols, u = unit
        ka = ka_ref[pl.ds(pl.multiple_of(ki * tk, tk), tk), :]
        st = jnp.dot(ka, q_heads[hh][:, cols], preferred_element_type=F32)
        if diagonal:
            st = jnp.where(key + parity * tk <= qry + cols.start, st, NEG_INF)
        s_bufs[parity][u] = st
        mx_bufs[parity][u] = jnp.max(st, axis=0, keepdims=True)

    ones_rows = jnp.ones((SUM_ROWS, tk), BF16)

    def softmax_pv(ki, parity, unit, carry):
        hh, _, u = unit
        m_prev, acc = carry
        m_new = jnp.maximum(m_prev, mx_bufs[parity][u])
        alpha = jnp.exp2(m_prev - m_new)
        pt = jnp.exp2(s_bufs[parity][u] - m_new).astype(BF16)
        vt = vt_ref[hh * HEAD_DIM:(hh + 1) * HEAD_DIM, pl.ds(pl.multiple_of(ki * tk, tk), tk)]
        vt = jnp.concatenate([vt, ones_rows], axis=0)
        return m_new, acc * alpha + jnp.dot(vt, pt, preferred_element_type=F32)

    def stage(k_next, p_next, k_cur, p_cur, carry, diagonal=False):
        out = []
        for unit, c in zip(units, carry):
            scores(k_next, p_next, unit, diagonal)
            out.append(softmax_pv(k_cur, p_cur, unit, c))
        return tuple(out)

    unit0 = (jnp.full((1, ATTN_QCHUNK), NEG_INF, F32), jnp.zeros((HEAD_DIM + SUM_ROWS, ATTN_QCHUNK), F32))
    for unit in units:
        scores(2 * qi, 0, unit, True)
    carry = stage(2 * qi + 1, 1, 2 * qi, 0, (unit0,) * len(units), diagonal=True)

    def pair(j, carry):
        carry = stage(2 * j, 0, jnp.where(j == 0, 2 * qi + 1, 2 * j - 1), 1, carry)
        return stage(2 * j + 1, 1, 2 * j, 0, carry)

    carry = lax.fori_loop(0, qi, pair, carry)
    last = jnp.where(qi == 0, 1, 2 * qi - 1)
    heads = [[], []]
    for unit, c in zip(units, carry):
        _, acc = softmax_pv(last, 1, unit, c)
        heads[unit[0]].append(acc[:HEAD_DIM] / acc[HEAD_DIM:HEAD_DIM + 1])
    o = jnp.concatenate([jnp.concatenate(h, axis=1) for h in heads], axis=0)
    o_ref[...] = o.T.astype(o_ref.dtype)


def _prompt_attention(qa, ka, vt, *, nb, seq, tq):
    tk = tq // 2
    qa3 = qa.reshape(nb, seq, 2 * ATT_WIDTH)
    ka3 = ka.reshape(nb, seq, 2 * ATT_WIDTH)
    out = pl.pallas_call(
        functools.partial(_attn_kernel, tq=tq, tk=tk),
        out_shape=jax.ShapeDtypeStruct((nb, seq, ATT_WIDTH), BF16),
        grid=(nb, N_PAIRS, seq // tq),
        in_specs=[pl.BlockSpec((None, tq, 2 * LANES), lambda b, p, i: (b, i, p)),
                  pl.BlockSpec((None, seq, 2 * LANES), lambda b, p, i: (b, 0, p)),
                  pl.BlockSpec((None, LANES, seq), lambda b, p, i: (b, p, 0))],
        out_specs=pl.BlockSpec((None, tq, LANES), lambda b, p, i: (b, i, p)),
        scratch_shapes=([pltpu.VMEM((2 * tq // ATTN_QCHUNK, tk, ATTN_QCHUNK), F32)] * 2
                        + [pltpu.VMEM((2 * tq // ATTN_QCHUNK, 1, ATTN_QCHUNK), F32)] * 2),
        compiler_params=pltpu.CompilerParams(dimension_semantics=("arbitrary", "arbitrary", "arbitrary"),
                                             vmem_limit_bytes=VMEM_LIMIT),
        name="prompt_attention",
    )(qa3, ka3, vt)
    return out.reshape(nb * seq, ATT_WIDTH)


def _sample_attn_kernel(pt_ref, q_ref, kn_ref, vn_ref, cn_ref, tri_ref, lf_ref, *rest, n_pages):
    npg = PAGES_PER_STEP
    k_refs = rest[:npg]
    v_refs = rest[npg:2 * npg]
    o_ref, q_scr, m_scr, l_scr, acc_scr, carry_scr = rest[2 * npg:]
    step = pl.program_id(1)
    last_page = pl.program_id(0) * n_pages + (n_pages - 1) - step * npg
    rows = N_HEADS * SAMPLE_T
    nt_dims = (((1,), (1,)), ((), ()))

    def per_head_rows(a):
        return jnp.concatenate([jnp.broadcast_to(a[h:h + 1, :], (SAMPLE_T, a.shape[1])) for h in range(N_HEADS)],
                               axis=0)

    def update(s, pv):
        m_prev = m_scr[...]
        m_new = jnp.maximum(m_prev, jnp.max(s, axis=-1, keepdims=True))
        alpha = jnp.exp(m_prev - m_new)
        p = jnp.exp(s - m_new)
        l_scr[...] = alpha * l_scr[...] + jnp.sum(p, axis=-1, keepdims=True)
        acc_scr[...] = alpha * acc_scr[...] + pv(p.astype(BF16))
        m_scr[...] = m_new

    @pl.when(step == 0)
    def _():
        q8 = q_ref[...].astype(F32)
        qt = jnp.concatenate([q8] * N_HEADS, axis=0)
        r_i = lax.broadcasted_iota(jnp.int32, (rows, ATT_WIDTH), 0)
        c_i = lax.broadcasted_iota(jnp.int32, (rows, ATT_WIDTH), 1)
        q_scr[...] = jnp.where(r_i // SAMPLE_T == c_i // HEAD_DIM, qt, 0.0).astype(BF16)
        m_scr[...] = jnp.full_like(m_scr, NEG_INF)
        l_scr[...] = jnp.zeros_like(l_scr)
        acc_scr[...] = jnp.zeros_like(acc_scr)
        carry_scr[...] = jnp.zeros_like(carry_scr)
        pad = jnp.zeros((PAGE_SIZE - SAMPLE_T, ATT_WIDTH), F32)
        k_new = jnp.concatenate([kn_ref[...], pad], axis=0).astype(BF16)
        v_new = jnp.concatenate([vn_ref[...], pad], axis=0).astype(BF16)
        c_new = jnp.concatenate([cn_ref[...], jnp.zeros((PAGE_SIZE - SAMPLE_T, LANES), F32)], axis=0)
        s = lax.dot_general(q_scr[...], k_new, nt_dims, preferred_element_type=F32)
        s = s - per_head_rows(c_new.T[:N_HEADS, :])
        r_s = lax.broadcasted_iota(jnp.int32, (rows, PAGE_SIZE), 0)
        c_s = lax.broadcasted_iota(jnp.int32, (rows, PAGE_SIZE), 1)
        update(jnp.where(c_s <= r_s % SAMPLE_T, s, NEG_INF),
               lambda p: jnp.dot(p, v_new, preferred_element_type=F32))

    pieces = []
    for i in range(npg):
        lf = lf_ref[pt_ref[last_page - i]]
        hi = lf.astype(BF16).astype(F32)
        mid = (lf - hi).astype(BF16).astype(F32)
        lo = (lf - hi - mid).astype(BF16).astype(F32)
        pieces += [hi, mid, lo]
    sums = jnp.dot(jnp.concatenate(pieces, axis=0).astype(BF16), tri_ref[...], preferred_element_type=F32)
    carry = carry_scr[...]
    bias = []
    for i in range(npg):
        part = [sums[(3 * i + j) * N_HEADS:(3 * i + j + 1) * N_HEADS, :] for j in range(AUG)]
        local = part[0] + part[1] + part[2]
        bias.append(local[:, :PAGE_SIZE] + carry)
        carry = carry + local[:, PAGE_SIZE:]
    carry_scr[...] = carry
    kt = jnp.concatenate([k_refs[i][...].astype(BF16) for i in range(npg)], axis=1)
    vt = jnp.concatenate([v_refs[i][...].astype(BF16) for i in range(npg)], axis=1)
    s = jnp.dot(q_scr[...], kt, preferred_element_type=F32) + per_head_rows(jnp.concatenate(bias, axis=1))
    update(s, lambda p: lax.dot_general(p, vt, nt_dims, preferred_element_type=F32))

    @pl.when(step == pl.num_programs(1) - 1)
    def _():
        o = acc_scr[...] / l_scr[...]
        r_i = lax.broadcasted_iota(jnp.int32, (rows, ATT_WIDTH), 0)
        c_i = lax.broadcasted_iota(jnp.int32, (rows, ATT_WIDTH), 1)
        o = jnp.where(r_i // SAMPLE_T == c_i // HEAD_DIM, o, 0.0).astype(BF16)
        t_i = lax.broadcasted_iota(jnp.int32, (SAMPLE_T, rows), 0)
        r_j = lax.broadcasted_iota(jnp.int32, (SAMPLE_T, rows), 1)
        sel = jnp.where(r_j % SAMPLE_T == t_i, 1.0, 0.0).astype(BF16)
        o_ref[...] = jnp.dot(sel, o, preferred_element_type=F32).astype(o_ref.dtype)


def _sample_attention(page_table, q, k_new, v_new, c_new, tri_page, cache_k, cache_v, cache_logf):
    nseq, n_pages = page_table.shape
    npg = PAGES_PER_STEP
    steps = n_pages // npg
    rows = N_HEADS * SAMPLE_T
    seq_spec = lambda width: pl.BlockSpec((None, SAMPLE_T, width), lambda b, s, pt: (b, 0, 0))

    def page_spec(height, i):
        def index(b, s, pt):
            return (pt[b * n_pages + (n_pages - 1) - (s * npg + i)], 0, 0)
        return pl.BlockSpec((None, height, PAGE_SIZE), index)

    in_specs = ([seq_spec(ATT_WIDTH), seq_spec(ATT_WIDTH), seq_spec(ATT_WIDTH), seq_spec(LANES),
                 pl.BlockSpec((PAGE_SIZE, 2 * PAGE_SIZE), lambda b, s, pt: (0, 0)),
                 pl.BlockSpec(cache_logf.shape, lambda b, s, pt: (0, 0, 0), pipeline_mode=pl.Buffered(1))]
                + [page_spec(ATT_WIDTH, i) for i in range(npg)]
                + [page_spec(ATT_WIDTH, i) for i in range(npg)])
    grid_spec = pltpu.PrefetchScalarGridSpec(
        num_scalar_prefetch=1,
        grid=(nseq, steps),
        in_specs=in_specs,
        out_specs=seq_spec(ATT_WIDTH),
        scratch_shapes=[pltpu.VMEM((rows, ATT_WIDTH), BF16),
                        pltpu.VMEM((rows, 1), F32), pltpu.VMEM((rows, 1), F32),
                        pltpu.VMEM((rows, ATT_WIDTH), F32),
                        pltpu.VMEM((N_HEADS, LANES), F32)])
    return pl.pallas_call(
        functools.partial(_sample_attn_kernel, n_pages=n_pages),
        out_shape=jax.ShapeDtypeStruct((nseq, SAMPLE_T, ATT_WIDTH), BF16),
        grid_spec=grid_spec,
        compiler_params=pltpu.CompilerParams(dimension_semantics=("arbitrary", "arbitrary"),
                                             vmem_limit_bytes=VMEM_LIMIT),
        name="sample_attention",
    )(page_table.reshape(-1), q, k_new, v_new, c_new, tri_page, cache_logf,
      *([cache_k] * npg), *([cache_v] * npg))


def _merge_tail(rows, conv, g1, x_ref, o_ref, ga_ref, gb_ref, dwb_ref, lng_ref, lnb_ref, wpa_ref, wpb_ref, bpb_ref,
                wo_ref, out_ref):
    yb = conv + dwb_ref[...]
    mu = jnp.mean(yb, axis=-1, keepdims=True)
    var = jnp.mean(jnp.square(yb - mu), axis=-1, keepdims=True)
    yb = (yb - mu) * lax.rsqrt(var + EPS) * lng_ref[...] + lnb_ref[...]
    yb = jnp.dot(_silu(yb).astype(BF16), wpb_ref[...], preferred_element_type=F32) + bpb_ref[...]
    ya = jnp.dot(o_ref[rows, :], wpa_ref[...], preferred_element_type=F32)
    m = ga_ref[rows, :].astype(F32) * ya + gb_ref[rows, :].astype(F32) * yb
    out_ref[rows, :] = x_ref[rows, :] + g1 * jnp.dot(m.astype(BF16), wo_ref[...], preferred_element_type=F32)


def _merge_prompt_kernel(x_ref, o_ref, ucur_ref, uprev_ref, ga_ref, gb_ref, g1_ref, dww_ref, dwb_ref, lng_ref,
                         lnb_ref, wpa_ref, wpb_ref, bpb_ref, wo_ref, out_ref, ubuf, *, nseq, row_block):
    tm = x_ref.shape[0]
    first = pl.program_id(1) == 0
    prev = uprev_ref[...]
    ubuf[:, 0:CONV_HALO, :] = jnp.where(first, jnp.zeros_like(prev), prev)
    ubuf[:, CONV_HALO:, :] = ucur_ref[...]
    g1 = _mod_rows(g1_ref, nseq, False)
    chunk = 64
    for b0 in range(0, tm, row_block):
        slabs = []
        for c in range(N_SLABS):
            pieces = []
            for r0 in range(b0, b0 + row_block, chunk):
                acc = jnp.zeros((chunk, LANES), F32)
                for j in range(CONV_K):
                    off = r0 + CONV_HALO - (CONV_K - 1) + j
                    acc = acc + dww_ref[c, j:j + 1, :] * ubuf[c, off:off + chunk, :]
                pieces.append(acc)
            slabs.append(jnp.concatenate(pieces, axis=0))
        _merge_tail(slice(b0, b0 + row_block), jnp.concatenate(slabs, axis=1), g1, x_ref, o_ref, ga_ref, gb_ref,
                    dwb_ref, lng_ref, lnb_ref, wpa_ref, wpb_ref, bpb_ref, wo_ref, out_ref)


def _merge_sample_kernel(x_ref, o_ref, uwin_ref, ga_ref, gb_ref, g1_ref, dww_ref, dwb_ref, lng_ref,
                         lnb_ref, wpa_ref, wpb_ref, bpb_ref, wo_ref, out_ref, *, nseq):
    slabs = []
    for c in range(N_SLABS):
        acc = jnp.zeros((nseq, SAMPLE_T, LANES), F32)
        for j in range(CONV_K):
            acc = acc + dww_ref[c, j:j + 1, :] * uwin_ref[c, :, j:j + SAMPLE_T, :]
        slabs.append(acc.reshape(nseq * SAMPLE_T, LANES))
    _merge_tail(slice(None), jnp.concatenate(slabs, axis=1), _mod_rows(g1_ref, nseq, True), x_ref, o_ref, ga_ref,
                gb_ref, dwb_ref, lng_ref, lnb_ref, wpa_ref, wpb_ref, bpb_ref, wo_ref, out_ref)


def _merge(x, o, u_args, ga, gb, mod, weights, *, nb, tm, sample, nseq):
    n = x.shape[0]
    nt = n // (nb * tm)
    row = lambda b, i: (b * nt + i, 0)
    const2 = lambda b, i: (0, 0)
    const3 = lambda b, i: (0, 0, 0)
    once = dict(pipeline_mode=pl.Buffered(1))
    mod_spec = _mod_spec(mod, 2)
    if sample:
        (uwin,) = u_args
        u_specs = [pl.BlockSpec(uwin.shape, lambda b, i: (0, 0, 0, 0))]
        kernel, scratch = functools.partial(_merge_sample_kernel, nseq=nseq), []
    else:
        (u,) = u_args
        u_args = (u, u)
        blocks_per_tile = tm // CONV_HALO
        u_specs = [pl.BlockSpec((N_SLABS, tm, LANES), lambda b, i: (0, b * nt + i, 0)),
                   pl.BlockSpec((N_SLABS, CONV_HALO, LANES),
                                lambda b, i: (0, jnp.maximum((b * nt + i) * blocks_per_tile - 1, 0), 0))]
        kernel = functools.partial(_merge_prompt_kernel, nseq=nseq, row_block=tm // 2)
        scratch = [pltpu.VMEM((N_SLABS, CONV_HALO + tm, LANES), F32)]
    dww, dwb, lng, lnb, wpa, wpb, bpb, wo = weights
    in_specs = ([pl.BlockSpec((tm, D_MODEL), row), pl.BlockSpec((tm, ATT_WIDTH), row)] + u_specs
                + [pl.BlockSpec((tm, D_MODEL), row), pl.BlockSpec((tm, D_MODEL), row), mod_spec,
                   pl.BlockSpec(dww.shape, const3),
                   pl.BlockSpec((1, CONV_WIDTH), const2), pl.BlockSpec((1, CONV_WIDTH), const2),
                   pl.BlockSpec((1, CONV_WIDTH), const2),
                   pl.BlockSpec((ATT_WIDTH, D_MODEL), const2, **once),
                   pl.BlockSpec((CONV_WIDTH, D_MODEL), const2, **once),
                   pl.BlockSpec((1, D_MODEL), const2),
                   pl.BlockSpec((D_MODEL, D_MODEL), const2, **once)])
    return pl.pallas_call(
        kernel,
        out_shape=jax.ShapeDtypeStruct((n, D_MODEL), F32),
        grid=(nb, nt),
        in_specs=in_specs,
        out_specs=pl.BlockSpec((tm, D_MODEL), row),
        scratch_shapes=scratch,
        compiler_params=pltpu.CompilerParams(dimension_semantics=("arbitrary", "arbitrary"),
                                             vmem_limit_bytes=VMEM_LIMIT),
        name="merge_sample" if sample else "merge_prompt",
    )(x, o, *u_args, ga, gb, mod, dww, dwb, lng, lnb, wpa, wpb, bpb, wo)


def _ffn_kernel(x_ref, sh_ref, sc_ref, g2_ref, rg_ref, fg_ref, win_ref, wout_ref, out_ref, *, chunk, sample, nseq):
    x = x_ref[...]
    ms = jnp.mean(x * x, axis=-1, keepdims=True)
    h = x * lax.rsqrt(ms + EPS) * rg_ref[...]
    hb = (h * (1.0 + _mod_rows(sc_ref, nseq, sample)) + _mod_rows(sh_ref, nseq, sample)).astype(BF16)
    acc = jnp.zeros(x.shape, F32)
    for lo in range(0, FFN_HIDDEN, chunk):
        gate = jnp.dot(hb, win_ref[:, lo:lo + chunk], preferred_element_type=F32)
        up = jnp.dot(hb, win_ref[:, FFN_HIDDEN + lo:FFN_HIDDEN + lo + chunk], preferred_element_type=F32)
        act = (_silu(gate) * up).astype(BF16)
        acc = acc + jnp.dot(act, wout_ref[lo:lo + chunk, :], preferred_element_type=F32)
    x2 = x + _mod_rows(g2_ref, nseq, sample) * acc
    ms2 = jnp.mean(x2 * x2, axis=-1, keepdims=True)
    out_ref[...] = x2 * lax.rsqrt(ms2 + EPS) * fg_ref[...]


def _ffn(x, mod, rms_g, final_g, w_in, w_out, *, nb, tm, sample, nseq):
    n = x.shape[0]
    nt = n // (nb * tm)
    row = lambda b, i: (b * nt + i, 0)
    const2 = lambda b, i: (0, 0)
    once = dict(pipeline_mode=pl.Buffered(1))
    return pl.pallas_call(
        functools.partial(_ffn_kernel, chunk=FFN_HIDDEN // 2, sample=sample, nseq=nseq),
        out_shape=jax.ShapeDtypeStruct((n, D_MODEL), F32),
        grid=(nb, nt),
        in_specs=[pl.BlockSpec((tm, D_MODEL), row), _mod_spec(mod, 3), _mod_spec(mod, 4), _mod_spec(mod, 5),
                  pl.BlockSpec((1, D_MODEL), const2), pl.BlockSpec((1, D_MODEL), const2),
                  pl.BlockSpec((D_MODEL, 2 * FFN_HIDDEN), const2, **once),
                  pl.BlockSpec((FFN_HIDDEN, D_MODEL), const2, **once)],
        out_specs=pl.BlockSpec((tm, D_MODEL), row),
        compiler_params=pltpu.CompilerParams(dimension_semantics=("arbitrary", "arbitrary"),
                                             vmem_limit_bytes=VMEM_LIMIT),
        name="ffn_sample" if sample else "ffn_prompt",
    )(x, mod, mod, mod, rms_g, final_g, w_in, w_out)


def _bias_placement():
    place = np.zeros((LANES, 2 * ATT_WIDTH), np.float32)
    ones = np.zeros((1, 2 * ATT_WIDTH), np.float32)
    for h in range(N_HEADS):
        base = (h // 2) * LANES + (h % 2) * HEAD_DIM
        for piece in range(AUG):
            place[piece * N_HEADS + h, base + piece] = 1.0
            ones[0, base + AUG + piece] = 1.0
            ones[0, ATT_WIDTH + base + piece] = 1.0
            place[piece * N_HEADS + h, ATT_WIDTH + base + AUG + piece] = -1.0
    return jnp.asarray(place, BF16), jnp.asarray(ones, F32)


def _lower_tri(n, block):
    t = np.arange(n)[:, None]
    s = np.arange(n)[None, :]
    return jnp.asarray(((s <= t) & (t // block == s // block)).astype(np.float32), BF16)


def _later_keys(n):
    j = np.arange(n)[:, None]
    s = np.arange(n)[None, :]
    return jnp.asarray(np.concatenate([(j > s).astype(np.float32), np.ones((n, n), np.float32)], axis=1), BF16)


def kernel(x_prompt, x_sample, c_prompt, c_sample, cache_k, cache_v, cache_logf, state_conv, page_table, rms1_g, rms2_g, w_ada, b_ada, w_in, b_in, dw_w, dw_b, ln_g, ln_b, w_pa, w_pb, b_pb, w_o, w_ffn_in, w_ffn_out, final_g):
    nb, seq, _ = x_prompt.shape
    nseq, dec_t, _ = x_sample.shape
    depth = w_in.shape[0]
    assert depth == 1 and dec_t <= SAMPLE_T
    n_prompt = nb * seq
    tm = 512
    n_sample = nseq * SAMPLE_T

    wt, b = w_in[0].T, b_in[0]
    g_off = 3 * ATT_WIDTH + N_HEADS
    w_all = jnp.concatenate([wt[:g_off], jnp.zeros((LANES - N_HEADS, D_MODEL), F32), wt[g_off:]], axis=0).astype(BF16)
    b_all = jnp.concatenate([b[:g_off], jnp.zeros((LANES - N_HEADS,), F32), b[g_off:]])[None, :]
    dww = jnp.pad(dw_w[0], ((0, CONV_HALO - CONV_K), (0, 0))).reshape(CONV_HALO, N_SLABS, LANES).transpose(1, 0, 2)
    merge_w = (dww, dw_b[0][None, :], ln_g[0][None, :], ln_b[0][None, :], w_pa[0].astype(BF16),
               w_pb[0].astype(BF16), b_pb[0][None, :], w_o[0].astype(BF16))
    wf_in, wf_out = w_ffn_in[0].astype(BF16), w_ffn_out[0].astype(BF16)
    g1w, g2w, gfw = rms1_g[0][None, :], rms2_g[0][None, :], final_g[None, :]
    place, ones_row = _bias_placement()

    n_cond = nb + nseq
    c_all = jnp.pad(jnp.concatenate([c_sample, c_prompt], axis=0), ((0, -n_cond % 8), (0, 0)))
    mod = _modulation(c_all, w_ada[0], b_ada[0][None, :])

    xp = x_prompt.reshape(n_prompt, D_MODEL)
    (qa, ka, kt_p, vt_p, vb_p, lft_p, u_p, ga_p, gb_p) = _inproj(
        xp, mod, g1w, w_all, b_all, _lower_tri(tm, tm), place, ones_row, nb=nb, tm=tm, sample=False, nseq=nseq)
    o_p = _prompt_attention(qa, ka, vb_p, nb=nb, seq=seq, tq=2 * tm)
    x1_p = _merge(xp, o_p, (u_p,), ga_p, gb_p, mod, merge_w, nb=nb, tm=tm, sample=False, nseq=nseq)
    y_p = _ffn(x1_p, mod, g2w, gfw, wf_in, wf_out, nb=nb, tm=tm, sample=False, nseq=nseq)

    xs = jnp.pad(x_sample, ((0, 0), (0, SAMPLE_T - dec_t), (0, 0))).reshape(n_sample, D_MODEL)
    (q_s, k_s, v_s, lf_s, c_s, u_s, ga_s, gb_s) = _inproj(
        xs, mod, g1w, w_all, b_all, _lower_tri(n_sample, SAMPLE_T), place, ones_row,
        nb=1, tm=n_sample, sample=True, nseq=nseq)
    n_phys = cache_k.shape[1]
    page_t = lambda c: c[0].transpose(0, 2, 3, 1).reshape(n_phys, ATT_WIDTH, PAGE_SIZE)
    o_s = _sample_attention(
        page_table, q_s.reshape(nseq, SAMPLE_T, ATT_WIDTH), k_s.reshape(nseq, SAMPLE_T, ATT_WIDTH),
        v_s.reshape(nseq, SAMPLE_T, ATT_WIDTH), c_s.reshape(nseq, SAMPLE_T, LANES), _later_keys(PAGE_SIZE),
        page_t(cache_k), page_t(cache_v), cache_logf[0].transpose(0, 2, 1))
    state_slabs = state_conv[0].reshape(nseq, CONV_K - 1, N_SLABS, LANES).transpose(2, 0, 1, 3)
    u_slabs = u_s.reshape(N_SLABS, nseq, SAMPLE_T, LANES)
    uwin = jnp.concatenate(
        [state_slabs, u_slabs, jnp.zeros((N_SLABS, nseq, SAMPLE_WIN - (CONV_K - 1) - SAMPLE_T, LANES), F32)], axis=2)
    x1_s = _merge(xs, o_s.reshape(n_sample, ATT_WIDTH), (uwin,), ga_s, gb_s, mod, merge_w,
                  nb=1, tm=n_sample, sample=True, nseq=nseq)
    y_s = _ffn(x1_s, mod, g2w, gfw, wf_in, wf_out, nb=1, tm=n_sample, sample=True, nseq=nseq)

    n_pg = seq // PAGE_SIZE
    tail = CONV_K - 1
    u_tail = u_p.reshape(N_SLABS, nb, seq, LANES)[:, :, seq - tail:]
    u_tail = u_tail.transpose(1, 2, 0, 3).reshape(nb, tail, CONV_WIDTH)
    us_rows = u_s.reshape(N_SLABS, nseq, SAMPLE_T, LANES)[:, :, :dec_t]
    us_rows = us_rows.transpose(1, 2, 0, 3).reshape(nseq, dec_t, CONV_WIDTH)
    unpad = lambda a, width: a.reshape(nseq, SAMPLE_T, *width)[:, :dec_t]
    return (y_p.reshape(nb, seq, D_MODEL),
            unpad(y_s, (D_MODEL,)),
            kt_p.reshape(1, nb, n_pg, N_HEADS, HEAD_DIM, PAGE_SIZE).transpose(0, 1, 2, 5, 3, 4),
            vt_p.reshape(1, nb, n_pg, N_HEADS, HEAD_DIM, PAGE_SIZE).transpose(0, 1, 2, 5, 3, 4),
            lft_p.reshape(1, nb, n_pg, N_HEADS, PAGE_SIZE).transpose(0, 1, 2, 4, 3),
            u_tail[None],
            unpad(k_s, (N_HEADS, HEAD_DIM))[None],
            unpad(v_s, (N_HEADS, HEAD_DIM))[None],
            unpad(lf_s, (N_HEADS,))[None],
            jnp.concatenate([state_conv[0][:, dec_t:], us_rows], axis=1)[None])
```

```python
import functools

import numpy as np
import jax
import jax.numpy as jnp
from jax import lax
from jax.experimental import pallas as pl
from jax.experimental.pallas import tpu as pltpu

F32 = jnp.float32
BF16 = jnp.bfloat16

D_MODEL = 1024
N_HEADS = 8
HEAD_DIM = 64
ATT_WIDTH = N_HEADS * HEAD_DIM
CONV_WIDTH = 512
CONV_K = 31
FFN_HIDDEN = 2816
PAGE_SIZE = 128
EPS = 1e-6
NEG_INF = -1e30
SCALE = HEAD_DIM ** -0.5
LOG2E = 1.4426950408889634

LANES = 128
N_PAIRS = N_HEADS // 2
N_SLABS = CONV_WIDTH // LANES
AUG = 3
SUM_ROWS = 16
ATTN_QCHUNK = 256
SAMPLE_T = 8
CONV_HALO = 32
SAMPLE_WIN = 40
PAGES_PER_STEP = 16
VMEM_LIMIT = 56 * 1024 * 1024

C_Q, C_K, C_V, C_F, C_GLA, C_GLB, C_GA, C_GB, C_END = 0, 512, 1024, 1536, 1664, 2176, 2688, 3712, 4736


def _sigmoid(x):
    return 1.0 / (1.0 + jnp.exp(-x))


def _silu(x):
    return x * _sigmoid(x)


def _split3_packed(a, lane):
    a = jnp.where(lane < N_HEADS, a, 0.0)
    hi = a.astype(BF16).astype(F32)
    r1 = a - hi
    mid = r1.astype(BF16).astype(F32)
    lo = (r1 - mid).astype(BF16).astype(F32)
    packed = hi + pltpu.roll(mid, N_HEADS, axis=1) + pltpu.roll(lo, 2 * N_HEADS, axis=1)
    return packed.astype(BF16)


def _unpack3(p, lane):
    s = p + pltpu.roll(p, LANES - N_HEADS, axis=1) + pltpu.roll(p, LANES - 2 * N_HEADS, axis=1)
    return jnp.where(lane < N_HEADS, s, 0.0)


def _mod_kernel(c_ref, w_ref, b_ref, o_ref):
    s = _silu(c_ref[...]).astype(BF16)
    o_ref[...] = jnp.dot(s, w_ref[...].astype(BF16), preferred_element_type=F32) + b_ref[...]


def _modulation(c_all, w_ada, b_ada):
    rows = c_all.shape[0]
    n_mod = w_ada.shape[1] // D_MODEL
    return pl.pallas_call(
        _mod_kernel,
        out_shape=jax.ShapeDtypeStruct((n_mod, rows, D_MODEL), F32),
        grid=(n_mod,),
        in_specs=[pl.BlockSpec((rows, D_MODEL), lambda j: (0, 0)),
                  pl.BlockSpec((D_MODEL, D_MODEL), lambda j: (0, j)),
                  pl.BlockSpec((1, D_MODEL), lambda j: (0, j))],
        out_specs=pl.BlockSpec((None, rows, D_MODEL), lambda j: (j, 0, 0)),
        compiler_params=pltpu.CompilerParams(dimension_semantics=("arbitrary",), vmem_limit_bytes=VMEM_LIMIT),
        name="modulation",
    )(c_all, w_ada, b_ada)


def _mod_rows(ref, nseq, sample):
    if sample:
        m = ref[0:nseq, :]
        return jnp.broadcast_to(m[:, None, :], (nseq, SAMPLE_T, D_MODEL)).reshape(nseq * SAMPLE_T, D_MODEL)
    return ref[pl.ds(nseq + pl.program_id(0), 1), :]


def _mod_spec(mod, kind):
    return pl.BlockSpec((None, mod.shape[1], D_MODEL), lambda b, i: (kind, 0, 0))


def _inproj_kernel(x_ref, sh_ref, sc_ref, g_ref, w_ref, b_ref, tri_ref, place_ref, ones_ref, *refs, sample, nseq):
    if sample:
        qp_ref, k_ref, v_ref, lf_ref, c_ref, u_ref, ga_ref, gb_ref, carry_ref = refs
    else:
        qat_ref, ka_ref, kt_ref, vt_ref, vb_ref, lft_ref, u_ref, ga_ref, gb_ref, carry_ref = refs
    tm = x_ref.shape[0]

    @pl.when(pl.program_id(1) == 0)
    def _():
        carry_ref[...] = jnp.zeros_like(carry_ref)

    x = x_ref[...]
    ms = jnp.mean(x * x, axis=-1, keepdims=True)
    h = x * lax.rsqrt(ms + EPS) * g_ref[...]
    h = h * (1.0 + _mod_rows(sc_ref, nseq, sample)) + _mod_rows(sh_ref, nseq, sample)
    hb = h.astype(BF16)

    def seg(lo, hi):
        z = lax.dot_general(hb, w_ref[lo:hi, :], (((1,), (1,)), ((), ())), preferred_element_type=F32)
        return z + b_ref[:, lo:hi]

    lane = lax.broadcasted_iota(jnp.int32, (tm, LANES), 1)
    zf = seg(C_F, C_GLA)
    lf = jnp.minimum(zf, 0.0) - jnp.log1p(jnp.exp(-jnp.abs(zf)))
    lf = jnp.where(lane < N_HEADS, lf, 0.0)
    csum = _unpack3(jnp.dot(tri_ref[...], _split3_packed(lf, lane), preferred_element_type=F32), lane)
    csum = csum + carry_ref[...]
    carry_ref[...] = csum[tm - 1:tm, :]

    zq = seg(C_Q, C_K)
    zk = seg(C_K, C_V)
    zv = seg(C_V, C_F)
    if sample:
        qp_ref[...] = (zq * SCALE).astype(BF16)
        k_ref[...] = zk
        v_ref[...] = zv
        lf_ref[...] = lf[:, :N_HEADS]
        c_ref[...] = csum
    else:
        for pg in range(tm // PAGE_SIZE):
            rows = slice(pg * PAGE_SIZE, (pg + 1) * PAGE_SIZE)
            kt_ref[pg] = zk[rows, :].T
            vt_page = zv[rows, :].T
            vt_ref[pg] = vt_page
            vb_ref[:, rows] = vt_page.astype(BF16)
            lft_ref[pg] = lf[rows, :].T[:N_HEADS, :]
        zq = zq * (SCALE * LOG2E)
        aug = jnp.dot(_split3_packed(csum * LOG2E, lane), place_ref[...], preferred_element_type=F32) + ones_ref[...]
        for h in range(N_HEADS):
            own_half = (lane < HEAD_DIM) if h % 2 == 0 else (lane >= HEAD_DIM)
            pair, feat = slice((h // 2) * LANES, (h // 2 + 1) * LANES), slice(h * LANES, (h + 1) * LANES)
            qat_ref[feat, :] = jnp.where(own_half, zq[:, pair], aug[:, feat]).T.astype(BF16)
            k_feat = jnp.where(own_half, zk[:, pair], aug[:, N_HEADS * LANES + h * LANES:N_HEADS * LANES + (h + 1) * LANES])
            ka_ref[:, feat] = k_feat.astype(BF16)

    u = seg(C_GLA, C_GLB) * _sigmoid(seg(C_GLB, C_GA))
    for c in range(N_SLABS):
        u_ref[c] = u[:, c * LANES:(c + 1) * LANES]
    ga_ref[...] = _sigmoid(seg(C_GA, C_GB)).astype(BF16)
    gb_ref[...] = _sigmoid(seg(C_GB, C_END)).astype(BF16)


def _inproj(x, mod, g1, w_all, b_all, tri, place, ones_row, *, nb, tm, sample, nseq):
    n = x.shape[0]
    nt = n // (nb * tm)
    row = lambda b, i: (b * nt + i, 0)
    const2 = lambda b, i: (0, 0)
    once = dict(pipeline_mode=pl.Buffered(1))
    in_specs = [pl.BlockSpec((tm, D_MODEL), row), _mod_spec(mod, 0), _mod_spec(mod, 1),
                pl.BlockSpec((1, D_MODEL), const2),
                pl.BlockSpec((C_END, D_MODEL), const2, **once),
                pl.BlockSpec((1, C_END), const2),
                pl.BlockSpec((tm, tm), const2, **once),
                pl.BlockSpec(place.shape, const2, **once),
                pl.BlockSpec(ones_row.shape, const2)]
    shared_shape = [jax.ShapeDtypeStruct((N_SLABS, n, LANES), F32),
                    jax.ShapeDtypeStruct((n, D_MODEL), BF16),
                    jax.ShapeDtypeStruct((n, D_MODEL), BF16)]
    shared_specs = [pl.BlockSpec((N_SLABS, tm, LANES), lambda b, i: (0, b * nt + i, 0)),
                    pl.BlockSpec((tm, D_MODEL), row), pl.BlockSpec((tm, D_MODEL), row)]
    if sample:
        out_shape = [jax.ShapeDtypeStruct((n, ATT_WIDTH), BF16),
                     jax.ShapeDtypeStruct((n, ATT_WIDTH), F32),
                     jax.ShapeDtypeStruct((n, ATT_WIDTH), F32),
                     jax.ShapeDtypeStruct((n, N_HEADS), F32),
                     jax.ShapeDtypeStruct((n, LANES), F32)]
        out_specs = [pl.BlockSpec((tm, ATT_WIDTH), row), pl.BlockSpec((tm, ATT_WIDTH), row),
                     pl.BlockSpec((tm, ATT_WIDTH), row), pl.BlockSpec((tm, N_HEADS), row),
                     pl.BlockSpec((tm, LANES), row)]
    else:
        n_pg, pg_tile = n // PAGE_SIZE, tm // PAGE_SIZE
        page = lambda b, i: (b * nt + i, 0, 0)
        n_feat = N_HEADS * LANES
        out_shape = [jax.ShapeDtypeStruct((nb, n_feat, n // nb), BF16),
                     jax.ShapeDtypeStruct((n, n_feat), BF16),
                     jax.ShapeDtypeStruct((n_pg, ATT_WIDTH, PAGE_SIZE), F32),
                     jax.ShapeDtypeStruct((n_pg, ATT_WIDTH, PAGE_SIZE), F32),
                     jax.ShapeDtypeStruct((nb, ATT_WIDTH, n // nb), BF16),
                     jax.ShapeDtypeStruct((n_pg, N_HEADS, PAGE_SIZE), F32)]
        out_specs = [pl.BlockSpec((None, n_feat, tm), lambda b, i: (b, 0, i)), pl.BlockSpec((tm, n_feat), row),
                     pl.BlockSpec((pg_tile, ATT_WIDTH, PAGE_SIZE), page),
                     pl.BlockSpec((pg_tile, ATT_WIDTH, PAGE_SIZE), page),
                     pl.BlockSpec((None, ATT_WIDTH, tm), lambda b, i: (b, 0, i)),
                     pl.BlockSpec((pg_tile, N_HEADS, PAGE_SIZE), page)]
    out_shape += shared_shape
    out_specs += shared_specs
    return pl.pallas_call(
        functools.partial(_inproj_kernel, sample=sample, nseq=nseq),
        out_shape=out_shape,
        grid=(nb, nt),
        in_specs=in_specs,
        out_specs=out_specs,
        scratch_shapes=[pltpu.VMEM((1, LANES), F32)],
        compiler_params=pltpu.CompilerParams(dimension_semantics=("arbitrary", "arbitrary"),
                                             vmem_limit_bytes=VMEM_LIMIT),
        name="inproj_sample" if sample else "inproj_prompt",
    )(x, mod, mod, g1, w_all, b_all, tri, place, ones_row)


def _attn_kernel(qat_ref, ka_ref, vt_ref, o_ref, s_even, s_odd, mx_even, mx_odd, *, tq, tk):
    qi = pl.program_id(2)
    s_bufs = (s_even, s_odd)
    mx_bufs = (mx_even, mx_odd)
    q_heads = [qat_ref[hh * LANES:(hh + 1) * LANES, :] for hh in range(2)]
    key = lax.broadcasted_iota(jnp.int32, (tk, ATTN_QCHUNK), 0)
    qry = lax.broadcasted_iota(jnp.int32, (tk, ATTN_QCHUNK), 1)

    units = [(hh, slice(c, c + ATTN_QCHUNK), hh * (tq // ATTN_QCHUNK) + c // ATTN_QCHUNK)
             for hh in range(2) for c in range(0, tq, ATTN_QCHUNK)]

    def scores(ki, parity, unit, diagonal):
        hh, cols, u = unit
        ka = ka_ref[pl.ds(pl.multiple_of(ki * tk, tk), tk), hh * LANES:(hh + 1) * LANES]
        st = jnp.dot(ka, q_heads[hh][:, cols], preferred_element_type=F32)
        if diagonal:
            st = jnp.where(key + parity * tk <= qry + cols.start, st, NEG_INF)
        s_bufs[parity][u] = st
        mx_bufs[parity][u] = jnp.max(st, axis=0, keepdims=True)

    ones_rows = jnp.ones((SUM_ROWS, tk), BF16)

    def softmax_pv(ki, parity, unit, carry):
        hh, _, u = unit
        m_prev, acc = carry
        m_new = jnp.maximum(m_prev, mx_bufs[parity][u])
        alpha = jnp.exp2(m_prev - m_new)
        pt = jnp.exp2(s_bufs[parity][u] - m_new).astype(BF16)
        vt = vt_ref[hh * HEAD_DIM:(hh + 1) * HEAD_DIM, pl.ds(pl.multiple_of(ki * tk, tk), tk)]
        vt = jnp.concatenate([vt, ones_rows], axis=0)
        return m_new, acc * alpha + jnp.dot(vt, pt, preferred_element_type=F32)

    def stage(k_next, p_next, k_cur, p_cur, carry, diagonal=False):
        out = []
        for unit, c in zip(units, carry):
            scores(k_next, p_next, unit, diagonal)
            out.append(softmax_pv(k_cur, p_cur, unit, c))
        return tuple(out)

    unit0 = (jnp.full((1, ATTN_QCHUNK), NEG_INF, F32), jnp.zeros((HEAD_DIM + SUM_ROWS, ATTN_QCHUNK), F32))
    for unit in units:
        scores(2 * qi, 0, unit, True)
    carry = stage(2 * qi + 1, 1, 2 * qi, 0, (unit0,) * len(units), diagonal=True)

    def pair(j, carry):
        carry = stage(2 * j, 0, jnp.where(j == 0, 2 * qi + 1, 2 * j - 1), 1, carry)
        return stage(2 * j + 1, 1, 2 * j, 0, carry)

    carry = lax.fori_loop(0, qi, pair, carry)
    last = jnp.where(qi == 0, 1, 2 * qi - 1)
    heads = [[], []]
    for unit, c in zip(units, carry):
        _, acc = softmax_pv(last, 1, unit, c)
        heads[unit[0]].append(acc[:HEAD_DIM] / acc[HEAD_DIM:HEAD_DIM + 1])
    o = jnp.concatenate([jnp.concatenate(h, axis=1) for h in heads], axis=0)
    o_ref[...] = o.T.astype(o_ref.dtype)


def _prompt_attention(qat, ka, vt, *, nb, seq, tq):
    tk = tq // 2
    ka3 = ka.reshape(nb, seq, N_HEADS * LANES)
    out = pl.pallas_call(
        functools.partial(_attn_kernel, tq=tq, tk=tk),
        out_shape=jax.ShapeDtypeStruct((nb, seq, ATT_WIDTH), BF16),
        grid=(nb, N_PAIRS, seq // tq),
        in_specs=[pl.BlockSpec((None, 2 * LANES, tq), lambda b, p, i: (b, p, i)),
                  pl.BlockSpec((None, seq, 2 * LANES), lambda b, p, i: (b, 0, p)),
                  pl.BlockSpec((None, LANES, seq), lambda b, p, i: (b, p, 0))],
        out_specs=pl.BlockSpec((None, tq, LANES), lambda b, p, i: (b, i, p)),
        scratch_shapes=([pltpu.VMEM((2 * tq // ATTN_QCHUNK, tk, ATTN_QCHUNK), F32)] * 2
                        + [pltpu.VMEM((2 * tq // ATTN_QCHUNK, 1, ATTN_QCHUNK), F32)] * 2),
        compiler_params=pltpu.CompilerParams(dimension_semantics=("arbitrary", "arbitrary", "arbitrary"),
                                             vmem_limit_bytes=VMEM_LIMIT),
        name="prompt_attention",
    )(qat, ka3, vt)
    return out.reshape(nb * seq, ATT_WIDTH)


class _SampleAttention:
    SCORE_ROWS = N_HEADS * SAMPLE_T
    NT_DIMS = (((1,), (1,)), ((), ()))

    def __init__(self, q_scr, m_scr, l_scr, acc_scr, carry_scr):
        self.q_scr, self.m_scr, self.l_scr, self.acc_scr, self.carry_scr = q_scr, m_scr, l_scr, acc_scr, carry_scr

    @staticmethod
    def scratch_shapes():
        rows = _SampleAttention.SCORE_ROWS
        return [pltpu.VMEM((rows, ATT_WIDTH), BF16), pltpu.VMEM((rows, 1), F32), pltpu.VMEM((rows, 1), F32),
                pltpu.VMEM((rows, ATT_WIDTH), F32), pltpu.VMEM((N_HEADS, LANES), F32)]

    @staticmethod
    def per_head_rows(a):
        return jnp.concatenate([jnp.broadcast_to(a[h:h + 1, :], (SAMPLE_T, a.shape[1])) for h in range(N_HEADS)],
                               axis=0)

    def update(self, s, pv):
        m_prev = self.m_scr[...]
        m_new = jnp.maximum(m_prev, jnp.max(s, axis=-1, keepdims=True))
        alpha = jnp.exp(m_prev - m_new)
        p = jnp.exp(s - m_new)
        self.l_scr[...] = alpha * self.l_scr[...] + jnp.sum(p, axis=-1, keepdims=True)
        self.acc_scr[...] = alpha * self.acc_scr[...] + pv(p.astype(BF16))
        self.m_scr[...] = m_new

    def start(self, q_ref, kn_ref, vn_ref, cn_ref):
        rows, q_scr, nt_dims, per_head_rows = self.SCORE_ROWS, self.q_scr, self.NT_DIMS, self.per_head_rows
        m_scr, l_scr, acc_scr, carry_scr, update = self.m_scr, self.l_scr, self.acc_scr, self.carry_scr, self.update
        q8 = q_ref[...].astype(F32)
        qt = jnp.concatenate([q8] * N_HEADS, axis=0)
        r_i = lax.broadcasted_iota(jnp.int32, (rows, ATT_WIDTH), 0)
        c_i = lax.broadcasted_iota(jnp.int32, (rows, ATT_WIDTH), 1)
        q_scr[...] = jnp.where(r_i // SAMPLE_T == c_i // HEAD_DIM, qt, 0.0).astype(BF16)
        m_scr[...] = jnp.full_like(m_scr, NEG_INF)
        l_scr[...] = jnp.zeros_like(l_scr)
        acc_scr[...] = jnp.zeros_like(acc_scr)
        carry_scr[...] = jnp.zeros_like(carry_scr)
        pad = jnp.zeros((PAGE_SIZE - SAMPLE_T, ATT_WIDTH), F32)
        k_new = jnp.concatenate([kn_ref[...], pad], axis=0).astype(BF16)
        v_new = jnp.concatenate([vn_ref[...], pad], axis=0).astype(BF16)
        c_new = jnp.concatenate([cn_ref[...], jnp.zeros((PAGE_SIZE - SAMPLE_T, LANES), F32)], axis=0)
        s = lax.dot_general(q_scr[...], k_new, nt_dims, preferred_element_type=F32)
        s = s - per_head_rows(c_new.T[:N_HEADS, :])
        r_s = lax.broadcasted_iota(jnp.int32, (rows, PAGE_SIZE), 0)
        c_s = lax.broadcasted_iota(jnp.int32, (rows, PAGE_SIZE), 1)
        update(jnp.where(c_s <= r_s % SAMPLE_T, s, NEG_INF),
               lambda p: jnp.dot(p, v_new, preferred_element_type=F32))

    def pages(self, k_pages, v_pages, lf_pages, tri_ref):
        pieces = []
        for lf in lf_pages:
            hi = lf.astype(BF16).astype(F32)
            mid = (lf - hi).astype(BF16).astype(F32)
            lo = (lf - hi - mid).astype(BF16).astype(F32)
            pieces += [hi, mid, lo]
        sums = jnp.dot(jnp.concatenate(pieces, axis=0).astype(BF16), tri_ref[...], preferred_element_type=F32)
        carry = self.carry_scr[...]
        bias = []
        for i in range(len(lf_pages)):
            part = [sums[(3 * i + j) * N_HEADS:(3 * i + j + 1) * N_HEADS, :] for j in range(AUG)]
            local = part[0] + part[1] + part[2]
            bias.append(local[:, :PAGE_SIZE] + carry)
            carry = carry + local[:, PAGE_SIZE:]
        self.carry_scr[...] = carry
        kt = jnp.concatenate([k.astype(BF16) for k in k_pages], axis=1)
        vt = jnp.concatenate([v.astype(BF16) for v in v_pages], axis=1)
        s = jnp.dot(self.q_scr[...], kt, preferred_element_type=F32)
        s = s + self.per_head_rows(jnp.concatenate(bias, axis=1))
        self.update(s, lambda p: lax.dot_general(p, vt, self.NT_DIMS, preferred_element_type=F32))

    def finish(self, o_ref):
        rows = self.SCORE_ROWS
        o = self.acc_scr[...] / self.l_scr[...]
        r_i = lax.broadcasted_iota(jnp.int32, (rows, ATT_WIDTH), 0)
        c_i = lax.broadcasted_iota(jnp.int32, (rows, ATT_WIDTH), 1)
        o = jnp.where(r_i // SAMPLE_T == c_i // HEAD_DIM, o, 0.0).astype(BF16)
        t_i = lax.broadcasted_iota(jnp.int32, (SAMPLE_T, rows), 0)
        r_j = lax.broadcasted_iota(jnp.int32, (SAMPLE_T, rows), 1)
        sel = jnp.where(r_j % SAMPLE_T == t_i, 1.0, 0.0).astype(BF16)
        o_ref[...] = jnp.dot(sel, o, preferred_element_type=F32).astype(o_ref.dtype)


def _merge_tail(rows, conv, g1, x_ref, o_ref, ga_ref, gb_ref, dwb_ref, lng_ref, lnb_ref, wpa_ref, wpb_ref, bpb_ref,
                wo_ref, out_ref):
    yb = conv + dwb_ref[...]
    mu = jnp.mean(yb, axis=-1, keepdims=True)
    var = jnp.mean(jnp.square(yb - mu), axis=-1, keepdims=True)
    yb = (yb - mu) * lax.rsqrt(var + EPS) * lng_ref[...] + lnb_ref[...]
    yb = jnp.dot(_silu(yb).astype(BF16), wpb_ref[...], preferred_element_type=F32) + bpb_ref[...]
    ya = jnp.dot(o_ref[rows, :], wpa_ref[...], preferred_element_type=F32)
    m = ga_ref[rows, :].astype(F32) * ya + gb_ref[rows, :].astype(F32) * yb
    out_ref[rows, :] = x_ref[rows, :] + g1 * jnp.dot(m.astype(BF16), wo_ref[...], preferred_element_type=F32)


def _merge_prompt_kernel(x_ref, o_ref, ucur_ref, uprev_ref, ga_ref, gb_ref, g1_ref, dww_ref, dwb_ref, lng_ref,
                         lnb_ref, wpa_ref, wpb_ref, bpb_ref, wo_ref, out_ref, ubuf, *, nseq, row_block):
    tm = x_ref.shape[0]
    first = pl.program_id(1) == 0
    prev = uprev_ref[...]
    ubuf[:, 0:CONV_HALO, :] = jnp.where(first, jnp.zeros_like(prev), prev)
    ubuf[:, CONV_HALO:, :] = ucur_ref[...]
    g1 = _mod_rows(g1_ref, nseq, False)
    chunk = 64
    for b0 in range(0, tm, row_block):
        slabs = []
        for c in range(N_SLABS):
            pieces = []
            for r0 in range(b0, b0 + row_block, chunk):
                acc = jnp.zeros((chunk, LANES), F32)
                for j in range(CONV_K):
                    off = r0 + CONV_HALO - (CONV_K - 1) + j
                    acc = acc + dww_ref[c, j:j + 1, :] * ubuf[c, off:off + chunk, :]
                pieces.append(acc)
            slabs.append(jnp.concatenate(pieces, axis=0))
        _merge_tail(slice(b0, b0 + row_block), jnp.concatenate(slabs, axis=1), g1, x_ref, o_ref, ga_ref, gb_ref,
                    dwb_ref, lng_ref, lnb_ref, wpa_ref, wpb_ref, bpb_ref, wo_ref, out_ref)


def _merge_sample_kernel(x_ref, o_ref, uwin_ref, ga_ref, gb_ref, g1_ref, dww_ref, dwb_ref, lng_ref,
                         lnb_ref, wpa_ref, wpb_ref, bpb_ref, wo_ref, out_ref, *, nseq):
    slabs = []
    for c in range(N_SLABS):
        acc = jnp.zeros((nseq, SAMPLE_T, LANES), F32)
        for j in range(CONV_K):
            acc = acc + dww_ref[c, j:j + 1, :] * uwin_ref[c, :, j:j + SAMPLE_T, :]
        slabs.append(acc.reshape(nseq * SAMPLE_T, LANES))
    _merge_tail(slice(None), jnp.concatenate(slabs, axis=1), _mod_rows(g1_ref, nseq, True), x_ref, o_ref, ga_ref,
                gb_ref, dwb_ref, lng_ref, lnb_ref, wpa_ref, wpb_ref, bpb_ref, wo_ref, out_ref)


def _merge(x, o, u_args, ga, gb, mod, weights, *, nb, tm, sample, nseq):
    n = x.shape[0]
    nt = n // (nb * tm)
    row = lambda b, i: (b * nt + i, 0)
    const2 = lambda b, i: (0, 0)
    const3 = lambda b, i: (0, 0, 0)
    once = dict(pipeline_mode=pl.Buffered(1))
    mod_spec = _mod_spec(mod, 2)
    if sample:
        (uwin,) = u_args
        u_specs = [pl.BlockSpec(uwin.shape, lambda b, i: (0, 0, 0, 0))]
        kernel, scratch = functools.partial(_merge_sample_kernel, nseq=nseq), []
    else:
        (u,) = u_args
        u_args = (u, u)
        blocks_per_tile = tm // CONV_HALO
        u_specs = [pl.BlockSpec((N_SLABS, tm, LANES), lambda b, i: (0, b * nt + i, 0)),
                   pl.BlockSpec((N_SLABS, CONV_HALO, LANES),
                                lambda b, i: (0, jnp.maximum((b * nt + i) * blocks_per_tile - 1, 0), 0))]
        kernel = functools.partial(_merge_prompt_kernel, nseq=nseq, row_block=tm // 2)
        scratch = [pltpu.VMEM((N_SLABS, CONV_HALO + tm, LANES), F32)]
    dww, dwb, lng, lnb, wpa, wpb, bpb, wo = weights
    in_specs = ([pl.BlockSpec((tm, D_MODEL), row), pl.BlockSpec((tm, ATT_WIDTH), row)] + u_specs
                + [pl.BlockSpec((tm, D_MODEL), row), pl.BlockSpec((tm, D_MODEL), row), mod_spec,
                   pl.BlockSpec(dww.shape, const3),
                   pl.BlockSpec((1, CONV_WIDTH), const2), pl.BlockSpec((1, CONV_WIDTH), const2),
                   pl.BlockSpec((1, CONV_WIDTH), const2),
                   pl.BlockSpec((ATT_WIDTH, D_MODEL), const2, **once),
                   pl.BlockSpec((CONV_WIDTH, D_MODEL), const2, **once),
                   pl.BlockSpec((1, D_MODEL), const2),
                   pl.BlockSpec((D_MODEL, D_MODEL), const2, **once)])
    return pl.pallas_call(
        kernel,
        out_shape=jax.ShapeDtypeStruct((n, D_MODEL), F32),
        grid=(nb, nt),
        in_specs=in_specs,
        out_specs=pl.BlockSpec((tm, D_MODEL), row),
        scratch_shapes=scratch,
        compiler_params=pltpu.CompilerParams(dimension_semantics=("arbitrary", "arbitrary"),
                                             vmem_limit_bytes=VMEM_LIMIT),
        name="merge_sample" if sample else "merge_prompt",
    )(x, o, *u_args, ga, gb, mod, dww, dwb, lng, lnb, wpa, wpb, bpb, wo)


def _ffn_steps(x_ref, sh_ref, sc_ref, g2_ref, rg_ref, fg_ref, win_ref, wout_ref, out_ref, *, bounds, sample, nseq):
    x = x_ref[...]
    ms = jnp.mean(x * x, axis=-1, keepdims=True)
    h = x * lax.rsqrt(ms + EPS) * rg_ref[...]
    hb = (h * (1.0 + _mod_rows(sc_ref, nseq, sample)) + _mod_rows(sh_ref, nseq, sample)).astype(BF16)
    acc = jnp.zeros(x.shape, F32)
    for lo, hi in zip(bounds[:-1], bounds[1:]):
        gate = jnp.dot(hb, win_ref[:, lo:hi], preferred_element_type=F32)
        up = jnp.dot(hb, win_ref[:, FFN_HIDDEN + lo:FFN_HIDDEN + hi], preferred_element_type=F32)
        act = (_silu(gate) * up).astype(BF16)
        acc = acc + jnp.dot(act, wout_ref[lo:hi, :], preferred_element_type=F32)
        yield
    x2 = x + _mod_rows(g2_ref, nseq, sample) * acc
    ms2 = jnp.mean(x2 * x2, axis=-1, keepdims=True)
    out_ref[...] = x2 * lax.rsqrt(ms2 + EPS) * fg_ref[...]
    yield


def _hidden_bounds(n_chunks):
    tiles = FFN_HIDDEN // LANES
    return [LANES * ((tiles * k) // n_chunks) for k in range(n_chunks + 1)]


def _ffn_kernel(*refs, sample, nseq):
    for _ in _ffn_steps(*refs, bounds=_hidden_bounds(2), sample=sample, nseq=nseq):
        pass


def _ffn_attn_kernel(pt_ref, x_ref, sh_ref, sc_ref, g2_ref, rg_ref, fg_ref, win_ref, wout_ref,
                     q_ref, kn_ref, vn_ref, cn_ref, tri_ref, ck_hbm, cv_hbm, clf_hbm,
                     out_ref, o_ref, kbuf, vbuf, lbuf, sem, *state, nseq, n_pages):
    npg = PAGES_PER_STEP
    n_chunks = n_pages // npg
    seq = pl.program_id(0) * pl.num_programs(1) + pl.program_id(1)
    attn = _SampleAttention(*state)

    def copies(c):
        slot, out = c % 2, []
        for j in range(npg):
            page = pt_ref[seq * n_pages + (n_pages - 1) - (c * npg + j)]
            out += [pltpu.make_async_copy(ck_hbm.at[page], kbuf.at[slot, j], sem.at[0, slot]),
                    pltpu.make_async_copy(cv_hbm.at[page], vbuf.at[slot, j], sem.at[1, slot]),
                    pltpu.make_async_copy(clf_hbm.at[page], lbuf.at[slot, j], sem.at[2, slot])]
        return out

    for cp in copies(0):
        cp.start()
    attn.start(q_ref, kn_ref, vn_ref, cn_ref)
    ffn = _ffn_steps(x_ref, sh_ref, sc_ref, g2_ref, rg_ref, fg_ref, win_ref, wout_ref, out_ref,
                     bounds=_hidden_bounds(n_chunks), sample=False, nseq=nseq)
    for c in range(n_chunks):
        next(ffn)
        if c + 1 < n_chunks:
            for cp in copies(c + 1):
                cp.start()
        for cp in copies(c):
            cp.wait()
        slot = c % 2
        attn.pages([kbuf[slot, j] for j in range(npg)], [vbuf[slot, j] for j in range(npg)],
                   [lbuf[slot, j] for j in range(npg)], tri_ref)
    next(ffn)
    attn.finish(o_ref)


def _ffn(x, mod, rms_g, final_g, w_in, w_out, *, nb, tm, sample, nseq):
    n = x.shape[0]
    nt = n // (nb * tm)
    row = lambda b, i: (b * nt + i, 0)
    const2 = lambda b, i: (0, 0)
    once = dict(pipeline_mode=pl.Buffered(1))
    return pl.pallas_call(
        functools.partial(_ffn_kernel, sample=sample, nseq=nseq),
        out_shape=jax.ShapeDtypeStruct((n, D_MODEL), F32),
        grid=(nb, nt),
        in_specs=[pl.BlockSpec((tm, D_MODEL), row), _mod_spec(mod, 3), _mod_spec(mod, 4), _mod_spec(mod, 5),
                  pl.BlockSpec((1, D_MODEL), const2), pl.BlockSpec((1, D_MODEL), const2),
                  pl.BlockSpec((D_MODEL, 2 * FFN_HIDDEN), const2, **once),
                  pl.BlockSpec((FFN_HIDDEN, D_MODEL), const2, **once)],
        out_specs=pl.BlockSpec((tm, D_MODEL), row),
        compiler_params=pltpu.CompilerParams(dimension_semantics=("arbitrary", "arbitrary"),
                                             vmem_limit_bytes=VMEM_LIMIT),
        name="ffn_sample" if sample else "ffn_prompt",
    )(x, mod, mod, mod, rms_g, final_g, w_in, w_out)


def _ffn_with_sample_attention(x, mod, rms_g, final_g, w_in, w_out, page_table, q, k_new, v_new, c_new, tri_page,
                               cache_k, cache_v, cache_logf, *, nb, tm):
    n = x.shape[0]
    nt = n // (nb * tm)
    nseq, n_pages = page_table.shape
    assert nseq == nb * nt and n_pages % PAGES_PER_STEP == 0
    row = lambda b, i, pt: (b * nt + i, 0)
    const2 = lambda b, i, pt: (0, 0)
    once = dict(pipeline_mode=pl.Buffered(1))
    mod_spec = lambda kind: pl.BlockSpec((None, mod.shape[1], D_MODEL), lambda b, i, pt: (kind, 0, 0))
    seq_spec = lambda width: pl.BlockSpec((None, SAMPLE_T, width), lambda b, i, pt: (b * nt + i, 0, 0))
    hbm = pl.BlockSpec(memory_space=pl.ANY)
    slots = 2
    grid_spec = pltpu.PrefetchScalarGridSpec(
        num_scalar_prefetch=1,
        grid=(nb, nt),
        in_specs=[pl.BlockSpec((tm, D_MODEL), row), mod_spec(3), mod_spec(4), mod_spec(5),
                  pl.BlockSpec((1, D_MODEL), const2), pl.BlockSpec((1, D_MODEL), const2),
                  pl.BlockSpec((D_MODEL, 2 * FFN_HIDDEN), const2, **once),
                  pl.BlockSpec((FFN_HIDDEN, D_MODEL), const2, **once),
                  seq_spec(ATT_WIDTH), seq_spec(ATT_WIDTH), seq_spec(ATT_WIDTH), seq_spec(LANES),
                  pl.BlockSpec((PAGE_SIZE, 2 * PAGE_SIZE), const2), hbm, hbm, hbm],
        out_specs=[pl.BlockSpec((tm, D_MODEL), row), seq_spec(ATT_WIDTH)],
        scratch_shapes=[pltpu.VMEM((slots, PAGES_PER_STEP, ATT_WIDTH, PAGE_SIZE), F32),
                        pltpu.VMEM((slots, PAGES_PER_STEP, ATT_WIDTH, PAGE_SIZE), F32),
                        pltpu.VMEM((slots, PAGES_PER_STEP, N_HEADS, PAGE_SIZE), F32),
                        pltpu.SemaphoreType.DMA((3, slots))] + _SampleAttention.scratch_shapes())
    return pl.pallas_call(
        functools.partial(_ffn_attn_kernel, nseq=nseq, n_pages=n_pages),
        out_shape=[jax.ShapeDtypeStruct((n, D_MODEL), F32),
                   jax.ShapeDtypeStruct((nseq, SAMPLE_T, ATT_WIDTH), BF16)],
        grid_spec=grid_spec,
        compiler_params=pltpu.CompilerParams(dimension_semantics=("arbitrary", "arbitrary"),
                                             vmem_limit_bytes=VMEM_LIMIT),
        name="ffn_prompt_sample_attention",
    )(page_table.reshape(-1), x, mod, mod, mod, rms_g, final_g, w_in, w_out, q, k_new, v_new, c_new, tri_page,
      cache_k, cache_v, cache_logf)


def _bias_placement():
    half = N_HEADS * LANES
    place = np.zeros((LANES, 2 * half), np.float32)
    ones = np.zeros((1, 2 * half), np.float32)
    for h in range(N_HEADS):
        base = h * LANES + (HEAD_DIM if h % 2 == 0 else 0)
        for piece in range(AUG):
            place[piece * N_HEADS + h, base + piece] = 1.0
            ones[0, base + AUG + piece] = 1.0
            ones[0, half + base + piece] = 1.0
            place[piece * N_HEADS + h, half + base + AUG + piece] = -1.0
    return jnp.asarray(place, BF16), jnp.asarray(ones, F32)


def _lower_tri(n, block):
    t = np.arange(n)[:, None]
    s = np.arange(n)[None, :]
    return jnp.asarray(((s <= t) & (t // block == s // block)).astype(np.float32), BF16)


def _later_keys(n):
    j = np.arange(n)[:, None]
    s = np.arange(n)[None, :]
    return jnp.asarray(np.concatenate([(j > s).astype(np.float32), np.ones((n, n), np.float32)], axis=1), BF16)


def kernel(x_prompt, x_sample, c_prompt, c_sample, cache_k, cache_v, cache_logf, state_conv, page_table, rms1_g, rms2_g, w_ada, b_ada, w_in, b_in, dw_w, dw_b, ln_g, ln_b, w_pa, w_pb, b_pb, w_o, w_ffn_in, w_ffn_out, final_g):
    nb, seq, _ = x_prompt.shape
    nseq, dec_t, _ = x_sample.shape
    depth = w_in.shape[0]
    assert depth == 1 and dec_t <= SAMPLE_T
    n_prompt = nb * seq
    tm = 512
    n_sample = nseq * SAMPLE_T

    wt, b = w_in[0].T, b_in[0]
    g_off = 3 * ATT_WIDTH + N_HEADS
    w_all = jnp.concatenate([wt[:g_off], jnp.zeros((LANES - N_HEADS, D_MODEL), F32), wt[g_off:]], axis=0).astype(BF16)
    b_all = jnp.concatenate([b[:g_off], jnp.zeros((LANES - N_HEADS,), F32), b[g_off:]])[None, :]
    dww = jnp.pad(dw_w[0], ((0, CONV_HALO - CONV_K), (0, 0))).reshape(CONV_HALO, N_SLABS, LANES).transpose(1, 0, 2)
    merge_w = (dww, dw_b[0][None, :], ln_g[0][None, :], ln_b[0][None, :], w_pa[0].astype(BF16),
               w_pb[0].astype(BF16), b_pb[0][None, :], w_o[0].astype(BF16))
    wf_in, wf_out = w_ffn_in[0].astype(BF16), w_ffn_out[0].astype(BF16)
    g1w, g2w, gfw = rms1_g[0][None, :], rms2_g[0][None, :], final_g[None, :]
    place, ones_row = _bias_placement()

    n_cond = nb + nseq
    c_all = jnp.pad(jnp.concatenate([c_sample, c_prompt], axis=0), ((0, -n_cond % 8), (0, 0)))
    mod = _modulation(c_all, w_ada[0], b_ada[0][None, :])

    xs = jnp.pad(x_sample, ((0, 0), (0, SAMPLE_T - dec_t), (0, 0))).reshape(n_sample, D_MODEL)
    (q_s, k_s, v_s, lf_s, c_s, u_s, ga_s, gb_s) = _inproj(
        xs, mod, g1w, w_all, b_all, _lower_tri(n_sample, SAMPLE_T), place, ones_row,
        nb=1, tm=n_sample, sample=True, nseq=nseq)

    xp = x_prompt.reshape(n_prompt, D_MODEL)
    (qa, ka, kt_p, vt_p, vb_p, lft_p, u_p, ga_p, gb_p) = _inproj(
        xp, mod, g1w, w_all, b_all, _lower_tri(tm, tm), place, ones_row, nb=nb, tm=tm, sample=False, nseq=nseq)
    o_p = _prompt_attention(qa, ka, vb_p, nb=nb, seq=seq, tq=2 * tm)
    x1_p = _merge(xp, o_p, (u_p,), ga_p, gb_p, mod, merge_w, nb=nb, tm=tm, sample=False, nseq=nseq)
    n_phys = cache_k.shape[1]
    page_t = lambda c: c[0].transpose(0, 2, 3, 1).reshape(n_phys, ATT_WIDTH, PAGE_SIZE)
    y_p, o_s = _ffn_with_sample_attention(
        x1_p, mod, g2w, gfw, wf_in, wf_out,
        page_table, q_s.reshape(nseq, SAMPLE_T, ATT_WIDTH), k_s.reshape(nseq, SAMPLE_T, ATT_WIDTH),
        v_s.reshape(nseq, SAMPLE_T, ATT_WIDTH), c_s.reshape(nseq, SAMPLE_T, LANES), _later_keys(PAGE_SIZE),
        page_t(cache_k), page_t(cache_v), cache_logf[0].transpose(0, 2, 1), nb=nb, tm=tm)

    state_slabs = state_conv[0].reshape(nseq, CONV_K - 1, N_SLABS, LANES).transpose(2, 0, 1, 3)
    u_slabs = u_s.reshape(N_SLABS, nseq, SAMPLE_T, LANES)
    uwin = jnp.concatenate(
        [state_slabs, u_slabs, jnp.zeros((N_SLABS, nseq, SAMPLE_WIN - (CONV_K - 1) - SAMPLE_T, LANES), F32)], axis=2)
    x1_s = _merge(xs, o_s.reshape(n_sample, ATT_WIDTH), (uwin,), ga_s, gb_s, mod, merge_w,
                  nb=1, tm=n_sample, sample=True, nseq=nseq)
    y_s = _ffn(x1_s, mod, g2w, gfw, wf_in, wf_out, nb=1, tm=n_sample, sample=True, nseq=nseq)

    n_pg = seq // PAGE_SIZE
    tail = CONV_K - 1
    u_tail = u_p.reshape(N_SLABS, nb, seq, LANES)[:, :, seq - tail:]
    u_tail = u_tail.transpose(1, 2, 0, 3).reshape(nb, tail, CONV_WIDTH)
    us_rows = u_s.reshape(N_SLABS, nseq, SAMPLE_T, LANES)[:, :, :dec_t]
    us_rows = us_rows.transpose(1, 2, 0, 3).reshape(nseq, dec_t, CONV_WIDTH)
    unpad = lambda a, width: a.reshape(nseq, SAMPLE_T, *width)[:, :dec_t]
    return (y_p.reshape(nb, seq, D_MODEL),
            unpad(y_s, (D_MODEL,)),
            kt_p.reshape(1, nb, n_pg, N_HEADS, HEAD_DIM, PAGE_SIZE).transpose(0, 1, 2, 5, 3, 4),
            vt_p.reshape(1, nb, n_pg, N_HEADS, HEAD_DIM, PAGE_SIZE).transpose(0, 1, 2, 5, 3, 4),
            lft_p.reshape(1, nb, n_pg, N_HEADS, PAGE_SIZE).transpose(0, 1, 2, 4, 3),
            u_tail[None],
            unpad(k_s, (N_HEADS, HEAD_DIM))[None],
            unpad(v_s, (N_HEADS, HEAD_DIM))[None],
            unpad(lf_s, (N_HEADS,))[None],
            jnp.concatenate([state_conv[0][:, dec_t:], us_rows], axis=1)[None])
```

```python
import functools

import numpy as np
import jax
import jax.numpy as jnp
from jax import lax
from jax.experimental import pallas as pl
from jax.experimental.pallas import tpu as pltpu

F32 = jnp.float32
BF16 = jnp.bfloat16

D_MODEL = 1024
N_HEADS = 8
HEAD_DIM = 64
ATT_WIDTH = N_HEADS * HEAD_DIM
CONV_WIDTH = 512
CONV_K = 31
FFN_HIDDEN = 2816
PAGE_SIZE = 128
EPS = 1e-6
NEG_INF = -1e30
SCALE = HEAD_DIM ** -0.5
LOG2E = 1.4426950408889634

LANES = 128
N_PAIRS = N_HEADS // 2
N_SLABS = CONV_WIDTH // LANES
AUG = 3
SUM_ROWS = 16
ATTN_QCHUNK = 256
SAMPLE_T = 8
CONV_HALO = 32
SAMPLE_WIN = 40
PAGES_PER_STEP = 16
VMEM_LIMIT = 56 * 1024 * 1024

C_Q, C_K, C_V, C_F, C_GLA, C_GLB, C_GA, C_GB, C_END = 0, 512, 1024, 1536, 1664, 2176, 2688, 3712, 4736


def _sigmoid(x):
    return 1.0 / (1.0 + jnp.exp(-x))


def _silu(x):
    return x * _sigmoid(x)


def _split3_packed(a, lane):
    a = jnp.where(lane < N_HEADS, a, 0.0)
    hi = a.astype(BF16).astype(F32)
    r1 = a - hi
    mid = r1.astype(BF16).astype(F32)
    lo = (r1 - mid).astype(BF16).astype(F32)
    packed = hi + pltpu.roll(mid, N_HEADS, axis=1) + pltpu.roll(lo, 2 * N_HEADS, axis=1)
    return packed.astype(BF16)


def _unpack3(p, lane):
    s = p + pltpu.roll(p, LANES - N_HEADS, axis=1) + pltpu.roll(p, LANES - 2 * N_HEADS, axis=1)
    return jnp.where(lane < N_HEADS, s, 0.0)


def _mod_kernel(c_ref, w_ref, b_ref, o_ref):
    s = _silu(c_ref[...]).astype(BF16)
    o_ref[...] = jnp.dot(s, w_ref[...].astype(BF16), preferred_element_type=F32) + b_ref[...]


def _modulation(c_all, w_ada, b_ada):
    rows = c_all.shape[0]
    n_mod = w_ada.shape[1] // D_MODEL
    return pl.pallas_call(
        _mod_kernel,
        out_shape=jax.ShapeDtypeStruct((n_mod, rows, D_MODEL), F32),
        grid=(n_mod,),
        in_specs=[pl.BlockSpec((rows, D_MODEL), lambda j: (0, 0)),
                  pl.BlockSpec((D_MODEL, D_MODEL), lambda j: (0, j)),
                  pl.BlockSpec((1, D_MODEL), lambda j: (0, j))],
        out_specs=pl.BlockSpec((None, rows, D_MODEL), lambda j: (j, 0, 0)),
        compiler_params=pltpu.CompilerParams(dimension_semantics=("arbitrary",), vmem_limit_bytes=VMEM_LIMIT),
        name="modulation",
    )(c_all, w_ada, b_ada)


def _mod_rows(ref, nseq, sample):
    if sample:
        m = ref[0:nseq, :]
        return jnp.broadcast_to(m[:, None, :], (nseq, SAMPLE_T, D_MODEL)).reshape(nseq * SAMPLE_T, D_MODEL)
    return ref[pl.ds(nseq + pl.program_id(0), 1), :]


def _mod_spec(mod, kind):
    return pl.BlockSpec((None, mod.shape[1], D_MODEL), lambda b, i: (kind, 0, 0))


def _inproj_kernel(x_ref, sh_ref, sc_ref, g_ref, w_ref, b_ref, tri_ref, place_ref, ones_ref, *refs, sample, nseq):
    if sample:
        qp_ref, k_ref, v_ref, lf_ref, c_ref, u_ref, ga_ref, gb_ref, carry_ref = refs
    else:
        qat_ref, ka_ref, kt_ref, vt_ref, vb_ref, lft_ref, u_ref, ga_ref, gb_ref, carry_ref = refs
    tm = x_ref.shape[0]

    @pl.when(pl.program_id(1) == 0)
    def _():
        carry_ref[...] = jnp.zeros_like(carry_ref)

    x = x_ref[...]
    ms = jnp.mean(x * x, axis=-1, keepdims=True)
    h = x * lax.rsqrt(ms + EPS) * g_ref[...]
    h = h * (1.0 + _mod_rows(sc_ref, nseq, sample)) + _mod_rows(sh_ref, nseq, sample)
    hb = h.astype(BF16)

    def seg(lo, hi):
        z = lax.dot_general(hb, w_ref[lo:hi, :], (((1,), (1,)), ((), ())), preferred_element_type=F32)
        return z + b_ref[:, lo:hi]

    lane = lax.broadcasted_iota(jnp.int32, (tm, LANES), 1)
    zf = seg(C_F, C_GLA)
    lf = jnp.minimum(zf, 0.0) - jnp.log1p(jnp.exp(-jnp.abs(zf)))
    lf = jnp.where(lane < N_HEADS, lf, 0.0)
    csum = _unpack3(jnp.dot(tri_ref[...], _split3_packed(lf, lane), preferred_element_type=F32), lane)
    csum = csum + carry_ref[...]
    carry_ref[...] = csum[tm - 1:tm, :]

    zq = seg(C_Q, C_K)
    zk = seg(C_K, C_V)
    zv = seg(C_V, C_F)
    if sample:
        qp_ref[...] = (zq * SCALE).astype(BF16)
        k_ref[...] = zk
        v_ref[...] = zv
        lf_ref[...] = lf[:, :N_HEADS]
        c_ref[...] = csum
    else:
        for pg in range(tm // PAGE_SIZE):
            rows = slice(pg * PAGE_SIZE, (pg + 1) * PAGE_SIZE)
            kt_ref[pg] = zk[rows, :].T
            vt_page = zv[rows, :].T
            vt_ref[pg] = vt_page
            vb_ref[:, rows] = vt_page.astype(BF16)
            lft_ref[pg] = lf[rows, :].T[:N_HEADS, :]
        zq = zq * (SCALE * LOG2E)
        aug = jnp.dot(_split3_packed(csum * LOG2E, lane), place_ref[...], preferred_element_type=F32) + ones_ref[...]
        for h in range(N_HEADS):
            own_half = (lane < HEAD_DIM) if h % 2 == 0 else (lane >= HEAD_DIM)
            pair, feat = slice((h // 2) * LANES, (h // 2 + 1) * LANES), slice(h * LANES, (h + 1) * LANES)
            qat_ref[feat, :] = jnp.where(own_half, zq[:, pair], aug[:, feat]).T.astype(BF16)
            k_feat = jnp.where(own_half, zk[:, pair], aug[:, N_HEADS * LANES + h * LANES:N_HEADS * LANES + (h + 1) * LANES])
            ka_ref[:, feat] = k_feat.astype(BF16)

    u = seg(C_GLA, C_GLB) * _sigmoid(seg(C_GLB, C_GA))
    for c in range(N_SLABS):
        u_ref[c] = u[:, c * LANES:(c + 1) * LANES]
    ga_ref[...] = _sigmoid(seg(C_GA, C_GB)).astype(BF16)
    gb_ref[...] = _sigmoid(seg(C_GB, C_END)).astype(BF16)


def _inproj(x, mod, g1, w_all, b_all, tri, place, ones_row, *, nb, tm, sample, nseq):
    n = x.shape[0]
    nt = n // (nb * tm)
    row = lambda b, i: (b * nt + i, 0)
    const2 = lambda b, i: (0, 0)
    once = dict(pipeline_mode=pl.Buffered(1))
    in_specs = [pl.BlockSpec((tm, D_MODEL), row), _mod_spec(mod, 0), _mod_spec(mod, 1),
                pl.BlockSpec((1, D_MODEL), const2),
                pl.BlockSpec((C_END, D_MODEL), const2, **once),
                pl.BlockSpec((1, C_END), const2),
                pl.BlockSpec((tm, tm), const2, **once),
                pl.BlockSpec(place.shape, const2, **once),
                pl.BlockSpec(ones_row.shape, const2)]
    shared_shape = [jax.ShapeDtypeStruct((N_SLABS, n, LANES), F32),
                    jax.ShapeDtypeStruct((n, D_MODEL), BF16),
                    jax.ShapeDtypeStruct((n, D_MODEL), BF16)]
    shared_specs = [pl.BlockSpec((N_SLABS, tm, LANES), lambda b, i: (0, b * nt + i, 0)),
                    pl.BlockSpec((tm, D_MODEL), row), pl.BlockSpec((tm, D_MODEL), row)]
    if sample:
        out_shape = [jax.ShapeDtypeStruct((n, ATT_WIDTH), BF16),
                     jax.ShapeDtypeStruct((n, ATT_WIDTH), F32),
                     jax.ShapeDtypeStruct((n, ATT_WIDTH), F32),
                     jax.ShapeDtypeStruct((n, N_HEADS), F32),
                     jax.ShapeDtypeStruct((n, LANES), F32)]
        out_specs = [pl.BlockSpec((tm, ATT_WIDTH), row), pl.BlockSpec((tm, ATT_WIDTH), row),
                     pl.BlockSpec((tm, ATT_WIDTH), row), pl.BlockSpec((tm, N_HEADS), row),
                     pl.BlockSpec((tm, LANES), row)]
    else:
        n_pg, pg_tile = n // PAGE_SIZE, tm // PAGE_SIZE
        page = lambda b, i: (b * nt + i, 0, 0)
        n_feat = N_HEADS * LANES
        out_shape = [jax.ShapeDtypeStruct((nb, n_feat, n // nb), BF16),
                     jax.ShapeDtypeStruct((n, n_feat), BF16),
                     jax.ShapeDtypeStruct((n_pg, ATT_WIDTH, PAGE_SIZE), F32),
                     jax.ShapeDtypeStruct((n_pg, ATT_WIDTH, PAGE_SIZE), F32),
                     jax.ShapeDtypeStruct((nb, ATT_WIDTH, n // nb), BF16),
                     jax.ShapeDtypeStruct((n_pg, N_HEADS, PAGE_SIZE), F32)]
        out_specs = [pl.BlockSpec((None, n_feat, tm), lambda b, i: (b, 0, i)), pl.BlockSpec((tm, n_feat), row),
                     pl.BlockSpec((pg_tile, ATT_WIDTH, PAGE_SIZE), page),
                     pl.BlockSpec((pg_tile, ATT_WIDTH, PAGE_SIZE), page),
                     pl.BlockSpec((None, ATT_WIDTH, tm), lambda b, i: (b, 0, i)),
                     pl.BlockSpec((pg_tile, N_HEADS, PAGE_SIZE), page)]
    out_shape += shared_shape
    out_specs += shared_specs
    return pl.pallas_call(
        functools.partial(_inproj_kernel, sample=sample, nseq=nseq),
        out_shape=out_shape,
        grid=(nb, nt),
        in_specs=in_specs,
        out_specs=out_specs,
        scratch_shapes=[pltpu.VMEM((1, LANES), F32)],
        compiler_params=pltpu.CompilerParams(dimension_semantics=("arbitrary", "arbitrary"),
                                             vmem_limit_bytes=VMEM_LIMIT),
        name="inproj_sample" if sample else "inproj_prompt",
    )(x, mod, mod, g1, w_all, b_all, tri, place, ones_row)


def _attn_kernel(qat_ref, ka_ref, vt_ref, o_ref, s_even, s_odd, mx_even, mx_odd, *, tq, tk):
    qi = pl.program_id(2)
    s_bufs = (s_even, s_odd)
    mx_bufs = (mx_even, mx_odd)
    q_heads = [qat_ref[hh * LANES:(hh + 1) * LANES, :] for hh in range(2)]
    key = lax.broadcasted_iota(jnp.int32, (tk, ATTN_QCHUNK), 0)
    qry = lax.broadcasted_iota(jnp.int32, (tk, ATTN_QCHUNK), 1)

    units = [(hh, slice(c, c + ATTN_QCHUNK), hh * (tq // ATTN_QCHUNK) + c // ATTN_QCHUNK)
             for hh in range(2) for c in range(0, tq, ATTN_QCHUNK)]

    def scores(ki, parity, unit, diagonal):
        hh, cols, u = unit
        ka = ka_ref[pl.ds(pl.multiple_of(ki * tk, tk), tk), hh * LANES:(hh + 1) * LANES]
        st = jnp.dot(ka, q_heads[hh][:, cols], preferred_element_type=F32)
        if diagonal:
            st = jnp.where(key + parity * tk <= qry + cols.start, st, NEG_INF)
        s_bufs[parity][u] = st
        mx_bufs[parity][u] = jnp.max(st, axis=0, keepdims=True)

    ones_rows = jnp.ones((SUM_ROWS, tk), BF16)

    def softmax_pv(ki, parity, unit, carry):
        hh, _, u = unit
        m_prev, acc = carry
        m_new = jnp.maximum(m_prev, mx_bufs[parity][u])
        alpha = jnp.exp2(m_prev - m_new)
        pt = jnp.exp2(s_bufs[parity][u] - m_new).astype(BF16)
        vt = vt_ref[hh * HEAD_DIM:(hh + 1) * HEAD_DIM, pl.ds(pl.multiple_of(ki * tk, tk), tk)]
        vt = jnp.concatenate([vt, ones_rows], axis=0)
        return m_new, acc * alpha + jnp.dot(vt, pt, preferred_element_type=F32)

    def stage(k_next, p_next, k_cur, p_cur, carry, diagonal=False):
        out = []
        for unit, c in zip(units, carry):
            scores(k_next, p_next, unit, diagonal)
            out.append(softmax_pv(k_cur, p_cur, unit, c))
        return tuple(out)

    unit0 = (jnp.full((1, ATTN_QCHUNK), NEG_INF, F32), jnp.zeros((HEAD_DIM + SUM_ROWS, ATTN_QCHUNK), F32))
    for unit in units:
        scores(2 * qi, 0, unit, True)
    carry = stage(2 * qi + 1, 1, 2 * qi, 0, (unit0,) * len(units), diagonal=True)

    def pair(j, carry):
        carry = stage(2 * j, 0, jnp.where(j == 0, 2 * qi + 1, 2 * j - 1), 1, carry)
        return stage(2 * j + 1, 1, 2 * j, 0, carry)

    carry = lax.fori_loop(0, qi, pair, carry)
    last = jnp.where(qi == 0, 1, 2 * qi - 1)
    heads = [[], []]
    for unit, c in zip(units, carry):
        _, acc = softmax_pv(last, 1, unit, c)
        heads[unit[0]].append(acc[:HEAD_DIM] / acc[HEAD_DIM:HEAD_DIM + 1])
    o = jnp.concatenate([jnp.concatenate(h, axis=1) for h in heads], axis=0)
    o_ref[...] = o.T.astype(o_ref.dtype)


def _prompt_attention(qat, ka, vt, *, nb, seq, tq):
    tk = tq // 2
    ka3 = ka.reshape(nb, seq, N_HEADS * LANES)
    out = pl.pallas_call(
        functools.partial(_attn_kernel, tq=tq, tk=tk),
        out_shape=jax.ShapeDtypeStruct((nb, seq, ATT_WIDTH), BF16),
        grid=(nb, N_PAIRS, seq // tq),
        in_specs=[pl.BlockSpec((None, 2 * LANES, tq), lambda b, p, i: (b, p, i)),
                  pl.BlockSpec((None, seq, 2 * LANES), lambda b, p, i: (b, 0, p)),
                  pl.BlockSpec((None, LANES, seq), lambda b, p, i: (b, p, 0))],
        out_specs=pl.BlockSpec((None, tq, LANES), lambda b, p, i: (b, i, p)),
        scratch_shapes=([pltpu.VMEM((2 * tq // ATTN_QCHUNK, tk, ATTN_QCHUNK), F32)] * 2
                        + [pltpu.VMEM((2 * tq // ATTN_QCHUNK, 1, ATTN_QCHUNK), F32)] * 2),
        compiler_params=pltpu.CompilerParams(dimension_semantics=("arbitrary", "arbitrary", "arbitrary"),
                                             vmem_limit_bytes=VMEM_LIMIT),
        name="prompt_attention",
    )(qat, ka3, vt)
    return out.reshape(nb * seq, ATT_WIDTH)


class _SampleAttention:
    SCORE_ROWS = N_HEADS * SAMPLE_T
    NT_DIMS = (((1,), (1,)), ((), ()))

    def __init__(self, q_scr, m_scr, l_scr, acc_scr, carry_scr):
        self.q_scr, self.m_scr, self.l_scr, self.acc_scr, self.carry_scr = q_scr, m_scr, l_scr, acc_scr, carry_scr

    @staticmethod
    def scratch_shapes():
        rows = _SampleAttention.SCORE_ROWS
        return [pltpu.VMEM((rows, ATT_WIDTH), BF16), pltpu.VMEM((rows, 1), F32), pltpu.VMEM((rows, 1), F32),
                pltpu.VMEM((rows, ATT_WIDTH), F32), pltpu.VMEM((N_HEADS, LANES), F32)]

    @staticmethod
    def per_head_rows(a):
        return jnp.concatenate([jnp.broadcast_to(a[h:h + 1, :], (SAMPLE_T, a.shape[1])) for h in range(N_HEADS)],
                               axis=0)

    def update(self, s, pv):
        m_prev = self.m_scr[...]
        m_new = jnp.maximum(m_prev, jnp.max(s, axis=-1, keepdims=True))
        alpha = jnp.exp(m_prev - m_new)
        p = jnp.exp(s - m_new)
        self.l_scr[...] = alpha * self.l_scr[...] + jnp.sum(p, axis=-1, keepdims=True)
        self.acc_scr[...] = alpha * self.acc_scr[...] + pv(p.astype(BF16))
        self.m_scr[...] = m_new

    def start(self, q_ref, kn_ref, vn_ref, cn_ref):
        rows, q_scr, nt_dims, per_head_rows = self.SCORE_ROWS, self.q_scr, self.NT_DIMS, self.per_head_rows
        m_scr, l_scr, acc_scr, carry_scr, update = self.m_scr, self.l_scr, self.acc_scr, self.carry_scr, self.update
        q8 = q_ref[...].astype(F32)
        qt = jnp.concatenate([q8] * N_HEADS, axis=0)
        r_i = lax.broadcasted_iota(jnp.int32, (rows, ATT_WIDTH), 0)
        c_i = lax.broadcasted_iota(jnp.int32, (rows, ATT_WIDTH), 1)
        q_scr[...] = jnp.where(r_i // SAMPLE_T == c_i // HEAD_DIM, qt, 0.0).astype(BF16)
        m_scr[...] = jnp.full_like(m_scr, NEG_INF)
        l_scr[...] = jnp.zeros_like(l_scr)
        acc_scr[...] = jnp.zeros_like(acc_scr)
        carry_scr[...] = jnp.zeros_like(carry_scr)
        pad = jnp.zeros((PAGE_SIZE - SAMPLE_T, ATT_WIDTH), F32)
        k_new = jnp.concatenate([kn_ref[...], pad], axis=0).astype(BF16)
        v_new = jnp.concatenate([vn_ref[...], pad], axis=0).astype(BF16)
        c_new = jnp.concatenate([cn_ref[...], jnp.zeros((PAGE_SIZE - SAMPLE_T, LANES), F32)], axis=0)
        s = lax.dot_general(q_scr[...], k_new, nt_dims, preferred_element_type=F32)
        s = s - per_head_rows(c_new.T[:N_HEADS, :])
        r_s = lax.broadcasted_iota(jnp.int32, (rows, PAGE_SIZE), 0)
        c_s = lax.broadcasted_iota(jnp.int32, (rows, PAGE_SIZE), 1)
        update(jnp.where(c_s <= r_s % SAMPLE_T, s, NEG_INF),
               lambda p: jnp.dot(p, v_new, preferred_element_type=F32))

    def page_scores(self, k_pages, lf_pages, tri_ref):
        pieces = []
        for lf in lf_pages:
            hi = lf.astype(BF16).astype(F32)
            mid = (lf - hi).astype(BF16).astype(F32)
            lo = (lf - hi - mid).astype(BF16).astype(F32)
            pieces += [hi, mid, lo]
        sums = jnp.dot(jnp.concatenate(pieces, axis=0).astype(BF16), tri_ref[...], preferred_element_type=F32)
        carry = self.carry_scr[...]
        bias = []
        for i in range(len(lf_pages)):
            part = [sums[(3 * i + j) * N_HEADS:(3 * i + j + 1) * N_HEADS, :] for j in range(AUG)]
            local = part[0] + part[1] + part[2]
            bias.append(local[:, :PAGE_SIZE] + carry)
            carry = carry + local[:, PAGE_SIZE:]
        self.carry_scr[...] = carry
        kt = jnp.concatenate([k.astype(BF16) for k in k_pages], axis=1)
        s = jnp.dot(self.q_scr[...], kt, preferred_element_type=F32)
        return s + self.per_head_rows(jnp.concatenate(bias, axis=1))

    def absorb_pages(self, s, v_pages):
        vt = jnp.concatenate([v.astype(BF16) for v in v_pages], axis=1)
        self.update(s, lambda p: lax.dot_general(p, vt, self.NT_DIMS, preferred_element_type=F32))

    def finish(self, o_ref):
        rows = self.SCORE_ROWS
        o = self.acc_scr[...] / self.l_scr[...]
        r_i = lax.broadcasted_iota(jnp.int32, (rows, ATT_WIDTH), 0)
        c_i = lax.broadcasted_iota(jnp.int32, (rows, ATT_WIDTH), 1)
        o = jnp.where(r_i // SAMPLE_T == c_i // HEAD_DIM, o, 0.0).astype(BF16)
        t_i = lax.broadcasted_iota(jnp.int32, (SAMPLE_T, rows), 0)
        r_j = lax.broadcasted_iota(jnp.int32, (SAMPLE_T, rows), 1)
        sel = jnp.where(r_j % SAMPLE_T == t_i, 1.0, 0.0).astype(BF16)
        o_ref[...] = jnp.dot(sel, o, preferred_element_type=F32).astype(o_ref.dtype)


def _merge_tail(rows, conv, g1, x_ref, o_ref, ga_ref, gb_ref, dwb_ref, lng_ref, lnb_ref, wpa_ref, wpb_ref, bpb_ref,
                wo_ref, out_ref):
    yb = conv + dwb_ref[...]
    mu = jnp.mean(yb, axis=-1, keepdims=True)
    var = jnp.mean(jnp.square(yb - mu), axis=-1, keepdims=True)
    yb = (yb - mu) * lax.rsqrt(var + EPS) * lng_ref[...] + lnb_ref[...]
    yb = jnp.dot(_silu(yb).astype(BF16), wpb_ref[...], preferred_element_type=F32) + bpb_ref[...]
    ya = jnp.dot(o_ref[rows, :], wpa_ref[...], preferred_element_type=F32)
    m = ga_ref[rows, :].astype(F32) * ya + gb_ref[rows, :].astype(F32) * yb
    out_ref[rows, :] = x_ref[rows, :] + g1 * jnp.dot(m.astype(BF16), wo_ref[...], preferred_element_type=F32)


def _merge_prompt_kernel(x_ref, o_ref, ucur_ref, uprev_ref, ga_ref, gb_ref, g1_ref, dww_ref, dwb_ref, lng_ref,
                         lnb_ref, wpa_ref, wpb_ref, bpb_ref, wo_ref, out_ref, ubuf, *, nseq, row_block):
    tm = x_ref.shape[0]
    first = pl.program_id(1) == 0
    prev = uprev_ref[...]
    ubuf[:, 0:CONV_HALO, :] = jnp.where(first, jnp.zeros_like(prev), prev)
    ubuf[:, CONV_HALO:, :] = ucur_ref[...]
    g1 = _mod_rows(g1_ref, nseq, False)
    chunk = 64
    for b0 in range(0, tm, row_block):
        slabs = []
        for c in range(N_SLABS):
            pieces = []
            for r0 in range(b0, b0 + row_block, chunk):
                acc = jnp.zeros((chunk, LANES), F32)
                for j in range(CONV_K):
                    off = r0 + CONV_HALO - (CONV_K - 1) + j
                    acc = acc + dww_ref[c, j:j + 1, :] * ubuf[c, off:off + chunk, :]
                pieces.append(acc)
            slabs.append(jnp.concatenate(pieces, axis=0))
        _merge_tail(slice(b0, b0 + row_block), jnp.concatenate(slabs, axis=1), g1, x_ref, o_ref, ga_ref, gb_ref,
                    dwb_ref, lng_ref, lnb_ref, wpa_ref, wpb_ref, bpb_ref, wo_ref, out_ref)


def _merge_sample_kernel(x_ref, o_ref, uwin_ref, ga_ref, gb_ref, g1_ref, dww_ref, dwb_ref, lng_ref,
                         lnb_ref, wpa_ref, wpb_ref, bpb_ref, wo_ref, out_ref, *, nseq):
    slabs = []
    for c in range(N_SLABS):
        acc = jnp.zeros((nseq, SAMPLE_T, LANES), F32)
        for j in range(CONV_K):
            acc = acc + dww_ref[c, j:j + 1, :] * uwin_ref[c, :, j:j + SAMPLE_T, :]
        slabs.append(acc.reshape(nseq * SAMPLE_T, LANES))
    _merge_tail(slice(None), jnp.concatenate(slabs, axis=1), _mod_rows(g1_ref, nseq, True), x_ref, o_ref, ga_ref,
                gb_ref, dwb_ref, lng_ref, lnb_ref, wpa_ref, wpb_ref, bpb_ref, wo_ref, out_ref)


def _merge(x, o, u_args, ga, gb, mod, weights, *, nb, tm, sample, nseq):
    n = x.shape[0]
    nt = n // (nb * tm)
    row = lambda b, i: (b * nt + i, 0)
    const2 = lambda b, i: (0, 0)
    const3 = lambda b, i: (0, 0, 0)
    once = dict(pipeline_mode=pl.Buffered(1))
    mod_spec = _mod_spec(mod, 2)
    if sample:
        (uwin,) = u_args
        u_specs = [pl.BlockSpec(uwin.shape, lambda b, i: (0, 0, 0, 0))]
        kernel, scratch = functools.partial(_merge_sample_kernel, nseq=nseq), []
    else:
        (u,) = u_args
        u_args = (u, u)
        blocks_per_tile = tm // CONV_HALO
        u_specs = [pl.BlockSpec((N_SLABS, tm, LANES), lambda b, i: (0, b * nt + i, 0)),
                   pl.BlockSpec((N_SLABS, CONV_HALO, LANES),
                                lambda b, i: (0, jnp.maximum((b * nt + i) * blocks_per_tile - 1, 0), 0))]
        kernel = functools.partial(_merge_prompt_kernel, nseq=nseq, row_block=tm // 2)
        scratch = [pltpu.VMEM((N_SLABS, CONV_HALO + tm, LANES), F32)]
    dww, dwb, lng, lnb, wpa, wpb, bpb, wo = weights
    in_specs = ([pl.BlockSpec((tm, D_MODEL), row), pl.BlockSpec((tm, ATT_WIDTH), row)] + u_specs
                + [pl.BlockSpec((tm, D_MODEL), row), pl.BlockSpec((tm, D_MODEL), row), mod_spec,
                   pl.BlockSpec(dww.shape, const3),
                   pl.BlockSpec((1, CONV_WIDTH), const2), pl.BlockSpec((1, CONV_WIDTH), const2),
                   pl.BlockSpec((1, CONV_WIDTH), const2),
                   pl.BlockSpec((ATT_WIDTH, D_MODEL), const2, **once),
                   pl.BlockSpec((CONV_WIDTH, D_MODEL), const2, **once),
                   pl.BlockSpec((1, D_MODEL), const2),
                   pl.BlockSpec((D_MODEL, D_MODEL), const2, **once)])
    return pl.pallas_call(
        kernel,
        out_shape=jax.ShapeDtypeStruct((n, D_MODEL), F32),
        grid=(nb, nt),
        in_specs=in_specs,
        out_specs=pl.BlockSpec((tm, D_MODEL), row),
        scratch_shapes=scratch,
        compiler_params=pltpu.CompilerParams(dimension_semantics=("arbitrary", "arbitrary"),
                                             vmem_limit_bytes=VMEM_LIMIT),
        name="merge_sample" if sample else "merge_prompt",
    )(x, o, *u_args, ga, gb, mod, dww, dwb, lng, lnb, wpa, wpb, bpb, wo)


def _ffn_steps(x_ref, sh_ref, sc_ref, g2_ref, rg_ref, fg_ref, win_ref, wout_ref, out_ref, *, bounds, sample, nseq):
    x = x_ref[...]
    ms = jnp.mean(x * x, axis=-1, keepdims=True)
    h = x * lax.rsqrt(ms + EPS) * rg_ref[...]
    hb = (h * (1.0 + _mod_rows(sc_ref, nseq, sample)) + _mod_rows(sh_ref, nseq, sample)).astype(BF16)
    acc = jnp.zeros(x.shape, F32)
    for lo, hi in zip(bounds[:-1], bounds[1:]):
        gate = jnp.dot(hb, win_ref[:, lo:hi], preferred_element_type=F32)
        up = jnp.dot(hb, win_ref[:, FFN_HIDDEN + lo:FFN_HIDDEN + hi], preferred_element_type=F32)
        yield
        act = (_silu(gate) * up).astype(BF16)
        acc = acc + jnp.dot(act, wout_ref[lo:hi, :], preferred_element_type=F32)
        if hi == bounds[-1]:
            x2 = x + _mod_rows(g2_ref, nseq, sample) * acc
            ms2 = jnp.mean(x2 * x2, axis=-1, keepdims=True)
            out_ref[...] = x2 * lax.rsqrt(ms2 + EPS) * fg_ref[...]
        yield


def _hidden_bounds(n_chunks):
    tiles = FFN_HIDDEN // LANES
    return [LANES * ((tiles * k) // n_chunks) for k in range(n_chunks + 1)]


def _ffn_kernel(*refs, sample, nseq):
    for _ in _ffn_steps(*refs, bounds=_hidden_bounds(2), sample=sample, nseq=nseq):
        pass


def _ffn_attn_kernel(pt_ref, x_ref, sh_ref, sc_ref, g2_ref, rg_ref, fg_ref, win_ref, wout_ref,
                     q_ref, kn_ref, vn_ref, cn_ref, tri_ref, ck_hbm, cv_hbm, clf_hbm,
                     out_ref, o_ref, kbuf, vbuf, lbuf, sem, *state, nseq, n_pages):
    npg = PAGES_PER_STEP
    n_chunks = n_pages // npg
    seq = pl.program_id(0) * pl.num_programs(1) + pl.program_id(1)
    attn = _SampleAttention(*state)

    def copies(sequence, c):
        slot, out = c % 2, []
        for j in range(npg):
            page = pt_ref[sequence * n_pages + (n_pages - 1) - (c * npg + j)]
            out += [pltpu.make_async_copy(ck_hbm.at[page], kbuf.at[slot, j], sem.at[0, slot]),
                    pltpu.make_async_copy(cv_hbm.at[page], vbuf.at[slot, j], sem.at[1, slot]),
                    pltpu.make_async_copy(clf_hbm.at[page], lbuf.at[slot, j], sem.at[2, slot])]
        return out

    def start(sequence, c):
        for cp in copies(sequence, c):
            cp.start()

    @pl.when(seq == 0)
    def _():
        start(seq, 0)

    ffn = _ffn_steps(x_ref, sh_ref, sc_ref, g2_ref, rg_ref, fg_ref, win_ref, wout_ref, out_ref,
                     bounds=_hidden_bounds(n_chunks), sample=False, nseq=nseq)
    for c in range(n_chunks):
        for cp in copies(seq, c):
            cp.wait()
        if c + 1 < n_chunks:
            start(seq, c + 1)
        else:
            @pl.when(seq + 1 < nseq)
            def _():
                start(seq + 1, 0)
        if c == 0:
            attn.start(q_ref, kn_ref, vn_ref, cn_ref)
        slot = c % 2
        next(ffn)
        s = attn.page_scores([kbuf[slot, j] for j in range(npg)], [lbuf[slot, j] for j in range(npg)], tri_ref)
        next(ffn)
        attn.absorb_pages(s, [vbuf[slot, j] for j in range(npg)])
    attn.finish(o_ref)


def _ffn(x, mod, rms_g, final_g, w_in, w_out, *, nb, tm, sample, nseq):
    n = x.shape[0]
    nt = n // (nb * tm)
    row = lambda b, i: (b * nt + i, 0)
    const2 = lambda b, i: (0, 0)
    once = dict(pipeline_mode=pl.Buffered(1))
    return pl.pallas_call(
        functools.partial(_ffn_kernel, sample=sample, nseq=nseq),
        out_shape=jax.ShapeDtypeStruct((n, D_MODEL), F32),
        grid=(nb, nt),
        in_specs=[pl.BlockSpec((tm, D_MODEL), row), _mod_spec(mod, 3), _mod_spec(mod, 4), _mod_spec(mod, 5),
                  pl.BlockSpec((1, D_MODEL), const2), pl.BlockSpec((1, D_MODEL), const2),
                  pl.BlockSpec((D_MODEL, 2 * FFN_HIDDEN), const2, **once),
                  pl.BlockSpec((FFN_HIDDEN, D_MODEL), const2, **once)],
        out_specs=pl.BlockSpec((tm, D_MODEL), row),
        compiler_params=pltpu.CompilerParams(dimension_semantics=("arbitrary", "arbitrary"),
                                             vmem_limit_bytes=VMEM_LIMIT),
        name="ffn_sample" if sample else "ffn_prompt",
    )(x, mod, mod, mod, rms_g, final_g, w_in, w_out)


def _ffn_with_sample_attention(x, mod, rms_g, final_g, w_in, w_out, page_table, q, k_new, v_new, c_new, tri_page,
                               cache_k, cache_v, cache_logf, *, nb, tm):
    n = x.shape[0]
    nt = n // (nb * tm)
    nseq, n_pages = page_table.shape
    assert nseq == nb * nt and n_pages % PAGES_PER_STEP == 0
    row = lambda b, i, pt: (b * nt + i, 0)
    const2 = lambda b, i, pt: (0, 0)
    once = dict(pipeline_mode=pl.Buffered(1))
    mod_spec = lambda kind: pl.BlockSpec((None, mod.shape[1], D_MODEL), lambda b, i, pt: (kind, 0, 0))
    seq_spec = lambda width: pl.BlockSpec((None, SAMPLE_T, width), lambda b, i, pt: (b * nt + i, 0, 0))
    hbm = pl.BlockSpec(memory_space=pl.ANY)
    slots = 2
    grid_spec = pltpu.PrefetchScalarGridSpec(
        num_scalar_prefetch=1,
        grid=(nb, nt),
        in_specs=[pl.BlockSpec((tm, D_MODEL), row), mod_spec(3), mod_spec(4), mod_spec(5),
                  pl.BlockSpec((1, D_MODEL), const2), pl.BlockSpec((1, D_MODEL), const2),
                  pl.BlockSpec((D_MODEL, 2 * FFN_HIDDEN), const2, **once),
                  pl.BlockSpec((FFN_HIDDEN, D_MODEL), const2, **once),
                  seq_spec(ATT_WIDTH), seq_spec(ATT_WIDTH), seq_spec(ATT_WIDTH), seq_spec(LANES),
                  pl.BlockSpec((PAGE_SIZE, 2 * PAGE_SIZE), const2), hbm, hbm, hbm],
        out_specs=[pl.BlockSpec((tm, D_MODEL), row), seq_spec(ATT_WIDTH)],
        scratch_shapes=[pltpu.VMEM((slots, PAGES_PER_STEP, ATT_WIDTH, PAGE_SIZE), F32),
                        pltpu.VMEM((slots, PAGES_PER_STEP, ATT_WIDTH, PAGE_SIZE), F32),
                        pltpu.VMEM((slots, PAGES_PER_STEP, N_HEADS, PAGE_SIZE), F32),
                        pltpu.SemaphoreType.DMA((3, slots))] + _SampleAttention.scratch_shapes())
    return pl.pallas_call(
        functools.partial(_ffn_attn_kernel, nseq=nseq, n_pages=n_pages),
        out_shape=[jax.ShapeDtypeStruct((n, D_MODEL), F32),
                   jax.ShapeDtypeStruct((nseq, SAMPLE_T, ATT_WIDTH), BF16)],
        grid_spec=grid_spec,
        compiler_params=pltpu.CompilerParams(dimension_semantics=("arbitrary", "arbitrary"),
                                             vmem_limit_bytes=VMEM_LIMIT),
        name="ffn_prompt_sample_attention",
    )(page_table.reshape(-1), x, mod, mod, mod, rms_g, final_g, w_in, w_out, q, k_new, v_new, c_new, tri_page,
      cache_k, cache_v, cache_logf)


def _bias_placement():
    half = N_HEADS * LANES
    place = np.zeros((LANES, 2 * half), np.float32)
    ones = np.zeros((1, 2 * half), np.float32)
    for h in range(N_HEADS):
        base = h * LANES + (HEAD_DIM if h % 2 == 0 else 0)
        for piece in range(AUG):
            place[piece * N_HEADS + h, base + piece] = 1.0
            ones[0, base + AUG + piece] = 1.0
            ones[0, half + base + piece] = 1.0
            place[piece * N_HEADS + h, half + base + AUG + piece] = -1.0
    return jnp.asarray(place, BF16), jnp.asarray(ones, F32)


def _lower_tri(n, block):
    t = np.arange(n)[:, None]
    s = np.arange(n)[None, :]
    return jnp.asarray(((s <= t) & (t // block == s // block)).astype(np.float32), BF16)


def _later_keys(n):
    j = np.arange(n)[:, None]
    s = np.arange(n)[None, :]
    return jnp.asarray(np.concatenate([(j > s).astype(np.float32), np.ones((n, n), np.float32)], axis=1), BF16)


def kernel(x_prompt, x_sample, c_prompt, c_sample, cache_k, cache_v, cache_logf, state_conv, page_table, rms1_g, rms2_g, w_ada, b_ada, w_in, b_in, dw_w, dw_b, ln_g, ln_b, w_pa, w_pb, b_pb, w_o, w_ffn_in, w_ffn_out, final_g):
    nb, seq, _ = x_prompt.shape
    nseq, dec_t, _ = x_sample.shape
    depth = w_in.shape[0]
    assert depth == 1 and dec_t <= SAMPLE_T
    n_prompt = nb * seq
    tm = 512
    n_sample = nseq * SAMPLE_T

    wt, b = w_in[0].T, b_in[0]
    g_off = 3 * ATT_WIDTH + N_HEADS
    w_all = jnp.concatenate([wt[:g_off], jnp.zeros((LANES - N_HEADS, D_MODEL), F32), wt[g_off:]], axis=0).astype(BF16)
    b_all = jnp.concatenate([b[:g_off], jnp.zeros((LANES - N_HEADS,), F32), b[g_off:]])[None, :]
    dww = jnp.pad(dw_w[0], ((0, CONV_HALO - CONV_K), (0, 0))).reshape(CONV_HALO, N_SLABS, LANES).transpose(1, 0, 2)
    merge_w = (dww, dw_b[0][None, :], ln_g[0][None, :], ln_b[0][None, :], w_pa[0].astype(BF16),
               w_pb[0].astype(BF16), b_pb[0][None, :], w_o[0].astype(BF16))
    wf_in, wf_out = w_ffn_in[0].astype(BF16), w_ffn_out[0].astype(BF16)
    g1w, g2w, gfw = rms1_g[0][None, :], rms2_g[0][None, :], final_g[None, :]
    place, ones_row = _bias_placement()

    n_cond = nb + nseq
    c_all = jnp.pad(jnp.concatenate([c_sample, c_prompt], axis=0), ((0, -n_cond % 8), (0, 0)))
    mod = _modulation(c_all, w_ada[0], b_ada[0][None, :])

    xs = jnp.pad(x_sample, ((0, 0), (0, SAMPLE_T - dec_t), (0, 0))).reshape(n_sample, D_MODEL)
    (q_s, k_s, v_s, lf_s, c_s, u_s, ga_s, gb_s) = _inproj(
        xs, mod, g1w, w_all, b_all, _lower_tri(n_sample, SAMPLE_T), place, ones_row,
        nb=1, tm=n_sample, sample=True, nseq=nseq)

    xp = x_prompt.reshape(n_prompt, D_MODEL)
    (qa, ka, kt_p, vt_p, vb_p, lft_p, u_p, ga_p, gb_p) = _inproj(
        xp, mod, g1w, w_all, b_all, _lower_tri(tm, tm), place, ones_row, nb=nb, tm=tm, sample=False, nseq=nseq)
    o_p = _prompt_attention(qa, ka, vb_p, nb=nb, seq=seq, tq=2 * tm)
    x1_p = _merge(xp, o_p, (u_p,), ga_p, gb_p, mod, merge_w, nb=nb, tm=tm, sample=False, nseq=nseq)
    n_phys = cache_k.shape[1]
    page_t = lambda c: c[0].transpose(0, 2, 3, 1).reshape(n_phys, ATT_WIDTH, PAGE_SIZE)
    y_p, o_s = _ffn_with_sample_attention(
        x1_p, mod, g2w, gfw, wf_in, wf_out,
        page_table, q_s.reshape(nseq, SAMPLE_T, ATT_WIDTH), k_s.reshape(nseq, SAMPLE_T, ATT_WIDTH),
        v_s.reshape(nseq, SAMPLE_T, ATT_WIDTH), c_s.reshape(nseq, SAMPLE_T, LANES), _later_keys(PAGE_SIZE),
        page_t(cache_k), page_t(cache_v), cache_logf[0].transpose(0, 2, 1), nb=nb, tm=tm)

    state_slabs = state_conv[0].reshape(nseq, CONV_K - 1, N_SLABS, LANES).transpose(2, 0, 1, 3)
    u_slabs = u_s.reshape(N_SLABS, nseq, SAMPLE_T, LANES)
    uwin = jnp.concatenate(
        [state_slabs, u_slabs, jnp.zeros((N_SLABS, nseq, SAMPLE_WIN - (CONV_K - 1) - SAMPLE_T, LANES), F32)], axis=2)
    x1_s = _merge(xs, o_s.reshape(n_sample, ATT_WIDTH), (uwin,), ga_s, gb_s, mod, merge_w,
                  nb=1, tm=n_sample, sample=True, nseq=nseq)
    y_s = _ffn(x1_s, mod, g2w, gfw, wf_in, wf_out, nb=1, tm=n_sample, sample=True, nseq=nseq)

    n_pg = seq // PAGE_SIZE
    tail = CONV_K - 1
    u_tail = u_p.reshape(N_SLABS, nb, seq, LANES)[:, :, seq - tail:]
    u_tail = u_tail.transpose(1, 2, 0, 3).reshape(nb, tail, CONV_WIDTH)
    us_rows = u_s.reshape(N_SLABS, nseq, SAMPLE_T, LANES)[:, :, :dec_t]
    us_rows = us_rows.transpose(1, 2, 0, 3).reshape(nseq, dec_t, CONV_WIDTH)
    unpad = lambda a, width: a.reshape(nseq, SAMPLE_T, *width)[:, :dec_t]
    return (y_p.reshape(nb, seq, D_MODEL),
            unpad(y_s, (D_MODEL,)),
            kt_p.reshape(1, nb, n_pg, N_HEADS, HEAD_DIM, PAGE_SIZE).transpose(0, 1, 2, 5, 3, 4),
            vt_p.reshape(1, nb, n_pg, N_HEADS, HEAD_DIM, PAGE_SIZE).transpose(0, 1, 2, 5, 3, 4),
            lft_p.reshape(1, nb, n_pg, N_HEADS, PAGE_SIZE).transpose(0, 1, 2, 4, 3),
            u_tail[None],
            unpad(k_s, (N_HEADS, HEAD_DIM))[None],
            unpad(v_s, (N_HEADS, HEAD_DIM))[None],
            unpad(lf_s, (N_HEADS,))[None],
            jnp.concatenate([state_conv[0][:, dec_t:], us_rows], axis=1)[None])
```

```python
import functools

import numpy as np
import jax
import jax.numpy as jnp
from jax import lax
from jax.experimental import pallas as pl
from jax.experimental.pallas import tpu as pltpu

F32 = jnp.float32
BF16 = jnp.bfloat16

D_MODEL = 1024
N_HEADS = 8
HEAD_DIM = 64
ATT_WIDTH = N_HEADS * HEAD_DIM
CONV_WIDTH = 512
CONV_K = 31
FFN_HIDDEN = 2816
PAGE_SIZE = 128
EPS = 1e-6
NEG_INF = -1e30
SCALE = HEAD_DIM ** -0.5
LOG2E = 1.4426950408889634

LANES = 128
MXU_TILE = 256
N_PAIRS = N_HEADS // 2
N_SLABS = CONV_WIDTH // LANES
AUG = 3
SUM_ROWS = 16
ATTN_QCHUNK = 256
SAMPLE_T = 8
CONV_HALO = 32
SAMPLE_WIN = 40
PAGES_PER_STEP = 16
VMEM_LIMIT = 56 * 1024 * 1024

C_Q, C_K, C_V, C_F, C_GLA, C_GLB, C_GA, C_GB, C_END = 0, 512, 1024, 1536, 1664, 2176, 2688, 3712, 4736


def _sigmoid(x):
    return 1.0 / (1.0 + jnp.exp(-x))


def _silu(x):
    return x * _sigmoid(x)


def _split3_packed(a, lane):
    a = jnp.where(lane < N_HEADS, a, 0.0)
    hi = a.astype(BF16).astype(F32)
    r1 = a - hi
    mid = r1.astype(BF16).astype(F32)
    lo = (r1 - mid).astype(BF16).astype(F32)
    packed = hi + pltpu.roll(mid, N_HEADS, axis=1) + pltpu.roll(lo, 2 * N_HEADS, axis=1)
    return packed.astype(BF16)


def _unpack3(p, lane):
    s = p + pltpu.roll(p, LANES - N_HEADS, axis=1) + pltpu.roll(p, LANES - 2 * N_HEADS, axis=1)
    return jnp.where(lane < N_HEADS, s, 0.0)


def _mod_kernel(c_ref, w_ref, b_ref, o_ref):
    s = _silu(c_ref[...]).astype(BF16)
    o_ref[...] = jnp.dot(s, w_ref[...].astype(BF16), preferred_element_type=F32) + b_ref[...]


def _modulation(c_all, w_ada, b_ada):
    rows = c_all.shape[0]
    n_mod = w_ada.shape[1] // D_MODEL
    return pl.pallas_call(
        _mod_kernel,
        out_shape=jax.ShapeDtypeStruct((n_mod, rows, D_MODEL), F32),
        grid=(n_mod,),
        in_specs=[pl.BlockSpec((rows, D_MODEL), lambda j: (0, 0)),
                  pl.BlockSpec((D_MODEL, D_MODEL), lambda j: (0, j)),
                  pl.BlockSpec((1, D_MODEL), lambda j: (0, j))],
        out_specs=pl.BlockSpec((None, rows, D_MODEL), lambda j: (j, 0, 0)),
        compiler_params=pltpu.CompilerParams(dimension_semantics=("arbitrary",), vmem_limit_bytes=VMEM_LIMIT),
        name="modulation",
    )(c_all, w_ada, b_ada)


def _mod_rows(ref, nseq, sample):
    if sample:
        m = ref[0:nseq, :]
        return jnp.broadcast_to(m[:, None, :], (nseq, SAMPLE_T, D_MODEL)).reshape(nseq * SAMPLE_T, D_MODEL)
    return ref[pl.ds(nseq + pl.program_id(0), 1), :]


def _mod_spec(mod, kind):
    return pl.BlockSpec((None, mod.shape[1], D_MODEL), lambda b, i: (kind, 0, 0))


def _inproj_kernel(x_ref, sh_ref, sc_ref, g_ref, w_ref, b_ref, tri_ref, place_ref, ones_ref, *refs, sample, nseq):
    if sample:
        qp_ref, k_ref, v_ref, lf_ref, c_ref, u_ref, ga_ref, gb_ref, carry_ref = refs
    else:
        qat_ref, ka_ref, kt_ref, vt_ref, vb_ref, lft_ref, u_ref, ga_ref, gb_ref, carry_ref = refs
    tm = x_ref.shape[0]

    @pl.when(pl.program_id(1) == 0)
    def _():
        carry_ref[...] = jnp.zeros_like(carry_ref)

    x = x_ref[...]
    ms = jnp.mean(x * x, axis=-1, keepdims=True)
    h = x * lax.rsqrt(ms + EPS) * g_ref[...]
    h = h * (1.0 + _mod_rows(sc_ref, nseq, sample)) + _mod_rows(sh_ref, nseq, sample)
    hb = h.astype(BF16)

    def seg(lo, hi):
        z = lax.dot_general(hb, w_ref[lo:hi, :], (((1,), (1,)), ((), ())), preferred_element_type=F32)
        return z + b_ref[:, lo:hi]

    lane = lax.broadcasted_iota(jnp.int32, (tm, LANES), 1)
    zf = seg(C_F, C_GLA)
    lf = jnp.minimum(zf, 0.0) - jnp.log1p(jnp.exp(-jnp.abs(zf)))
    lf = jnp.where(lane < N_HEADS, lf, 0.0)
    csum = _unpack3(jnp.dot(tri_ref[...], _split3_packed(lf, lane), preferred_element_type=F32), lane)
    csum = csum + carry_ref[...]
    carry_ref[...] = csum[tm - 1:tm, :]

    zq = seg(C_Q, C_K)
    zk = seg(C_K, C_V)
    zv = seg(C_V, C_F)
    if sample:
        qp_ref[...] = (zq * SCALE).astype(BF16)
        k_ref[...] = zk
        v_ref[...] = zv
        lf_ref[...] = lf[:, :N_HEADS]
        c_ref[...] = csum
    else:
        for pg in range(tm // PAGE_SIZE):
            rows = slice(pg * PAGE_SIZE, (pg + 1) * PAGE_SIZE)
            kt_ref[pg] = zk[rows, :].T
            vt_page = zv[rows, :].T
            vt_ref[pg] = vt_page
            vb_ref[:, rows] = vt_page.astype(BF16)
            lft_ref[pg] = lf[rows, :].T[:N_HEADS, :]
        zq = zq * (SCALE * LOG2E)
        aug = jnp.dot(_split3_packed(csum * LOG2E, lane), place_ref[...], preferred_element_type=F32) + ones_ref[...]
        for h in range(N_HEADS):
            own_half = (lane < HEAD_DIM) if h % 2 == 0 else (lane >= HEAD_DIM)
            pair, feat = slice((h // 2) * LANES, (h // 2 + 1) * LANES), slice(h * LANES, (h + 1) * LANES)
            qat_ref[feat, :] = jnp.where(own_half, zq[:, pair], aug[:, feat]).T.astype(BF16)
            k_feat = jnp.where(own_half, zk[:, pair], aug[:, N_HEADS * LANES + h * LANES:N_HEADS * LANES + (h + 1) * LANES])
            ka_ref[:, feat] = k_feat.astype(BF16)

    u = seg(C_GLA, C_GLB) * _sigmoid(seg(C_GLB, C_GA))
    for c in range(N_SLABS):
        u_ref[c] = u[:, c * LANES:(c + 1) * LANES]
    ga_ref[...] = _sigmoid(seg(C_GA, C_GB)).astype(BF16)
    gb_ref[...] = _sigmoid(seg(C_GB, C_END)).astype(BF16)


def _inproj(x, mod, g1, w_all, b_all, tri, place, ones_row, *, nb, tm, sample, nseq):
    n = x.shape[0]
    nt = n // (nb * tm)
    row = lambda b, i: (b * nt + i, 0)
    const2 = lambda b, i: (0, 0)
    once = dict(pipeline_mode=pl.Buffered(1))
    in_specs = [pl.BlockSpec((tm, D_MODEL), row), _mod_spec(mod, 0), _mod_spec(mod, 1),
                pl.BlockSpec((1, D_MODEL), const2),
                pl.BlockSpec((C_END, D_MODEL), const2, **once),
                pl.BlockSpec((1, C_END), const2),
                pl.BlockSpec((tm, tm), const2, **once),
                pl.BlockSpec(place.shape, const2, **once),
                pl.BlockSpec(ones_row.shape, const2)]
    shared_shape = [jax.ShapeDtypeStruct((N_SLABS, n, LANES), F32),
                    jax.ShapeDtypeStruct((n, D_MODEL), BF16),
                    jax.ShapeDtypeStruct((n, D_MODEL), BF16)]
    shared_specs = [pl.BlockSpec((N_SLABS, tm, LANES), lambda b, i: (0, b * nt + i, 0)),
                    pl.BlockSpec((tm, D_MODEL), row), pl.BlockSpec((tm, D_MODEL), row)]
    if sample:
        out_shape = [jax.ShapeDtypeStruct((n, ATT_WIDTH), BF16),
                     jax.ShapeDtypeStruct((n, ATT_WIDTH), F32),
                     jax.ShapeDtypeStruct((n, ATT_WIDTH), F32),
                     jax.ShapeDtypeStruct((n, N_HEADS), F32),
                     jax.ShapeDtypeStruct((n, LANES), F32)]
        out_specs = [pl.BlockSpec((tm, ATT_WIDTH), row), pl.BlockSpec((tm, ATT_WIDTH), row),
                     pl.BlockSpec((tm, ATT_WIDTH), row), pl.BlockSpec((tm, N_HEADS), row),
                     pl.BlockSpec((tm, LANES), row)]
    else:
        n_pg, pg_tile = n // PAGE_SIZE, tm // PAGE_SIZE
        page = lambda b, i: (b * nt + i, 0, 0)
        n_feat = N_HEADS * LANES
        out_shape = [jax.ShapeDtypeStruct((nb, n_feat, n // nb), BF16),
                     jax.ShapeDtypeStruct((n, n_feat), BF16),
                     jax.ShapeDtypeStruct((n_pg, ATT_WIDTH, PAGE_SIZE), F32),
                     jax.ShapeDtypeStruct((n_pg, ATT_WIDTH, PAGE_SIZE), F32),
                     jax.ShapeDtypeStruct((nb, ATT_WIDTH, n // nb), BF16),
                     jax.ShapeDtypeStruct((n_pg, N_HEADS, PAGE_SIZE), F32)]
        out_specs = [pl.BlockSpec((None, n_feat, tm), lambda b, i: (b, 0, i)), pl.BlockSpec((tm, n_feat), row),
                     pl.BlockSpec((pg_tile, ATT_WIDTH, PAGE_SIZE), page),
                     pl.BlockSpec((pg_tile, ATT_WIDTH, PAGE_SIZE), page),
                     pl.BlockSpec((None, ATT_WIDTH, tm), lambda b, i: (b, 0, i)),
                     pl.BlockSpec((pg_tile, N_HEADS, PAGE_SIZE), page)]
    out_shape += shared_shape
    out_specs += shared_specs
    return pl.pallas_call(
        functools.partial(_inproj_kernel, sample=sample, nseq=nseq),
        out_shape=out_shape,
        grid=(nb, nt),
        in_specs=in_specs,
        out_specs=out_specs,
        scratch_shapes=[pltpu.VMEM((1, LANES), F32)],
        compiler_params=pltpu.CompilerParams(dimension_semantics=("arbitrary", "arbitrary"),
                                             vmem_limit_bytes=VMEM_LIMIT),
        name="inproj_sample" if sample else "inproj_prompt",
    )(x, mod, mod, g1, w_all, b_all, tri, place, ones_row)


def _attn_kernel(qat_ref, ka_ref, vt_ref, o_ref, s_even, s_odd, mx_even, mx_odd, *, tq, tk):
    qi = pl.program_id(2)
    s_bufs = (s_even, s_odd)
    mx_bufs = (mx_even, mx_odd)
    q_heads = [qat_ref[hh * LANES:(hh + 1) * LANES, :] for hh in range(2)]
    key = lax.broadcasted_iota(jnp.int32, (tk, ATTN_QCHUNK), 0)
    qry = lax.broadcasted_iota(jnp.int32, (tk, ATTN_QCHUNK), 1)

    units = [(hh, slice(c, c + ATTN_QCHUNK), hh * (tq // ATTN_QCHUNK) + c // ATTN_QCHUNK)
             for hh in range(2) for c in range(0, tq, ATTN_QCHUNK)]

    def scores(ki, parity, unit, diagonal):
        hh, cols, u = unit
        ka = ka_ref[pl.ds(pl.multiple_of(ki * tk, tk), tk), hh * LANES:(hh + 1) * LANES]
        st = jnp.dot(ka, q_heads[hh][:, cols], preferred_element_type=F32)
        if diagonal:
            st = jnp.where(key + parity * tk <= qry + cols.start, st, NEG_INF)
        s_bufs[parity][u] = st
        mx_bufs[parity][u] = jnp.max(st, axis=0, keepdims=True)

    ones_rows = jnp.ones((SUM_ROWS, tk), BF16)

    def softmax_pv(ki, parity, unit, carry):
        hh, _, u = unit
        m_prev, acc = carry
        m_new = jnp.maximum(m_prev, mx_bufs[parity][u])
        alpha = jnp.exp2(m_prev - m_new)
        pt = jnp.exp2(s_bufs[parity][u] - m_new).astype(BF16)
        vt = vt_ref[hh * HEAD_DIM:(hh + 1) * HEAD_DIM, pl.ds(pl.multiple_of(ki * tk, tk), tk)]
        vt = jnp.concatenate([vt, ones_rows], axis=0)
        return m_new, acc * alpha + jnp.dot(vt, pt, preferred_element_type=F32)

    def stage(k_next, p_next, k_cur, p_cur, carry, diagonal=False):
        out = []
        for unit, c in zip(units, carry):
            scores(k_next, p_next, unit, diagonal)
            out.append(softmax_pv(k_cur, p_cur, unit, c))
        return tuple(out)

    unit0 = (jnp.full((1, ATTN_QCHUNK), NEG_INF, F32), jnp.zeros((HEAD_DIM + SUM_ROWS, ATTN_QCHUNK), F32))
    for unit in units:
        scores(2 * qi, 0, unit, True)
    carry = stage(2 * qi + 1, 1, 2 * qi, 0, (unit0,) * len(units), diagonal=True)

    def pair(j, carry):
        carry = stage(2 * j, 0, jnp.where(j == 0, 2 * qi + 1, 2 * j - 1), 1, carry)
        return stage(2 * j + 1, 1, 2 * j, 0, carry)

    carry = lax.fori_loop(0, qi, pair, carry)
    last = jnp.where(qi == 0, 1, 2 * qi - 1)
    heads = [[], []]
    for unit, c in zip(units, carry):
        _, acc = softmax_pv(last, 1, unit, c)
        heads[unit[0]].append(acc[:HEAD_DIM] / acc[HEAD_DIM:HEAD_DIM + 1])
    o = jnp.concatenate([jnp.concatenate(h, axis=1) for h in heads], axis=0)
    o_ref[...] = o.T.astype(o_ref.dtype)


def _prompt_attention(qat, ka, vt, *, nb, seq, tq):
    tk = tq // 2
    ka3 = ka.reshape(nb, seq, N_HEADS * LANES)
    out = pl.pallas_call(
        functools.partial(_attn_kernel, tq=tq, tk=tk),
        out_shape=jax.ShapeDtypeStruct((nb, seq, ATT_WIDTH), BF16),
        grid=(nb, N_PAIRS, seq // tq),
        in_specs=[pl.BlockSpec((None, 2 * LANES, tq), lambda b, p, i: (b, p, i)),
                  pl.BlockSpec((None, seq, 2 * LANES), lambda b, p, i: (b, 0, p)),
                  pl.BlockSpec((None, LANES, seq), lambda b, p, i: (b, p, 0))],
        out_specs=pl.BlockSpec((None, tq, LANES), lambda b, p, i: (b, i, p)),
        scratch_shapes=([pltpu.VMEM((2 * tq // ATTN_QCHUNK, tk, ATTN_QCHUNK), F32)] * 2
                        + [pltpu.VMEM((2 * tq // ATTN_QCHUNK, 1, ATTN_QCHUNK), F32)] * 2),
        compiler_params=pltpu.CompilerParams(dimension_semantics=("arbitrary", "arbitrary", "arbitrary"),
                                             vmem_limit_bytes=VMEM_LIMIT),
        name="prompt_attention",
    )(qat, ka3, vt)
    return out.reshape(nb * seq, ATT_WIDTH)


class _SampleAttention:
    SCORE_ROWS = N_HEADS * SAMPLE_T
    NT_DIMS = (((1,), (1,)), ((), ()))

    def __init__(self, q_scr, m_scr, l_scr, acc_scr, carry_scr):
        self.q_scr, self.m_scr, self.l_scr, self.acc_scr, self.carry_scr = q_scr, m_scr, l_scr, acc_scr, carry_scr

    @staticmethod
    def scratch_shapes():
        rows = _SampleAttention.SCORE_ROWS
        return [pltpu.VMEM((rows, ATT_WIDTH), BF16), pltpu.VMEM((rows, 1), F32), pltpu.VMEM((rows, 1), F32),
                pltpu.VMEM((rows, ATT_WIDTH), F32), pltpu.VMEM((N_HEADS, LANES), F32)]

    @staticmethod
    def per_head_rows(a):
        return jnp.concatenate([jnp.broadcast_to(a[h:h + 1, :], (SAMPLE_T, a.shape[1])) for h in range(N_HEADS)],
                               axis=0)

    def update(self, s, pv):
        m_prev = self.m_scr[...]
        m_new = jnp.maximum(m_prev, jnp.max(s, axis=-1, keepdims=True))
        alpha = jnp.exp(m_prev - m_new)
        p = jnp.exp(s - m_new)
        self.l_scr[...] = alpha * self.l_scr[...] + jnp.sum(p, axis=-1, keepdims=True)
        self.acc_scr[...] = alpha * self.acc_scr[...] + pv(p.astype(BF16))
        self.m_scr[...] = m_new

    def start(self, q_ref, kn_ref, vn_ref, cn_ref):
        rows, q_scr, nt_dims, per_head_rows = self.SCORE_ROWS, self.q_scr, self.NT_DIMS, self.per_head_rows
        m_scr, l_scr, acc_scr, carry_scr, update = self.m_scr, self.l_scr, self.acc_scr, self.carry_scr, self.update
        q8 = q_ref[...].astype(F32)
        qt = jnp.concatenate([q8] * N_HEADS, axis=0)
        r_i = lax.broadcasted_iota(jnp.int32, (rows, ATT_WIDTH), 0)
        c_i = lax.broadcasted_iota(jnp.int32, (rows, ATT_WIDTH), 1)
        q_scr[...] = jnp.where(r_i // SAMPLE_T == c_i // HEAD_DIM, qt, 0.0).astype(BF16)
        m_scr[...] = jnp.full_like(m_scr, NEG_INF)
        l_scr[...] = jnp.zeros_like(l_scr)
        acc_scr[...] = jnp.zeros_like(acc_scr)
        carry_scr[...] = jnp.zeros_like(carry_scr)
        pad = jnp.zeros((PAGE_SIZE - SAMPLE_T, ATT_WIDTH), F32)
        k_new = jnp.concatenate([kn_ref[...], pad], axis=0).astype(BF16)
        v_new = jnp.concatenate([vn_ref[...], pad], axis=0).astype(BF16)
        c_new = jnp.concatenate([cn_ref[...], jnp.zeros((PAGE_SIZE - SAMPLE_T, LANES), F32)], axis=0)
        s = lax.dot_general(q_scr[...], k_new, nt_dims, preferred_element_type=F32)
        s = s - per_head_rows(c_new.T[:N_HEADS, :])
        r_s = lax.broadcasted_iota(jnp.int32, (rows, PAGE_SIZE), 0)
        c_s = lax.broadcasted_iota(jnp.int32, (rows, PAGE_SIZE), 1)
        update(jnp.where(c_s <= r_s % SAMPLE_T, s, NEG_INF),
               lambda p: jnp.dot(p, v_new, preferred_element_type=F32))

    def page_scores(self, k_pages, lf_pages, tri_ref):
        pieces = []
        for lf in lf_pages:
            hi = lf.astype(BF16).astype(F32)
            mid = (lf - hi).astype(BF16).astype(F32)
            lo = (lf - hi - mid).astype(BF16).astype(F32)
            pieces += [hi, mid, lo]
        sums = jnp.dot(jnp.concatenate(pieces, axis=0).astype(BF16), tri_ref[...], preferred_element_type=F32)
        carry = self.carry_scr[...]
        bias = []
        for i in range(len(lf_pages)):
            part = [sums[(3 * i + j) * N_HEADS:(3 * i + j + 1) * N_HEADS, :] for j in range(AUG)]
            local = part[0] + part[1] + part[2]
            bias.append(local[:, :PAGE_SIZE] + carry)
            carry = carry + local[:, PAGE_SIZE:]
        self.carry_scr[...] = carry
        kt = jnp.concatenate([k.astype(BF16) for k in k_pages], axis=1)
        s = jnp.dot(self.q_scr[...], kt, preferred_element_type=F32)
        return s + self.per_head_rows(jnp.concatenate(bias, axis=1))

    def absorb_pages(self, s, v_pages):
        vt = jnp.concatenate([v.astype(BF16) for v in v_pages], axis=1)
        self.update(s, lambda p: lax.dot_general(p, vt, self.NT_DIMS, preferred_element_type=F32))

    def finish(self, o_ref):
        rows = self.SCORE_ROWS
        o = self.acc_scr[...] / self.l_scr[...]
        r_i = lax.broadcasted_iota(jnp.int32, (rows, ATT_WIDTH), 0)
        c_i = lax.broadcasted_iota(jnp.int32, (rows, ATT_WIDTH), 1)
        o = jnp.where(r_i // SAMPLE_T == c_i // HEAD_DIM, o, 0.0).astype(BF16)
        t_i = lax.broadcasted_iota(jnp.int32, (SAMPLE_T, rows), 0)
        r_j = lax.broadcasted_iota(jnp.int32, (SAMPLE_T, rows), 1)
        sel = jnp.where(r_j % SAMPLE_T == t_i, 1.0, 0.0).astype(BF16)
        o_ref[...] = jnp.dot(sel, o, preferred_element_type=F32).astype(o_ref.dtype)


def _merge_tail(rows, conv, g1, x_ref, o_ref, ga_ref, gb_ref, dwb_ref, lng_ref, lnb_ref, wpa_ref, wpb_ref, bpb_ref,
                wo_ref, out_ref):
    yb = conv + dwb_ref[...]
    mu = jnp.mean(yb, axis=-1, keepdims=True)
    var = jnp.mean(jnp.square(yb - mu), axis=-1, keepdims=True)
    yb = (yb - mu) * lax.rsqrt(var + EPS) * lng_ref[...] + lnb_ref[...]
    yb = jnp.dot(_silu(yb).astype(BF16), wpb_ref[...], preferred_element_type=F32) + bpb_ref[...]
    ya = jnp.dot(o_ref[rows, :], wpa_ref[...], preferred_element_type=F32)
    m = ga_ref[rows, :].astype(F32) * ya + gb_ref[rows, :].astype(F32) * yb
    out_ref[rows, :] = x_ref[rows, :] + g1 * jnp.dot(m.astype(BF16), wo_ref[...], preferred_element_type=F32)


def _merge_prompt_kernel(x_ref, o_ref, ucur_ref, uprev_ref, ga_ref, gb_ref, g1_ref, dww_ref, dwb_ref, lng_ref,
                         lnb_ref, wpa_ref, wpb_ref, bpb_ref, wo_ref, out_ref, ubuf, *, nseq, row_block):
    tm = x_ref.shape[0]
    first = pl.program_id(1) == 0
    prev = uprev_ref[...]
    ubuf[:, 0:CONV_HALO, :] = jnp.where(first, jnp.zeros_like(prev), prev)
    ubuf[:, CONV_HALO:, :] = ucur_ref[...]
    g1 = _mod_rows(g1_ref, nseq, False)
    chunk = 64
    for b0 in range(0, tm, row_block):
        slabs = []
        for c in range(N_SLABS):
            pieces = []
            for r0 in range(b0, b0 + row_block, chunk):
                acc = jnp.zeros((chunk, LANES), F32)
                for j in range(CONV_K):
                    off = r0 + CONV_HALO - (CONV_K - 1) + j
                    acc = acc + dww_ref[c, j:j + 1, :] * ubuf[c, off:off + chunk, :]
                pieces.append(acc)
            slabs.append(jnp.concatenate(pieces, axis=0))
        _merge_tail(slice(b0, b0 + row_block), jnp.concatenate(slabs, axis=1), g1, x_ref, o_ref, ga_ref, gb_ref,
                    dwb_ref, lng_ref, lnb_ref, wpa_ref, wpb_ref, bpb_ref, wo_ref, out_ref)


def _merge_sample_kernel(x_ref, o_ref, uwin_ref, ga_ref, gb_ref, g1_ref, dww_ref, dwb_ref, lng_ref,
                         lnb_ref, wpa_ref, wpb_ref, bpb_ref, wo_ref, out_ref, *, nseq):
    slabs = []
    for c in range(N_SLABS):
        acc = jnp.zeros((nseq, SAMPLE_T, LANES), F32)
        for j in range(CONV_K):
            acc = acc + dww_ref[c, j:j + 1, :] * uwin_ref[c, :, j:j + SAMPLE_T, :]
        slabs.append(acc.reshape(nseq * SAMPLE_T, LANES))
    _merge_tail(slice(None), jnp.concatenate(slabs, axis=1), _mod_rows(g1_ref, nseq, True), x_ref, o_ref, ga_ref,
                gb_ref, dwb_ref, lng_ref, lnb_ref, wpa_ref, wpb_ref, bpb_ref, wo_ref, out_ref)


def _merge(x, o, u_args, ga, gb, mod, weights, *, nb, tm, sample, nseq):
    n = x.shape[0]
    nt = n // (nb * tm)
    row = lambda b, i: (b * nt + i, 0)
    const2 = lambda b, i: (0, 0)
    const3 = lambda b, i: (0, 0, 0)
    once = dict(pipeline_mode=pl.Buffered(1))
    mod_spec = _mod_spec(mod, 2)
    if sample:
        (uwin,) = u_args
        u_specs = [pl.BlockSpec(uwin.shape, lambda b, i: (0, 0, 0, 0))]
        kernel, scratch = functools.partial(_merge_sample_kernel, nseq=nseq), []
    else:
        (u,) = u_args
        u_args = (u, u)
        blocks_per_tile = tm // CONV_HALO
        u_specs = [pl.BlockSpec((N_SLABS, tm, LANES), lambda b, i: (0, b * nt + i, 0)),
                   pl.BlockSpec((N_SLABS, CONV_HALO, LANES),
                                lambda b, i: (0, jnp.maximum((b * nt + i) * blocks_per_tile - 1, 0), 0))]
        kernel = functools.partial(_merge_prompt_kernel, nseq=nseq, row_block=tm // 2)
        scratch = [pltpu.VMEM((N_SLABS, CONV_HALO + tm, LANES), F32)]
    dww, dwb, lng, lnb, wpa, wpb, bpb, wo = weights
    in_specs = ([pl.BlockSpec((tm, D_MODEL), row), pl.BlockSpec((tm, ATT_WIDTH), row)] + u_specs
                + [pl.BlockSpec((tm, D_MODEL), row), pl.BlockSpec((tm, D_MODEL), row), mod_spec,
                   pl.BlockSpec(dww.shape, const3),
                   pl.BlockSpec((1, CONV_WIDTH), const2), pl.BlockSpec((1, CONV_WIDTH), const2),
                   pl.BlockSpec((1, CONV_WIDTH), const2),
                   pl.BlockSpec((ATT_WIDTH, D_MODEL), const2, **once),
                   pl.BlockSpec((CONV_WIDTH, D_MODEL), const2, **once),
                   pl.BlockSpec((1, D_MODEL), const2),
                   pl.BlockSpec((D_MODEL, D_MODEL), const2, **once)])
    return pl.pallas_call(
        kernel,
        out_shape=jax.ShapeDtypeStruct((n, D_MODEL), F32),
        grid=(nb, nt),
        in_specs=in_specs,
        out_specs=pl.BlockSpec((tm, D_MODEL), row),
        scratch_shapes=scratch,
        compiler_params=pltpu.CompilerParams(dimension_semantics=("arbitrary", "arbitrary"),
                                             vmem_limit_bytes=VMEM_LIMIT),
        name="merge_sample" if sample else "merge_prompt",
    )(x, o, *u_args, ga, gb, mod, dww, dwb, lng, lnb, wpa, wpb, bpb, wo)


def _ffn_steps(x_ref, sh_ref, sc_ref, g2_ref, rg_ref, fg_ref, win_ref, wout_ref, out_ref, *, bounds, sample, nseq):
    x = x_ref[...]
    ms = jnp.mean(x * x, axis=-1, keepdims=True)
    h = x * lax.rsqrt(ms + EPS) * rg_ref[...]
    hb = (h * (1.0 + _mod_rows(sc_ref, nseq, sample)) + _mod_rows(sh_ref, nseq, sample)).astype(BF16)
    acc = jnp.zeros(x.shape, F32)
    for lo, hi in zip(bounds[:-1], bounds[1:]):
        gate = jnp.dot(hb, win_ref[:, lo:hi], preferred_element_type=F32)
        up = jnp.dot(hb, win_ref[:, FFN_HIDDEN + lo:FFN_HIDDEN + hi], preferred_element_type=F32)
        yield
        act = (_silu(gate) * up).astype(BF16)
        acc = acc + jnp.dot(act, wout_ref[lo:hi, :], preferred_element_type=F32)
        if hi == bounds[-1]:
            x2 = x + _mod_rows(g2_ref, nseq, sample) * acc
            ms2 = jnp.mean(x2 * x2, axis=-1, keepdims=True)
            out_ref[...] = x2 * lax.rsqrt(ms2 + EPS) * fg_ref[...]
        yield


def _hidden_bounds(n_chunks):
    tiles = FFN_HIDDEN // MXU_TILE
    assert tiles * MXU_TILE == FFN_HIDDEN
    return [MXU_TILE * ((tiles * k) // n_chunks) for k in range(n_chunks + 1)]


def _ffn_kernel(*refs, sample, nseq):
    for _ in _ffn_steps(*refs, bounds=_hidden_bounds(2), sample=sample, nseq=nseq):
        pass


def _ffn_attn_kernel(pt_ref, x_ref, sh_ref, sc_ref, g2_ref, rg_ref, fg_ref, win_ref, wout_ref,
                     q_ref, kn_ref, vn_ref, cn_ref, tri_ref, ck_hbm, cv_hbm, clf_hbm,
                     out_ref, o_ref, kbuf, vbuf, lbuf, sem, *state, nseq, n_pages):
    npg = PAGES_PER_STEP
    n_chunks = n_pages // npg
    seq = pl.program_id(0) * pl.num_programs(1) + pl.program_id(1)
    attn = _SampleAttention(*state)

    def copies(sequence, c):
        slot, out = c % 2, []
        for j in range(npg):
            page = pt_ref[sequence * n_pages + (n_pages - 1) - (c * npg + j)]
            out += [pltpu.make_async_copy(ck_hbm.at[page], kbuf.at[slot, j], sem.at[0, slot]),
                    pltpu.make_async_copy(cv_hbm.at[page], vbuf.at[slot, j], sem.at[1, slot]),
                    pltpu.make_async_copy(clf_hbm.at[page], lbuf.at[slot, j], sem.at[2, slot])]
        return out

    def start(sequence, c):
        for cp in copies(sequence, c):
            cp.start()

    @pl.when(seq == 0)
    def _():
        start(seq, 0)

    ffn = _ffn_steps(x_ref, sh_ref, sc_ref, g2_ref, rg_ref, fg_ref, win_ref, wout_ref, out_ref,
                     bounds=_hidden_bounds(n_chunks), sample=False, nseq=nseq)
    for c in range(n_chunks):
        for cp in copies(seq, c):
            cp.wait()
        if c + 1 < n_chunks:
            start(seq, c + 1)
        else:
            @pl.when(seq + 1 < nseq)
            def _():
                start(seq + 1, 0)
        if c == 0:
            attn.start(q_ref, kn_ref, vn_ref, cn_ref)
        slot = c % 2
        next(ffn)
        s = attn.page_scores([kbuf[slot, j] for j in range(npg)], [lbuf[slot, j] for j in range(npg)], tri_ref)
        next(ffn)
        attn.absorb_pages(s, [vbuf[slot, j] for j in range(npg)])
    attn.finish(o_ref)


def _ffn(x, mod, rms_g, final_g, w_in, w_out, *, nb, tm, sample, nseq):
    n = x.shape[0]
    nt = n // (nb * tm)
    row = lambda b, i: (b * nt + i, 0)
    const2 = lambda b, i: (0, 0)
    once = dict(pipeline_mode=pl.Buffered(1))
    return pl.pallas_call(
        functools.partial(_ffn_kernel, sample=sample, nseq=nseq),
        out_shape=jax.ShapeDtypeStruct((n, D_MODEL), F32),
        grid=(nb, nt),
        in_specs=[pl.BlockSpec((tm, D_MODEL), row), _mod_spec(mod, 3), _mod_spec(mod, 4), _mod_spec(mod, 5),
                  pl.BlockSpec((1, D_MODEL), const2), pl.BlockSpec((1, D_MODEL), const2),
                  pl.BlockSpec((D_MODEL, 2 * FFN_HIDDEN), const2, **once),
                  pl.BlockSpec((FFN_HIDDEN, D_MODEL), const2, **once)],
        out_specs=pl.BlockSpec((tm, D_MODEL), row),
        compiler_params=pltpu.CompilerParams(dimension_semantics=("arbitrary", "arbitrary"),
                                             vmem_limit_bytes=VMEM_LIMIT),
        name="ffn_sample" if sample else "ffn_prompt",
    )(x, mod, mod, mod, rms_g, final_g, w_in, w_out)


def _ffn_with_sample_attention(x, mod, rms_g, final_g, w_in, w_out, page_table, q, k_new, v_new, c_new, tri_page,
                               cache_k, cache_v, cache_logf, *, nb, tm):
    n = x.shape[0]
    nt = n // (nb * tm)
    nseq, n_pages = page_table.shape
    assert nseq == nb * nt and n_pages % PAGES_PER_STEP == 0
    row = lambda b, i, pt: (b * nt + i, 0)
    const2 = lambda b, i, pt: (0, 0)
    once = dict(pipeline_mode=pl.Buffered(1))
    mod_spec = lambda kind: pl.BlockSpec((None, mod.shape[1], D_MODEL), lambda b, i, pt: (kind, 0, 0))
    seq_spec = lambda width: pl.BlockSpec((None, SAMPLE_T, width), lambda b, i, pt: (b * nt + i, 0, 0))
    hbm = pl.BlockSpec(memory_space=pl.ANY)
    slots = 2
    grid_spec = pltpu.PrefetchScalarGridSpec(
        num_scalar_prefetch=1,
        grid=(nb, nt),
        in_specs=[pl.BlockSpec((tm, D_MODEL), row), mod_spec(3), mod_spec(4), mod_spec(5),
                  pl.BlockSpec((1, D_MODEL), const2), pl.BlockSpec((1, D_MODEL), const2),
                  pl.BlockSpec((D_MODEL, 2 * FFN_HIDDEN), const2, **once),
                  pl.BlockSpec((FFN_HIDDEN, D_MODEL), const2, **once),
                  seq_spec(ATT_WIDTH), seq_spec(ATT_WIDTH), seq_spec(ATT_WIDTH), seq_spec(LANES),
                  pl.BlockSpec((PAGE_SIZE, 2 * PAGE_SIZE), const2), hbm, hbm, hbm],
        out_specs=[pl.BlockSpec((tm, D_MODEL), row), seq_spec(ATT_WIDTH)],
        scratch_shapes=[pltpu.VMEM((slots, PAGES_PER_STEP, ATT_WIDTH, PAGE_SIZE), F32),
                        pltpu.VMEM((slots, PAGES_PER_STEP, ATT_WIDTH, PAGE_SIZE), F32),
                        pltpu.VMEM((slots, PAGES_PER_STEP, N_HEADS, PAGE_SIZE), F32),
                        pltpu.SemaphoreType.DMA((3, slots))] + _SampleAttention.scratch_shapes())
    return pl.pallas_call(
        functools.partial(_ffn_attn_kernel, nseq=nseq, n_pages=n_pages),
        out_shape=[jax.ShapeDtypeStruct((n, D_MODEL), F32),
                   jax.ShapeDtypeStruct((nseq, SAMPLE_T, ATT_WIDTH), BF16)],
        grid_spec=grid_spec,
        compiler_params=pltpu.CompilerParams(dimension_semantics=("arbitrary", "arbitrary"),
                                             vmem_limit_bytes=VMEM_LIMIT),
        name="ffn_prompt_sample_attention",
    )(page_table.reshape(-1), x, mod, mod, mod, rms_g, final_g, w_in, w_out, q, k_new, v_new, c_new, tri_page,
      cache_k, cache_v, cache_logf)


def _bias_placement():
    half = N_HEADS * LANES
    place = np.zeros((LANES, 2 * half), np.float32)
    ones = np.zeros((1, 2 * half), np.float32)
    for h in range(N_HEADS):
        base = h * LANES + (HEAD_DIM if h % 2 == 0 else 0)
        for piece in range(AUG):
            place[piece * N_HEADS + h, base + piece] = 1.0
            ones[0, base + AUG + piece] = 1.0
            ones[0, half + base + piece] = 1.0
            place[piece * N_HEADS + h, half + base + AUG + piece] = -1.0
    return jnp.asarray(place, BF16), jnp.asarray(ones, F32)


def _lower_tri(n, block):
    t = np.arange(n)[:, None]
    s = np.arange(n)[None, :]
    return jnp.asarray(((s <= t) & (t // block == s // block)).astype(np.float32), BF16)


def _later_keys(n):
    j = np.arange(n)[:, None]
    s = np.arange(n)[None, :]
    return jnp.asarray(np.concatenate([(j > s).astype(np.float32), np.ones((n, n), np.float32)], axis=1), BF16)


def kernel(x_prompt, x_sample, c_prompt, c_sample, cache_k, cache_v, cache_logf, state_conv, page_table, rms1_g, rms2_g, w_ada, b_ada, w_in, b_in, dw_w, dw_b, ln_g, ln_b, w_pa, w_pb, b_pb, w_o, w_ffn_in, w_ffn_out, final_g):
    nb, seq, _ = x_prompt.shape
    nseq, dec_t, _ = x_sample.shape
    depth = w_in.shape[0]
    assert depth == 1 and dec_t <= SAMPLE_T
    n_prompt = nb * seq
    tm = 512
    n_sample = nseq * SAMPLE_T

    wt, b = w_in[0].T, b_in[0]
    g_off = 3 * ATT_WIDTH + N_HEADS
    w_all = jnp.concatenate([wt[:g_off], jnp.zeros((LANES - N_HEADS, D_MODEL), F32), wt[g_off:]], axis=0).astype(BF16)
    b_all = jnp.concatenate([b[:g_off], jnp.zeros((LANES - N_HEADS,), F32), b[g_off:]])[None, :]
    dww = jnp.pad(dw_w[0], ((0, CONV_HALO - CONV_K), (0, 0))).reshape(CONV_HALO, N_SLABS, LANES).transpose(1, 0, 2)
    merge_w = (dww, dw_b[0][None, :], ln_g[0][None, :], ln_b[0][None, :], w_pa[0].astype(BF16),
               w_pb[0].astype(BF16), b_pb[0][None, :], w_o[0].astype(BF16))
    wf_in, wf_out = w_ffn_in[0].astype(BF16), w_ffn_out[0].astype(BF16)
    g1w, g2w, gfw = rms1_g[0][None, :], rms2_g[0][None, :], final_g[None, :]
    place, ones_row = _bias_placement()

    n_cond = nb + nseq
    c_all = jnp.pad(jnp.concatenate([c_sample, c_prompt], axis=0), ((0, -n_cond % 8), (0, 0)))
    mod = _modulation(c_all, w_ada[0], b_ada[0][None, :])

    xs = jnp.pad(x_sample, ((0, 0), (0, SAMPLE_T - dec_t), (0, 0))).reshape(n_sample, D_MODEL)
    (q_s, k_s, v_s, lf_s, c_s, u_s, ga_s, gb_s) = _inproj(
        xs, mod, g1w, w_all, b_all, _lower_tri(n_sample, SAMPLE_T), place, ones_row,
        nb=1, tm=n_sample, sample=True, nseq=nseq)

    xp = x_prompt.reshape(n_prompt, D_MODEL)
    (qa, ka, kt_p, vt_p, vb_p, lft_p, u_p, ga_p, gb_p) = _inproj(
        xp, mod, g1w, w_all, b_all, _lower_tri(tm, tm), place, ones_row, nb=nb, tm=tm, sample=False, nseq=nseq)
    o_p = _prompt_attention(qa, ka, vb_p, nb=nb, seq=seq, tq=2 * tm)
    x1_p = _merge(xp, o_p, (u_p,), ga_p, gb_p, mod, merge_w, nb=nb, tm=tm, sample=False, nseq=nseq)
    n_phys = cache_k.shape[1]
    page_t = lambda c: c[0].transpose(0, 2, 3, 1).reshape(n_phys, ATT_WIDTH, PAGE_SIZE)
    y_p, o_s = _ffn_with_sample_attention(
        x1_p, mod, g2w, gfw, wf_in, wf_out,
        page_table, q_s.reshape(nseq, SAMPLE_T, ATT_WIDTH), k_s.reshape(nseq, SAMPLE_T, ATT_WIDTH),
        v_s.reshape(nseq, SAMPLE_T, ATT_WIDTH), c_s.reshape(nseq, SAMPLE_T, LANES), _later_keys(PAGE_SIZE),
        page_t(cache_k), page_t(cache_v), cache_logf[0].transpose(0, 2, 1), nb=nb, tm=tm)

    state_slabs = state_conv[0].reshape(nseq, CONV_K - 1, N_SLABS, LANES).transpose(2, 0, 1, 3)
    u_slabs = u_s.reshape(N_SLABS, nseq, SAMPLE_T, LANES)
    uwin = jnp.concatenate(
        [state_slabs, u_slabs, jnp.zeros((N_SLABS, nseq, SAMPLE_WIN - (CONV_K - 1) - SAMPLE_T, LANES), F32)], axis=2)
    x1_s = _merge(xs, o_s.reshape(n_sample, ATT_WIDTH), (uwin,), ga_s, gb_s, mod, merge_w,
                  nb=1, tm=n_sample, sample=True, nseq=nseq)
    y_s = _ffn(x1_s, mod, g2w, gfw, wf_in, wf_out, nb=1, tm=n_sample, sample=True, nseq=nseq)

    n_pg = seq // PAGE_SIZE
    tail = CONV_K - 1
    u_tail = u_p.reshape(N_SLABS, nb, seq, LANES)[:, :, seq - tail:]
    u_tail = u_tail.transpose(1, 2, 0, 3).reshape(nb, tail, CONV_WIDTH)
    us_rows = u_s.reshape(N_SLABS, nseq, SAMPLE_T, LANES)[:, :, :dec_t]
    us_rows = us_rows.transpose(1, 2, 0, 3).reshape(nseq, dec_t, CONV_WIDTH)
    unpad = lambda a, width: a.reshape(nseq, SAMPLE_T, *width)[:, :dec_t]
    return (y_p.reshape(nb, seq, D_MODEL),
            unpad(y_s, (D_MODEL,)),
            kt_p.reshape(1, nb, n_pg, N_HEADS, HEAD_DIM, PAGE_SIZE).transpose(0, 1, 2, 5, 3, 4),
            vt_p.reshape(1, nb, n_pg, N_HEADS, HEAD_DIM, PAGE_SIZE).transpose(0, 1, 2, 5, 3, 4),
            lft_p.reshape(1, nb, n_pg, N_HEADS, PAGE_SIZE).transpose(0, 1, 2, 4, 3),
            u_tail[None],
            unpad(k_s, (N_HEADS, HEAD_DIM))[None],
            unpad(v_s, (N_HEADS, HEAD_DIM))[None],
            unpad(lf_s, (N_HEADS,))[None],
            jnp.concatenate([state_conv[0][:, dec_t:], us_rows], axis=1)[None])
```

```python
import functools

import numpy as np
import jax
import jax.numpy as jnp
from jax import lax
from jax.experimental import pallas as pl
from jax.experimental.pallas import tpu as pltpu

F32 = jnp.float32
BF16 = jnp.bfloat16

D_MODEL = 1024
N_HEADS = 8
HEAD_DIM = 64
ATT_WIDTH = N_HEADS * HEAD_DIM
CONV_WIDTH = 512
CONV_K = 31
FFN_HIDDEN = 2816
PAGE_SIZE = 128
EPS = 1e-6
NEG_INF = -1e30
SCALE = HEAD_DIM ** -0.5
LOG2E = 1.4426950408889634

LANES = 128
MXU_TILE = 256
N_PAIRS = N_HEADS // 2
N_SLABS = CONV_WIDTH // LANES
AUG = 3
SUM_ROWS = 16
ATTN_QCHUNK = 256
SAMPLE_T = 8
CONV_HALO = 32
SAMPLE_WIN = 40
PAGES_PER_STEP = 16
VMEM_LIMIT = 56 * 1024 * 1024

C_Q, C_K, C_V, C_F, C_GLA, C_GLB, C_GA, C_GB, C_END = 0, 512, 1024, 1536, 1664, 2176, 2688, 3712, 4736


def _sigmoid(x):
    return 1.0 / (1.0 + jnp.exp(-x))


def _silu(x):
    return x * _sigmoid(x)


def _split3_packed(a, lane):
    a = jnp.where(lane < N_HEADS, a, 0.0)
    hi = a.astype(BF16).astype(F32)
    r1 = a - hi
    mid = r1.astype(BF16).astype(F32)
    lo = (r1 - mid).astype(BF16).astype(F32)
    packed = hi + pltpu.roll(mid, N_HEADS, axis=1) + pltpu.roll(lo, 2 * N_HEADS, axis=1)
    return packed.astype(BF16)


def _unpack3(p, lane):
    s = p + pltpu.roll(p, LANES - N_HEADS, axis=1) + pltpu.roll(p, LANES - 2 * N_HEADS, axis=1)
    return jnp.where(lane < N_HEADS, s, 0.0)


def _mod_kernel(c_ref, w_ref, b_ref, o_ref):
    s = _silu(c_ref[...]).astype(BF16)
    o_ref[...] = jnp.dot(s, w_ref[...].astype(BF16), preferred_element_type=F32) + b_ref[...]


def _modulation(c_all, w_ada, b_ada):
    rows = c_all.shape[0]
    n_mod = w_ada.shape[1] // D_MODEL
    return pl.pallas_call(
        _mod_kernel,
        out_shape=jax.ShapeDtypeStruct((n_mod, rows, D_MODEL), F32),
        grid=(n_mod,),
        in_specs=[pl.BlockSpec((rows, D_MODEL), lambda j: (0, 0)),
                  pl.BlockSpec((D_MODEL, D_MODEL), lambda j: (0, j)),
                  pl.BlockSpec((1, D_MODEL), lambda j: (0, j))],
        out_specs=pl.BlockSpec((None, rows, D_MODEL), lambda j: (j, 0, 0)),
        compiler_params=pltpu.CompilerParams(dimension_semantics=("arbitrary",), vmem_limit_bytes=VMEM_LIMIT),
        name="modulation",
    )(c_all, w_ada, b_ada)


def _mod_rows(ref, nseq, sample):
    if sample:
        m = ref[0:nseq, :]
        return jnp.broadcast_to(m[:, None, :], (nseq, SAMPLE_T, D_MODEL)).reshape(nseq * SAMPLE_T, D_MODEL)
    return ref[pl.ds(nseq + pl.program_id(0), 1), :]


def _mod_spec(mod, kind):
    return pl.BlockSpec((None, mod.shape[1], D_MODEL), lambda b, i: (kind, 0, 0))


def _inproj_kernel(x_ref, sh_ref, sc_ref, g_ref, w_ref, b_ref, tri_ref, place_ref, ones_ref, *refs, sample, nseq):
    if sample:
        qp_ref, k_ref, v_ref, lf_ref, c_ref, u_ref, ga_ref, gb_ref, carry_ref = refs
    else:
        qat_ref, ka_ref, kt_ref, vt_ref, vb_ref, lft_ref, u_ref, ga_ref, gb_ref, carry_ref = refs
    tm = x_ref.shape[0]

    @pl.when(pl.program_id(1) == 0)
    def _():
        carry_ref[...] = jnp.zeros_like(carry_ref)

    x = x_ref[...]
    ms = jnp.mean(x * x, axis=-1, keepdims=True)
    h = x * lax.rsqrt(ms + EPS) * g_ref[...]
    h = h * (1.0 + _mod_rows(sc_ref, nseq, sample)) + _mod_rows(sh_ref, nseq, sample)
    hb = h.astype(BF16)

    def seg(lo, hi):
        z = lax.dot_general(hb, w_ref[lo:hi, :], (((1,), (1,)), ((), ())), preferred_element_type=F32)
        return z + b_ref[:, lo:hi]

    lane = lax.broadcasted_iota(jnp.int32, (tm, LANES), 1)
    zf = seg(C_F, C_GLA)
    lf = jnp.minimum(zf, 0.0) - jnp.log1p(jnp.exp(-jnp.abs(zf)))
    lf = jnp.where(lane < N_HEADS, lf, 0.0)
    csum = _unpack3(jnp.dot(tri_ref[...], _split3_packed(lf, lane), preferred_element_type=F32), lane)
    csum = csum + carry_ref[...]
    carry_ref[...] = csum[tm - 1:tm, :]

    zq = seg(C_Q, C_K)
    zk = seg(C_K, C_V)
    zv = seg(C_V, C_F)
    if sample:
        qp_ref[...] = (zq * SCALE).astype(BF16)
        k_ref[...] = zk
        v_ref[...] = zv
        lf_ref[...] = lf[:, :N_HEADS]
        c_ref[...] = csum
    else:
        for pg in range(tm // PAGE_SIZE):
            rows = slice(pg * PAGE_SIZE, (pg + 1) * PAGE_SIZE)
            kt_ref[pg] = zk[rows, :].T
            vt_page = zv[rows, :].T
            vt_ref[pg] = vt_page
            vb_ref[:, rows] = vt_page.astype(BF16)
            lft_ref[pg] = lf[rows, :].T[:N_HEADS, :]
        zq = zq * (SCALE * LOG2E)
        aug = jnp.dot(_split3_packed(csum * LOG2E, lane), place_ref[...], preferred_element_type=F32) + ones_ref[...]
        for h in range(N_HEADS):
            own_half = (lane < HEAD_DIM) if h % 2 == 0 else (lane >= HEAD_DIM)
            pair, feat = slice((h // 2) * LANES, (h // 2 + 1) * LANES), slice(h * LANES, (h + 1) * LANES)
            qat_ref[feat, :] = jnp.where(own_half, zq[:, pair], aug[:, feat]).T.astype(BF16)
            k_feat = jnp.where(own_half, zk[:, pair], aug[:, N_HEADS * LANES + h * LANES:N_HEADS * LANES + (h + 1) * LANES])
            ka_ref[:, feat] = k_feat.astype(BF16)

    u = seg(C_GLA, C_GLB) * _sigmoid(seg(C_GLB, C_GA))
    for c in range(N_SLABS):
        u_ref[c] = u[:, c * LANES:(c + 1) * LANES]
    ga_ref[...] = _sigmoid(seg(C_GA, C_GB)).astype(BF16)
    gb_ref[...] = _sigmoid(seg(C_GB, C_END)).astype(BF16)


def _inproj(x, mod, g1, w_all, b_all, tri, place, ones_row, *, nb, tm, sample, nseq):
    n = x.shape[0]
    nt = n // (nb * tm)
    row = lambda b, i: (b * nt + i, 0)
    const2 = lambda b, i: (0, 0)
    once = dict(pipeline_mode=pl.Buffered(1))
    in_specs = [pl.BlockSpec((tm, D_MODEL), row), _mod_spec(mod, 0), _mod_spec(mod, 1),
                pl.BlockSpec((1, D_MODEL), const2),
                pl.BlockSpec((C_END, D_MODEL), const2, **once),
                pl.BlockSpec((1, C_END), const2),
                pl.BlockSpec((tm, tm), const2, **once),
                pl.BlockSpec(place.shape, const2, **once),
                pl.BlockSpec(ones_row.shape, const2)]
    shared_shape = [jax.ShapeDtypeStruct((N_SLABS, n, LANES), F32),
                    jax.ShapeDtypeStruct((n, D_MODEL), BF16),
                    jax.ShapeDtypeStruct((n, D_MODEL), BF16)]
    shared_specs = [pl.BlockSpec((N_SLABS, tm, LANES), lambda b, i: (0, b * nt + i, 0)),
                    pl.BlockSpec((tm, D_MODEL), row), pl.BlockSpec((tm, D_MODEL), row)]
    if sample:
        out_shape = [jax.ShapeDtypeStruct((n, ATT_WIDTH), BF16),
                     jax.ShapeDtypeStruct((n, ATT_WIDTH), F32),
                     jax.ShapeDtypeStruct((n, ATT_WIDTH), F32),
                     jax.ShapeDtypeStruct((n, N_HEADS), F32),
                     jax.ShapeDtypeStruct((n, LANES), F32)]
        out_specs = [pl.BlockSpec((tm, ATT_WIDTH), row), pl.BlockSpec((tm, ATT_WIDTH), row),
                     pl.BlockSpec((tm, ATT_WIDTH), row), pl.BlockSpec((tm, N_HEADS), row),
                     pl.BlockSpec((tm, LANES), row)]
    else:
        n_pg, pg_tile = n // PAGE_SIZE, tm // PAGE_SIZE
        page = lambda b, i: (b * nt + i, 0, 0)
        n_feat = N_HEADS * LANES
        out_shape = [jax.ShapeDtypeStruct((nb, n_feat, n // nb), BF16),
                     jax.ShapeDtypeStruct((n, n_feat), BF16),
                     jax.ShapeDtypeStruct((n_pg, ATT_WIDTH, PAGE_SIZE), F32),
                     jax.ShapeDtypeStruct((n_pg, ATT_WIDTH, PAGE_SIZE), F32),
                     jax.ShapeDtypeStruct((nb, ATT_WIDTH, n // nb), BF16),
                     jax.ShapeDtypeStruct((n_pg, N_HEADS, PAGE_SIZE), F32)]
        out_specs = [pl.BlockSpec((None, n_feat, tm), lambda b, i: (b, 0, i)), pl.BlockSpec((tm, n_feat), row),
                     pl.BlockSpec((pg_tile, ATT_WIDTH, PAGE_SIZE), page),
                     pl.BlockSpec((pg_tile, ATT_WIDTH, PAGE_SIZE), page),
                     pl.BlockSpec((None, ATT_WIDTH, tm), lambda b, i: (b, 0, i)),
                     pl.BlockSpec((pg_tile, N_HEADS, PAGE_SIZE), page)]
    out_shape += shared_shape
    out_specs += shared_specs
    return pl.pallas_call(
        functools.partial(_inproj_kernel, sample=sample, nseq=nseq),
        out_shape=out_shape,
        grid=(nb, nt),
        in_specs=in_specs,
        out_specs=out_specs,
        scratch_shapes=[pltpu.VMEM((1, LANES), F32)],
        compiler_params=pltpu.CompilerParams(dimension_semantics=("arbitrary", "arbitrary"),
                                             vmem_limit_bytes=VMEM_LIMIT),
        name="inproj_sample" if sample else "inproj_prompt",
    )(x, mod, mod, g1, w_all, b_all, tri, place, ones_row)


def _attn_kernel(qat_ref, ka_ref, vt_ref, o_ref, s_even, s_odd, mx_even, mx_odd, *, tq, tk):
    qi = pl.program_id(2)
    s_bufs = (s_even, s_odd)
    mx_bufs = (mx_even, mx_odd)
    q_heads = [qat_ref[hh * LANES:(hh + 1) * LANES, :] for hh in range(2)]
    key = lax.broadcasted_iota(jnp.int32, (tk, ATTN_QCHUNK), 0)
    qry = lax.broadcasted_iota(jnp.int32, (tk, ATTN_QCHUNK), 1)

    units = [(hh, slice(c, c + ATTN_QCHUNK), hh * (tq // ATTN_QCHUNK) + c // ATTN_QCHUNK)
             for hh in range(2) for c in range(0, tq, ATTN_QCHUNK)]

    def scores(ki, parity, unit, diagonal):
        hh, cols, u = unit
        ka = ka_ref[pl.ds(pl.multiple_of(ki * tk, tk), tk), hh * LANES:(hh + 1) * LANES]
        st = jnp.dot(ka, q_heads[hh][:, cols], preferred_element_type=F32)
        if diagonal:
            st = jnp.where(key + parity * tk <= qry + cols.start, st, NEG_INF)
        s_bufs[parity][u] = st
        mx_bufs[parity][u] = jnp.max(st, axis=0, keepdims=True)

    ones_rows = jnp.ones((SUM_ROWS, tk), BF16)

    def softmax_pv(ki, parity, unit, carry):
        hh, _, u = unit
        m_prev, acc = carry
        m_new = jnp.maximum(m_prev, mx_bufs[parity][u])
        alpha = jnp.exp2(m_prev - m_new)
        pt = jnp.exp2(s_bufs[parity][u] - m_new).astype(BF16)
        vt = vt_ref[hh * HEAD_DIM:(hh + 1) * HEAD_DIM, pl.ds(pl.multiple_of(ki * tk, tk), tk)]
        vt = jnp.concatenate([vt, ones_rows], axis=0)
        return m_new, acc * alpha + jnp.dot(vt, pt, preferred_element_type=F32)

    def stage(k_next, p_next, k_cur, p_cur, carry, diagonal=False):
        out = []
        for unit, c in zip(units, carry):
            scores(k_next, p_next, unit, diagonal)
            out.append(softmax_pv(k_cur, p_cur, unit, c))
        return tuple(out)

    unit0 = (jnp.full((1, ATTN_QCHUNK), NEG_INF, F32), jnp.zeros((HEAD_DIM + SUM_ROWS, ATTN_QCHUNK), F32))
    for unit in units:
        scores(2 * qi, 0, unit, True)
    carry = stage(2 * qi + 1, 1, 2 * qi, 0, (unit0,) * len(units), diagonal=True)

    def pair(j, carry):
        carry = stage(2 * j, 0, jnp.where(j == 0, 2 * qi + 1, 2 * j - 1), 1, carry)
        return stage(2 * j + 1, 1, 2 * j, 0, carry)

    carry = lax.fori_loop(0, qi, pair, carry)
    last = jnp.where(qi == 0, 1, 2 * qi - 1)
    heads = [[], []]
    for unit, c in zip(units, carry):
        _, acc = softmax_pv(last, 1, unit, c)
        heads[unit[0]].append(acc[:HEAD_DIM] / acc[HEAD_DIM:HEAD_DIM + 1])
    o = jnp.concatenate([jnp.concatenate(h, axis=1) for h in heads], axis=0)
    o_ref[...] = o.T.astype(o_ref.dtype)


def _prompt_attention(qat, ka, vt, *, nb, seq, tq):
    tk = tq // 2
    ka3 = ka.reshape(nb, seq, N_HEADS * LANES)
    out = pl.pallas_call(
        functools.partial(_attn_kernel, tq=tq, tk=tk),
        out_shape=jax.ShapeDtypeStruct((nb, seq, ATT_WIDTH), BF16),
        grid=(nb, N_PAIRS, seq // tq),
        in_specs=[pl.BlockSpec((None, 2 * LANES, tq), lambda b, p, i: (b, p, i)),
                  pl.BlockSpec((None, seq, 2 * LANES), lambda b, p, i: (b, 0, p)),
                  pl.BlockSpec((None, LANES, seq), lambda b, p, i: (b, p, 0))],
        out_specs=pl.BlockSpec((None, tq, LANES), lambda b, p, i: (b, i, p)),
        scratch_shapes=([pltpu.VMEM((2 * tq // ATTN_QCHUNK, tk, ATTN_QCHUNK), F32)] * 2
                        + [pltpu.VMEM((2 * tq // ATTN_QCHUNK, 1, ATTN_QCHUNK), F32)] * 2),
        compiler_params=pltpu.CompilerParams(dimension_semantics=("arbitrary", "arbitrary", "arbitrary"),
                                             vmem_limit_bytes=VMEM_LIMIT),
        name="prompt_attention",
    )(qat, ka3, vt)
    return out.reshape(nb * seq, ATT_WIDTH)


class _SampleAttention:
    SCORE_ROWS = N_HEADS * SAMPLE_T
    NT_DIMS = (((1,), (1,)), ((), ()))

    def __init__(self, q_scr, m_scr, l_scr, acc_scr, carry_scr):
        self.q_scr, self.m_scr, self.l_scr, self.acc_scr, self.carry_scr = q_scr, m_scr, l_scr, acc_scr, carry_scr

    @staticmethod
    def scratch_shapes():
        rows = _SampleAttention.SCORE_ROWS
        return [pltpu.VMEM((rows, ATT_WIDTH), BF16), pltpu.VMEM((rows, 1), F32), pltpu.VMEM((rows, 1), F32),
                pltpu.VMEM((rows, ATT_WIDTH), F32), pltpu.VMEM((N_HEADS, LANES), F32)]

    @staticmethod
    def per_head_rows(a):
        return jnp.concatenate([jnp.broadcast_to(a[h:h + 1, :], (SAMPLE_T, a.shape[1])) for h in range(N_HEADS)],
                               axis=0)

    def update(self, s, pv):
        m_prev = self.m_scr[...]
        m_new = jnp.maximum(m_prev, jnp.max(s, axis=-1, keepdims=True))
        alpha = jnp.exp(m_prev - m_new)
        p = jnp.exp(s - m_new)
        self.l_scr[...] = alpha * self.l_scr[...] + jnp.sum(p, axis=-1, keepdims=True)
        self.acc_scr[...] = alpha * self.acc_scr[...] + pv(p.astype(BF16))
        self.m_scr[...] = m_new

    def start(self, q_ref, kn_ref, vn_ref, cn_ref):
        rows, q_scr, nt_dims, per_head_rows = self.SCORE_ROWS, self.q_scr, self.NT_DIMS, self.per_head_rows
        m_scr, l_scr, acc_scr, carry_scr, update = self.m_scr, self.l_scr, self.acc_scr, self.carry_scr, self.update
        q8 = q_ref[...].astype(F32)
        qt = jnp.concatenate([q8] * N_HEADS, axis=0)
        r_i = lax.broadcasted_iota(jnp.int32, (rows, ATT_WIDTH), 0)
        c_i = lax.broadcasted_iota(jnp.int32, (rows, ATT_WIDTH), 1)
        q_scr[...] = jnp.where(r_i // SAMPLE_T == c_i // HEAD_DIM, qt, 0.0).astype(BF16)
        m_scr[...] = jnp.full_like(m_scr, NEG_INF)
        l_scr[...] = jnp.zeros_like(l_scr)
        acc_scr[...] = jnp.zeros_like(acc_scr)
        carry_scr[...] = jnp.zeros_like(carry_scr)
        pad = jnp.zeros((PAGE_SIZE - SAMPLE_T, ATT_WIDTH), F32)
        k_new = jnp.concatenate([kn_ref[...], pad], axis=0).astype(BF16)
        v_new = jnp.concatenate([vn_ref[...], pad], axis=0).astype(BF16)
        c_new = jnp.concatenate([cn_ref[...], jnp.zeros((PAGE_SIZE - SAMPLE_T, LANES), F32)], axis=0)
        s = lax.dot_general(q_scr[...], k_new, nt_dims, preferred_element_type=F32)
        s = s - per_head_rows(c_new.T[:N_HEADS, :])
        r_s = lax.broadcasted_iota(jnp.int32, (rows, PAGE_SIZE), 0)
        c_s = lax.broadcasted_iota(jnp.int32, (rows, PAGE_SIZE), 1)
        update(jnp.where(c_s <= r_s % SAMPLE_T, s, NEG_INF),
               lambda p: jnp.dot(p, v_new, preferred_element_type=F32))

    def page_scores(self, k_pages, lf_pages, tri_ref):
        pieces = []
        for lf in lf_pages:
            hi = lf.astype(BF16).astype(F32)
            mid = (lf - hi).astype(BF16).astype(F32)
            lo = (lf - hi - mid).astype(BF16).astype(F32)
            pieces += [hi, mid, lo]
        sums = jnp.dot(jnp.concatenate(pieces, axis=0).astype(BF16), tri_ref[...], preferred_element_type=F32)
        carry = self.carry_scr[...]
        bias = []
        for i in range(len(lf_pages)):
            part = [sums[(3 * i + j) * N_HEADS:(3 * i + j + 1) * N_HEADS, :] for j in range(AUG)]
            local = part[0] + part[1] + part[2]
            bias.append(local[:, :PAGE_SIZE] + carry)
            carry = carry + local[:, PAGE_SIZE:]
        self.carry_scr[...] = carry
        kt = jnp.concatenate([k.astype(BF16) for k in k_pages], axis=1)
        s = jnp.dot(self.q_scr[...], kt, preferred_element_type=F32)
        return s + self.per_head_rows(jnp.concatenate(bias, axis=1))

    def absorb_pages(self, s, v_pages):
        vt = jnp.concatenate([v.astype(BF16) for v in v_pages], axis=1)
        self.update(s, lambda p: lax.dot_general(p, vt, self.NT_DIMS, preferred_element_type=F32))

    def finish(self, o_ref):
        rows = self.SCORE_ROWS
        o = self.acc_scr[...] / self.l_scr[...]
        r_i = lax.broadcasted_iota(jnp.int32, (rows, ATT_WIDTH), 0)
        c_i = lax.broadcasted_iota(jnp.int32, (rows, ATT_WIDTH), 1)
        o = jnp.where(r_i // SAMPLE_T == c_i // HEAD_DIM, o, 0.0).astype(BF16)
        t_i = lax.broadcasted_iota(jnp.int32, (SAMPLE_T, rows), 0)
        r_j = lax.broadcasted_iota(jnp.int32, (SAMPLE_T, rows), 1)
        sel = jnp.where(r_j % SAMPLE_T == t_i, 1.0, 0.0).astype(BF16)
        o_ref[...] = jnp.dot(sel, o, preferred_element_type=F32).astype(o_ref.dtype)


def _merge_tail(rows, conv, g1, x_ref, o_ref, ga_ref, gb_ref, dwb_ref, lng_ref, lnb_ref, wpa_ref, wpb_ref, bpb_ref,
                wo_ref, out_ref):
    yb = conv + dwb_ref[...]
    mu = jnp.mean(yb, axis=-1, keepdims=True)
    var = jnp.mean(jnp.square(yb - mu), axis=-1, keepdims=True)
    yb = (yb - mu) * lax.rsqrt(var + EPS) * lng_ref[...] + lnb_ref[...]
    yb = jnp.dot(_silu(yb).astype(BF16), wpb_ref[...], preferred_element_type=F32) + bpb_ref[...]
    ya = jnp.dot(o_ref[rows, :], wpa_ref[...], preferred_element_type=F32)
    m = ga_ref[rows, :].astype(F32) * ya + gb_ref[rows, :].astype(F32) * yb
    out_ref[rows, :] = x_ref[rows, :] + g1 * jnp.dot(m.astype(BF16), wo_ref[...], preferred_element_type=F32)


def _merge_prompt_kernel(x_ref, o_ref, ucur_ref, uprev_ref, ga_ref, gb_ref, g1_ref, dww_ref, dwb_ref, lng_ref,
                         lnb_ref, wpa_ref, wpb_ref, bpb_ref, wo_ref, out_ref, ubuf, *, nseq, row_block):
    tm = x_ref.shape[0]
    first = pl.program_id(1) == 0
    prev = uprev_ref[...]
    ubuf[:, 0:CONV_HALO, :] = jnp.where(first, jnp.zeros_like(prev), prev)
    ubuf[:, CONV_HALO:, :] = ucur_ref[...]
    g1 = _mod_rows(g1_ref, nseq, False)
    chunk = 64
    for b0 in range(0, tm, row_block):
        slabs = []
        for c in range(N_SLABS):
            pieces = []
            for r0 in range(b0, b0 + row_block, chunk):
                acc = jnp.zeros((chunk, LANES), F32)
                for j in range(CONV_K):
                    off = r0 + CONV_HALO - (CONV_K - 1) + j
                    acc = acc + dww_ref[c, j:j + 1, :] * ubuf[c, off:off + chunk, :]
                pieces.append(acc)
            slabs.append(jnp.concatenate(pieces, axis=0))
        _merge_tail(slice(b0, b0 + row_block), jnp.concatenate(slabs, axis=1), g1, x_ref, o_ref, ga_ref, gb_ref,
                    dwb_ref, lng_ref, lnb_ref, wpa_ref, wpb_ref, bpb_ref, wo_ref, out_ref)


def _merge_sample_kernel(x_ref, o_ref, uwin_ref, ga_ref, gb_ref, g1_ref, dww_ref, dwb_ref, lng_ref,
                         lnb_ref, wpa_ref, wpb_ref, bpb_ref, wo_ref, out_ref, *, nseq):
    slabs = []
    for c in range(N_SLABS):
        acc = jnp.zeros((nseq, SAMPLE_T, LANES), F32)
        for j in range(CONV_K):
            acc = acc + dww_ref[c, j:j + 1, :] * uwin_ref[c, :, j:j + SAMPLE_T, :]
        slabs.append(acc.reshape(nseq * SAMPLE_T, LANES))
    _merge_tail(slice(None), jnp.concatenate(slabs, axis=1), _mod_rows(g1_ref, nseq, True), x_ref, o_ref, ga_ref,
                gb_ref, dwb_ref, lng_ref, lnb_ref, wpa_ref, wpb_ref, bpb_ref, wo_ref, out_ref)


def _merge(x, o, u_args, ga, gb, mod, weights, *, nb, tm, sample, nseq):
    n = x.shape[0]
    nt = n // (nb * tm)
    row = lambda b, i: (b * nt + i, 0)
    const2 = lambda b, i: (0, 0)
    const3 = lambda b, i: (0, 0, 0)
    once = dict(pipeline_mode=pl.Buffered(1))
    mod_spec = _mod_spec(mod, 2)
    if sample:
        (uwin,) = u_args
        u_specs = [pl.BlockSpec(uwin.shape, lambda b, i: (0, 0, 0, 0))]
        kernel, scratch = functools.partial(_merge_sample_kernel, nseq=nseq), []
    else:
        (u,) = u_args
        u_args = (u, u)
        blocks_per_tile = tm // CONV_HALO
        u_specs = [pl.BlockSpec((N_SLABS, tm, LANES), lambda b, i: (0, b * nt + i, 0)),
                   pl.BlockSpec((N_SLABS, CONV_HALO, LANES),
                                lambda b, i: (0, jnp.maximum((b * nt + i) * blocks_per_tile - 1, 0), 0))]
        kernel = functools.partial(_merge_prompt_kernel, nseq=nseq, row_block=tm // 2)
        scratch = [pltpu.VMEM((N_SLABS, CONV_HALO + tm, LANES), F32)]
    dww, dwb, lng, lnb, wpa, wpb, bpb, wo = weights
    in_specs = ([pl.BlockSpec((tm, D_MODEL), row), pl.BlockSpec((tm, ATT_WIDTH), row)] + u_specs
                + [pl.BlockSpec((tm, D_MODEL), row), pl.BlockSpec((tm, D_MODEL), row), mod_spec,
                   pl.BlockSpec(dww.shape, const3),
                   pl.BlockSpec((1, CONV_WIDTH), const2), pl.BlockSpec((1, CONV_WIDTH), const2),
                   pl.BlockSpec((1, CONV_WIDTH), const2),
                   pl.BlockSpec((ATT_WIDTH, D_MODEL), const2, **once),
                   pl.BlockSpec((CONV_WIDTH, D_MODEL), const2, **once),
                   pl.BlockSpec((1, D_MODEL), const2),
                   pl.BlockSpec((D_MODEL, D_MODEL), const2, **once)])
    return pl.pallas_call(
        kernel,
        out_shape=jax.ShapeDtypeStruct((n, D_MODEL), F32),
        grid=(nb, nt),
        in_specs=in_specs,
        out_specs=pl.BlockSpec((tm, D_MODEL), row),
        scratch_shapes=scratch,
        compiler_params=pltpu.CompilerParams(dimension_semantics=("arbitrary", "arbitrary"),
                                             vmem_limit_bytes=VMEM_LIMIT),
        name="merge_sample" if sample else "merge_prompt",
    )(x, o, *u_args, ga, gb, mod, dww, dwb, lng, lnb, wpa, wpb, bpb, wo)


def _ffn_steps(x_ref, sh_ref, sc_ref, g2_ref, rg_ref, fg_ref, win_ref, wout_ref, out_ref, *, bounds, sample, nseq):
    x = x_ref[...]
    ms = jnp.mean(x * x, axis=-1, keepdims=True)
    h = x * lax.rsqrt(ms + EPS) * rg_ref[...]
    hb = (h * (1.0 + _mod_rows(sc_ref, nseq, sample)) + _mod_rows(sh_ref, nseq, sample)).astype(BF16)
    acc = jnp.zeros(x.shape, F32)
    for lo, hi in zip(bounds[:-1], bounds[1:]):
        gate = jnp.dot(hb, win_ref[:, lo:hi], preferred_element_type=F32)
        up = jnp.dot(hb, win_ref[:, FFN_HIDDEN + lo:FFN_HIDDEN + hi], preferred_element_type=F32)
        yield
        act = (_silu(gate) * up).astype(BF16)
        acc = acc + jnp.dot(act, wout_ref[lo:hi, :], preferred_element_type=F32)
        if hi == bounds[-1]:
            x2 = x + _mod_rows(g2_ref, nseq, sample) * acc
            ms2 = jnp.mean(x2 * x2, axis=-1, keepdims=True)
            out_ref[...] = x2 * lax.rsqrt(ms2 + EPS) * fg_ref[...]
        yield


def _hidden_bounds(n_chunks):
    tiles = FFN_HIDDEN // MXU_TILE
    assert tiles * MXU_TILE == FFN_HIDDEN
    return [MXU_TILE * ((tiles * k) // n_chunks) for k in range(n_chunks + 1)]


def _ffn_kernel(*refs, sample, nseq):
    for _ in _ffn_steps(*refs, bounds=_hidden_bounds(2), sample=sample, nseq=nseq):
        pass


def _ffn_attn_kernel(pt_ref, x_ref, sh_ref, sc_ref, g2_ref, rg_ref, fg_ref, win_ref, wout_ref,
                     q_ref, kn_ref, vn_ref, cn_ref, tri_ref, ck_hbm, cv_hbm, clf_hbm,
                     out_ref, o_ref, kbuf, vbuf, lbuf, sem, *state, nseq, n_pages):
    npg = PAGES_PER_STEP
    n_chunks = n_pages // npg
    seq = pl.program_id(0) * pl.num_programs(1) + pl.program_id(1)
    attn = _SampleAttention(*state)

    def copies(sequence, c):
        slot, out = c % 2, []
        for j in range(npg):
            page = pt_ref[sequence * n_pages + (n_pages - 1) - (c * npg + j)]
            out += [pltpu.make_async_copy(ck_hbm.at[page], kbuf.at[slot, j], sem.at[0, slot]),
                    pltpu.make_async_copy(cv_hbm.at[page], vbuf.at[slot, j], sem.at[1, slot]),
                    pltpu.make_async_copy(clf_hbm.at[page], lbuf.at[slot, j], sem.at[2, slot])]
        return out

    def start(sequence, c):
        for n, cp in enumerate(copies(sequence, c)):
            cp.start(priority=n % 2)

    @pl.when(seq == 0)
    def _():
        start(seq, 0)

    ffn = _ffn_steps(x_ref, sh_ref, sc_ref, g2_ref, rg_ref, fg_ref, win_ref, wout_ref, out_ref,
                     bounds=_hidden_bounds(n_chunks), sample=False, nseq=nseq)
    for c in range(n_chunks):
        for cp in copies(seq, c):
            cp.wait()
        if c + 1 < n_chunks:
            start(seq, c + 1)
        else:
            @pl.when(seq + 1 < nseq)
            def _():
                start(seq + 1, 0)
        if c == 0:
            attn.start(q_ref, kn_ref, vn_ref, cn_ref)
        slot = c % 2
        next(ffn)
        s = attn.page_scores([kbuf[slot, j] for j in range(npg)], [lbuf[slot, j] for j in range(npg)], tri_ref)
        next(ffn)
        attn.absorb_pages(s, [vbuf[slot, j] for j in range(npg)])
    attn.finish(o_ref)


def _ffn(x, mod, rms_g, final_g, w_in, w_out, *, nb, tm, sample, nseq):
    n = x.shape[0]
    nt = n // (nb * tm)
    row = lambda b, i: (b * nt + i, 0)
    const2 = lambda b, i: (0, 0)
    once = dict(pipeline_mode=pl.Buffered(1))
    return pl.pallas_call(
        functools.partial(_ffn_kernel, sample=sample, nseq=nseq),
        out_shape=jax.ShapeDtypeStruct((n, D_MODEL), F32),
        grid=(nb, nt),
        in_specs=[pl.BlockSpec((tm, D_MODEL), row), _mod_spec(mod, 3), _mod_spec(mod, 4), _mod_spec(mod, 5),
                  pl.BlockSpec((1, D_MODEL), const2), pl.BlockSpec((1, D_MODEL), const2),
                  pl.BlockSpec((D_MODEL, 2 * FFN_HIDDEN), const2, **once),
                  pl.BlockSpec((FFN_HIDDEN, D_MODEL), const2, **once)],
        out_specs=pl.BlockSpec((tm, D_MODEL), row),
        compiler_params=pltpu.CompilerParams(dimension_semantics=("arbitrary", "arbitrary"),
                                             vmem_limit_bytes=VMEM_LIMIT),
        name="ffn_sample" if sample else "ffn_prompt",
    )(x, mod, mod, mod, rms_g, final_g, w_in, w_out)


def _ffn_with_sample_attention(x, mod, rms_g, final_g, w_in, w_out, page_table, q, k_new, v_new, c_new, tri_page,
                               cache_k, cache_v, cache_logf, *, nb, tm):
    n = x.shape[0]
    nt = n // (nb * tm)
    nseq, n_pages = page_table.shape
    assert nseq == nb * nt and n_pages % PAGES_PER_STEP == 0
    row = lambda b, i, pt: (b * nt + i, 0)
    const2 = lambda b, i, pt: (0, 0)
    once = dict(pipeline_mode=pl.Buffered(1))
    mod_spec = lambda kind: pl.BlockSpec((None, mod.shape[1], D_MODEL), lambda b, i, pt: (kind, 0, 0))
    seq_spec = lambda width: pl.BlockSpec((None, SAMPLE_T, width), lambda b, i, pt: (b * nt + i, 0, 0))
    hbm = pl.BlockSpec(memory_space=pl.ANY)
    slots = 2
    grid_spec = pltpu.PrefetchScalarGridSpec(
        num_scalar_prefetch=1,
        grid=(nb, nt),
        in_specs=[pl.BlockSpec((tm, D_MODEL), row), mod_spec(3), mod_spec(4), mod_spec(5),
                  pl.BlockSpec((1, D_MODEL), const2), pl.BlockSpec((1, D_MODEL), const2),
                  pl.BlockSpec((D_MODEL, 2 * FFN_HIDDEN), const2, **once),
                  pl.BlockSpec((FFN_HIDDEN, D_MODEL), const2, **once),
                  seq_spec(ATT_WIDTH), seq_spec(ATT_WIDTH), seq_spec(ATT_WIDTH), seq_spec(LANES),
                  pl.BlockSpec((PAGE_SIZE, 2 * PAGE_SIZE), const2), hbm, hbm, hbm],
        out_specs=[pl.BlockSpec((tm, D_MODEL), row), seq_spec(ATT_WIDTH)],
        scratch_shapes=[pltpu.VMEM((slots, PAGES_PER_STEP, ATT_WIDTH, PAGE_SIZE), F32),
                        pltpu.VMEM((slots, PAGES_PER_STEP, ATT_WIDTH, PAGE_SIZE), F32),
                        pltpu.VMEM((slots, PAGES_PER_STEP, N_HEADS, PAGE_SIZE), F32),
                        pltpu.SemaphoreType.DMA((3, slots))] + _SampleAttention.scratch_shapes())
    return pl.pallas_call(
        functools.partial(_ffn_attn_kernel, nseq=nseq, n_pages=n_pages),
        out_shape=[jax.ShapeDtypeStruct((n, D_MODEL), F32),
                   jax.ShapeDtypeStruct((nseq, SAMPLE_T, ATT_WIDTH), BF16)],
        grid_spec=grid_spec,
        compiler_params=pltpu.CompilerParams(dimension_semantics=("arbitrary", "arbitrary"),
                                             vmem_limit_bytes=VMEM_LIMIT),
        name="ffn_prompt_sample_attention",
    )(page_table.reshape(-1), x, mod, mod, mod, rms_g, final_g, w_in, w_out, q, k_new, v_new, c_new, tri_page,
      cache_k, cache_v, cache_logf)


def _bias_placement():
    half = N_HEADS * LANES
    place = np.zeros((LANES, 2 * half), np.float32)
    ones = np.zeros((1, 2 * half), np.float32)
    for h in range(N_HEADS):
        base = h * LANES + (HEAD_DIM if h % 2 == 0 else 0)
        for piece in range(AUG):
            place[piece * N_HEADS + h, base + piece] = 1.0
            ones[0, base + AUG + piece] = 1.0
            ones[0, half + base + piece] = 1.0
            place[piece * N_HEADS + h, half + base + AUG + piece] = -1.0
    return jnp.asarray(place, BF16), jnp.asarray(ones, F32)


def _lower_tri(n, block):
    t = np.arange(n)[:, None]
    s = np.arange(n)[None, :]
    return jnp.asarray(((s <= t) & (t // block == s // block)).astype(np.float32), BF16)


def _later_keys(n):
    j = np.arange(n)[:, None]
    s = np.arange(n)[None, :]
    return jnp.asarray(np.concatenate([(j > s).astype(np.float32), np.ones((n, n), np.float32)], axis=1), BF16)


def kernel(x_prompt, x_sample, c_prompt, c_sample, cache_k, cache_v, cache_logf, state_conv, page_table, rms1_g, rms2_g, w_ada, b_ada, w_in, b_in, dw_w, dw_b, ln_g, ln_b, w_pa, w_pb, b_pb, w_o, w_ffn_in, w_ffn_out, final_g):
    nb, seq, _ = x_prompt.shape
    nseq, dec_t, _ = x_sample.shape
    depth = w_in.shape[0]
    assert depth == 1 and dec_t <= SAMPLE_T
    n_prompt = nb * seq
    tm = 512
    n_sample = nseq * SAMPLE_T

    wt, b = w_in[0].T, b_in[0]
    g_off = 3 * ATT_WIDTH + N_HEADS
    w_all = jnp.concatenate([wt[:g_off], jnp.zeros((LANES - N_HEADS, D_MODEL), F32), wt[g_off:]], axis=0).astype(BF16)
    b_all = jnp.concatenate([b[:g_off], jnp.zeros((LANES - N_HEADS,), F32), b[g_off:]])[None, :]
    dww = jnp.pad(dw_w[0], ((0, CONV_HALO - CONV_K), (0, 0))).reshape(CONV_HALO, N_SLABS, LANES).transpose(1, 0, 2)
    merge_w = (dww, dw_b[0][None, :], ln_g[0][None, :], ln_b[0][None, :], w_pa[0].astype(BF16),
               w_pb[0].astype(BF16), b_pb[0][None, :], w_o[0].astype(BF16))
    wf_in, wf_out = w_ffn_in[0].astype(BF16), w_ffn_out[0].astype(BF16)
    g1w, g2w, gfw = rms1_g[0][None, :], rms2_g[0][None, :], final_g[None, :]
    place, ones_row = _bias_placement()

    n_cond = nb + nseq
    c_all = jnp.pad(jnp.concatenate([c_sample, c_prompt], axis=0), ((0, -n_cond % 8), (0, 0)))
    mod = _modulation(c_all, w_ada[0], b_ada[0][None, :])

    xs = jnp.pad(x_sample, ((0, 0), (0, SAMPLE_T - dec_t), (0, 0))).reshape(n_sample, D_MODEL)
    (q_s, k_s, v_s, lf_s, c_s, u_s, ga_s, gb_s) = _inproj(
        xs, mod, g1w, w_all, b_all, _lower_tri(n_sample, SAMPLE_T), place, ones_row,
        nb=1, tm=n_sample, sample=True, nseq=nseq)

    xp = x_prompt.reshape(n_prompt, D_MODEL)
    (qa, ka, kt_p, vt_p, vb_p, lft_p, u_p, ga_p, gb_p) = _inproj(
        xp, mod, g1w, w_all, b_all, _lower_tri(tm, tm), place, ones_row, nb=nb, tm=tm, sample=False, nseq=nseq)
    o_p = _prompt_attention(qa, ka, vb_p, nb=nb, seq=seq, tq=2 * tm)
    x1_p = _merge(xp, o_p, (u_p,), ga_p, gb_p, mod, merge_w, nb=nb, tm=tm, sample=False, nseq=nseq)
    n_phys = cache_k.shape[1]
    page_t = lambda c: c[0].transpose(0, 2, 3, 1).reshape(n_phys, ATT_WIDTH, PAGE_SIZE)
    y_p, o_s = _ffn_with_sample_attention(
        x1_p, mod, g2w, gfw, wf_in, wf_out,
        page_table, q_s.reshape(nseq, SAMPLE_T, ATT_WIDTH), k_s.reshape(nseq, SAMPLE_T, ATT_WIDTH),
        v_s.reshape(nseq, SAMPLE_T, ATT_WIDTH), c_s.reshape(nseq, SAMPLE_T, LANES), _later_keys(PAGE_SIZE),
        page_t(cache_k), page_t(cache_v), cache_logf[0].transpose(0, 2, 1), nb=nb, tm=tm)

    state_slabs = state_conv[0].reshape(nseq, CONV_K - 1, N_SLABS, LANES).transpose(2, 0, 1, 3)
    u_slabs = u_s.reshape(N_SLABS, nseq, SAMPLE_T, LANES)
    uwin = jnp.concatenate(
        [state_slabs, u_slabs, jnp.zeros((N_SLABS, nseq, SAMPLE_WIN - (CONV_K - 1) - SAMPLE_T, LANES), F32)], axis=2)
    x1_s = _merge(xs, o_s.reshape(n_sample, ATT_WIDTH), (uwin,), ga_s, gb_s, mod, merge_w,
                  nb=1, tm=n_sample, sample=True, nseq=nseq)
    y_s = _ffn(x1_s, mod, g2w, gfw, wf_in, wf_out, nb=1, tm=n_sample, sample=True, nseq=nseq)

    n_pg = seq // PAGE_SIZE
    tail = CONV_K - 1
    u_tail = u_p.reshape(N_SLABS, nb, seq, LANES)[:, :, seq - tail:]
    u_tail = u_tail.transpose(1, 2, 0, 3).reshape(nb, tail, CONV_WIDTH)
    us_rows = u_s.reshape(N_SLABS, nseq, SAMPLE_T, LANES)[:, :, :dec_t]
    us_rows = us_rows.transpose(1, 2, 0, 3).reshape(nseq, dec_t, CONV_WIDTH)
    unpad = lambda a, width: a.reshape(nseq, SAMPLE_T, *width)[:, :dec_t]
    return (y_p.reshape(nb, seq, D_MODEL),
            unpad(y_s, (D_MODEL,)),
            kt_p.reshape(1, nb, n_pg, N_HEADS, HEAD_DIM, PAGE_SIZE).transpose(0, 1, 2, 5, 3, 4),
            vt_p.reshape(1, nb, n_pg, N_HEADS, HEAD_DIM, PAGE_SIZE).transpose(0, 1, 2, 5, 3, 4),
            lft_p.reshape(1, nb, n_pg, N_HEADS, PAGE_SIZE).transpose(0, 1, 2, 4, 3),
            u_tail[None],
            unpad(k_s, (N_HEADS, HEAD_DIM))[None],
            unpad(v_s, (N_HEADS, HEAD_DIM))[None],
            unpad(lf_s, (N_HEADS,))[None],
            jnp.concatenate([state_conv[0][:, dec_t:], us_rows], axis=1)[None])
```

```python
import functools

import numpy as np
import jax
import jax.numpy as jnp
from jax import lax
from jax.experimental import pallas as pl
from jax.experimental.pallas import tpu as pltpu

F32 = jnp.float32
BF16 = jnp.bfloat16

D_MODEL = 1024
N_HEADS = 8
HEAD_DIM = 64
ATT_WIDTH = N_HEADS * HEAD_DIM
CONV_WIDTH = 512
CONV_K = 31
FFN_HIDDEN = 2816
PAGE_SIZE = 128
EPS = 1e-6
NEG_INF = -1e30
SCALE = HEAD_DIM ** -0.5
LOG2E = 1.4426950408889634

LANES = 128
MXU_TILE = 256
N_PAIRS = N_HEADS // 2
N_SLABS = CONV_WIDTH // LANES
AUG = 3
SUM_ROWS = 16
ATTN_QCHUNK = 256
SAMPLE_T = 8
CONV_HALO = 32
SAMPLE_WIN = 40
PAGES_PER_STEP = 16
VMEM_LIMIT = 56 * 1024 * 1024

C_Q, C_K, C_V, C_F, C_GLA, C_GLB, C_GA, C_GB, C_END = 0, 512, 1024, 1536, 1664, 2176, 2688, 3712, 4736


def _sigmoid(x):
    return 1.0 / (1.0 + jnp.exp(-x))


def _silu(x):
    return x * _sigmoid(x)


def _split3_packed(a, lane):
    a = jnp.where(lane < N_HEADS, a, 0.0)
    hi = a.astype(BF16).astype(F32)
    r1 = a - hi
    mid = r1.astype(BF16).astype(F32)
    lo = (r1 - mid).astype(BF16).astype(F32)
    packed = hi + pltpu.roll(mid, N_HEADS, axis=1) + pltpu.roll(lo, 2 * N_HEADS, axis=1)
    return packed.astype(BF16)


def _unpack3(p, lane):
    s = p + pltpu.roll(p, LANES - N_HEADS, axis=1) + pltpu.roll(p, LANES - 2 * N_HEADS, axis=1)
    return jnp.where(lane < N_HEADS, s, 0.0)


def _mod_kernel(c_ref, w_ref, b_ref, o_ref):
    s = _silu(c_ref[...]).astype(BF16)
    o_ref[...] = jnp.dot(s, w_ref[...].astype(BF16), preferred_element_type=F32) + b_ref[...]


def _modulation(c_all, w_ada, b_ada):
    rows = c_all.shape[0]
    n_mod = w_ada.shape[1] // D_MODEL
    return pl.pallas_call(
        _mod_kernel,
        out_shape=jax.ShapeDtypeStruct((n_mod, rows, D_MODEL), F32),
        grid=(n_mod,),
        in_specs=[pl.BlockSpec((rows, D_MODEL), lambda j: (0, 0)),
                  pl.BlockSpec((D_MODEL, D_MODEL), lambda j: (0, j)),
                  pl.BlockSpec((1, D_MODEL), lambda j: (0, j))],
        out_specs=pl.BlockSpec((None, rows, D_MODEL), lambda j: (j, 0, 0)),
        compiler_params=pltpu.CompilerParams(dimension_semantics=("arbitrary",), vmem_limit_bytes=VMEM_LIMIT),
        name="modulation",
    )(c_all, w_ada, b_ada)


def _mod_rows(ref, nseq, sample):
    if sample:
        m = ref[0:nseq, :]
        return jnp.broadcast_to(m[:, None, :], (nseq, SAMPLE_T, D_MODEL)).reshape(nseq * SAMPLE_T, D_MODEL)
    return ref[pl.ds(nseq + pl.program_id(0), 1), :]


def _mod_spec(mod, kind):
    return pl.BlockSpec((None, mod.shape[1], D_MODEL), lambda b, i: (kind, 0, 0))


def _inproj_kernel(x_ref, sh_ref, sc_ref, g_ref, w_ref, b_ref, tri_ref, place_ref, ones_ref, *refs, sample, nseq):
    if sample:
        qp_ref, k_ref, v_ref, lf_ref, c_ref, u_ref, ga_ref, gb_ref, carry_ref = refs
    else:
        qat_ref, ka_ref, kt_ref, vt_ref, vb_ref, lft_ref, u_ref, ga_ref, gb_ref, carry_ref = refs
    tm = x_ref.shape[0]

    @pl.when(pl.program_id(1) == 0)
    def _():
        carry_ref[...] = jnp.zeros_like(carry_ref)

    x = x_ref[...]
    ms = jnp.mean(x * x, axis=-1, keepdims=True)
    h = x * lax.rsqrt(ms + EPS) * g_ref[...]
    h = h * (1.0 + _mod_rows(sc_ref, nseq, sample)) + _mod_rows(sh_ref, nseq, sample)
    hb = h.astype(BF16)

    def seg(lo, hi):
        z = lax.dot_general(hb, w_ref[lo:hi, :], (((1,), (1,)), ((), ())), preferred_element_type=F32)
        return z + b_ref[:, lo:hi]

    lane = lax.broadcasted_iota(jnp.int32, (tm, LANES), 1)
    zf = seg(C_F, C_GLA)
    lf = jnp.minimum(zf, 0.0) - jnp.log1p(jnp.exp(-jnp.abs(zf)))
    lf = jnp.where(lane < N_HEADS, lf, 0.0)
    csum = _unpack3(jnp.dot(tri_ref[...], _split3_packed(lf, lane), preferred_element_type=F32), lane)
    csum = csum + carry_ref[...]
    carry_ref[...] = csum[tm - 1:tm, :]

    zq = seg(C_Q, C_K)
    zk = seg(C_K, C_V)
    zv = seg(C_V, C_F)
    if sample:
        qp_ref[...] = (zq * SCALE).astype(BF16)
        k_ref[...] = zk
        v_ref[...] = zv
        lf_ref[...] = lf[:, :N_HEADS]
        c_ref[...] = csum
    else:
        for pg in range(tm // PAGE_SIZE):
            rows = slice(pg * PAGE_SIZE, (pg + 1) * PAGE_SIZE)
            kt_ref[pg] = zk[rows, :].T
            vt_page = zv[rows, :].T
            vt_ref[pg] = vt_page
            vb_ref[:, rows] = vt_page.astype(BF16)
            lft_ref[pg] = lf[rows, :].T[:N_HEADS, :]
        zq = zq * (SCALE * LOG2E)
        aug = jnp.dot(_split3_packed(csum * LOG2E, lane), place_ref[...], preferred_element_type=F32) + ones_ref[...]
        for h in range(N_HEADS):
            own_half = (lane < HEAD_DIM) if h % 2 == 0 else (lane >= HEAD_DIM)
            pair, feat = slice((h // 2) * LANES, (h // 2 + 1) * LANES), slice(h * LANES, (h + 1) * LANES)
            qat_ref[feat, :] = jnp.where(own_half, zq[:, pair], aug[:, feat]).T.astype(BF16)
            k_feat = jnp.where(own_half, zk[:, pair], aug[:, N_HEADS * LANES + h * LANES:N_HEADS * LANES + (h + 1) * LANES])
            ka_ref[:, feat] = k_feat.astype(BF16)

    u = seg(C_GLA, C_GLB) * _sigmoid(seg(C_GLB, C_GA))
    for c in range(N_SLABS):
        u_ref[c] = u[:, c * LANES:(c + 1) * LANES]
    ga_ref[...] = _sigmoid(seg(C_GA, C_GB)).astype(BF16)
    gb_ref[...] = _sigmoid(seg(C_GB, C_END)).astype(BF16)


def _inproj(x, mod, g1, w_all, b_all, tri, place, ones_row, *, nb, tm, sample, nseq):
    n = x.shape[0]
    nt = n // (nb * tm)
    row = lambda b, i: (b * nt + i, 0)
    const2 = lambda b, i: (0, 0)
    once = dict(pipeline_mode=pl.Buffered(1))
    in_specs = [pl.BlockSpec((tm, D_MODEL), row), _mod_spec(mod, 0), _mod_spec(mod, 1),
                pl.BlockSpec((1, D_MODEL), const2),
                pl.BlockSpec((C_END, D_MODEL), const2, **once),
                pl.BlockSpec((1, C_END), const2),
                pl.BlockSpec((tm, tm), const2, **once),
                pl.BlockSpec(place.shape, const2, **once),
                pl.BlockSpec(ones_row.shape, const2)]
    shared_shape = [jax.ShapeDtypeStruct((N_SLABS, n, LANES), F32),
                    jax.ShapeDtypeStruct((n, D_MODEL), BF16),
                    jax.ShapeDtypeStruct((n, D_MODEL), BF16)]
    shared_specs = [pl.BlockSpec((N_SLABS, tm, LANES), lambda b, i: (0, b * nt + i, 0)),
                    pl.BlockSpec((tm, D_MODEL), row), pl.BlockSpec((tm, D_MODEL), row)]
    if sample:
        out_shape = [jax.ShapeDtypeStruct((n, ATT_WIDTH), BF16),
                     jax.ShapeDtypeStruct((n, ATT_WIDTH), F32),
                     jax.ShapeDtypeStruct((n, ATT_WIDTH), F32),
                     jax.ShapeDtypeStruct((n, N_HEADS), F32),
                     jax.ShapeDtypeStruct((n, LANES), F32)]
        out_specs = [pl.BlockSpec((tm, ATT_WIDTH), row), pl.BlockSpec((tm, ATT_WIDTH), row),
                     pl.BlockSpec((tm, ATT_WIDTH), row), pl.BlockSpec((tm, N_HEADS), row),
                     pl.BlockSpec((tm, LANES), row)]
    else:
        n_pg, pg_tile = n // PAGE_SIZE, tm // PAGE_SIZE
        page = lambda b, i: (b * nt + i, 0, 0)
        n_feat = N_HEADS * LANES
        out_shape = [jax.ShapeDtypeStruct((nb, n_feat, n // nb), BF16),
                     jax.ShapeDtypeStruct((n, n_feat), BF16),
                     jax.ShapeDtypeStruct((n_pg, ATT_WIDTH, PAGE_SIZE), F32),
                     jax.ShapeDtypeStruct((n_pg, ATT_WIDTH, PAGE_SIZE), F32),
                     jax.ShapeDtypeStruct((nb, ATT_WIDTH, n // nb), BF16),
                     jax.ShapeDtypeStruct((n_pg, N_HEADS, PAGE_SIZE), F32)]
        out_specs = [pl.BlockSpec((None, n_feat, tm), lambda b, i: (b, 0, i)), pl.BlockSpec((tm, n_feat), row),
                     pl.BlockSpec((pg_tile, ATT_WIDTH, PAGE_SIZE), page),
                     pl.BlockSpec((pg_tile, ATT_WIDTH, PAGE_SIZE), page),
                     pl.BlockSpec((None, ATT_WIDTH, tm), lambda b, i: (b, 0, i)),
                     pl.BlockSpec((pg_tile, N_HEADS, PAGE_SIZE), page)]
    out_shape += shared_shape
    out_specs += shared_specs
    return pl.pallas_call(
        functools.partial(_inproj_kernel, sample=sample, nseq=nseq),
        out_shape=out_shape,
        grid=(nb, nt),
        in_specs=in_specs,
        out_specs=out_specs,
        scratch_shapes=[pltpu.VMEM((1, LANES), F32)],
        compiler_params=pltpu.CompilerParams(dimension_semantics=("arbitrary", "arbitrary"),
                                             vmem_limit_bytes=VMEM_LIMIT),
        name="inproj_sample" if sample else "inproj_prompt",
    )(x, mod, mod, g1, w_all, b_all, tri, place, ones_row)


def _attn_kernel(qat_ref, ka_ref, vt_ref, o_ref, s_even, s_odd, mx_even, mx_odd, *, tq, tk):
    qi = pl.program_id(2)
    s_bufs = (s_even, s_odd)
    mx_bufs = (mx_even, mx_odd)
    q_heads = [qat_ref[hh * LANES:(hh + 1) * LANES, :] for hh in range(2)]
    key = lax.broadcasted_iota(jnp.int32, (tk, ATTN_QCHUNK), 0)
    qry = lax.broadcasted_iota(jnp.int32, (tk, ATTN_QCHUNK), 1)

    units = [(hh, slice(c, c + ATTN_QCHUNK), hh * (tq // ATTN_QCHUNK) + c // ATTN_QCHUNK)
             for hh in range(2) for c in range(0, tq, ATTN_QCHUNK)]

    def scores(ki, parity, unit, diagonal):
        hh, cols, u = unit
        ka = ka_ref[pl.ds(pl.multiple_of(ki * tk, tk), tk), hh * LANES:(hh + 1) * LANES]
        st = jnp.dot(ka, q_heads[hh][:, cols], preferred_element_type=F32)
        if diagonal:
            st = jnp.where(key + parity * tk <= qry + cols.start, st, NEG_INF)
        s_bufs[parity][u] = st
        mx_bufs[parity][u] = jnp.max(st, axis=0, keepdims=True)

    ones_rows = jnp.ones((SUM_ROWS, tk), BF16)

    def softmax_pv(ki, parity, unit, carry):
        hh, _, u = unit
        m_prev, acc = carry
        m_new = jnp.maximum(m_prev, mx_bufs[parity][u])
        alpha = jnp.exp2(m_prev - m_new)
        pt = jnp.exp2(s_bufs[parity][u] - m_new).astype(BF16)
        vt = vt_ref[hh * HEAD_DIM:(hh + 1) * HEAD_DIM, pl.ds(pl.multiple_of(ki * tk, tk), tk)]
        vt = jnp.concatenate([vt, ones_rows], axis=0)
        return m_new, acc * alpha + jnp.dot(vt, pt, preferred_element_type=F32)

    def stage(k_next, p_next, k_cur, p_cur, carry, diagonal=False):
        out = []
        for unit, c in zip(units, carry):
            scores(k_next, p_next, unit, diagonal)
            out.append(softmax_pv(k_cur, p_cur, unit, c))
        return tuple(out)

    unit0 = (jnp.full((1, ATTN_QCHUNK), NEG_INF, F32), jnp.zeros((HEAD_DIM + SUM_ROWS, ATTN_QCHUNK), F32))
    for unit in units:
        scores(2 * qi, 0, unit, True)
    carry = stage(2 * qi + 1, 1, 2 * qi, 0, (unit0,) * len(units), diagonal=True)

    def pair(j, carry):
        carry = stage(2 * j, 0, jnp.where(j == 0, 2 * qi + 1, 2 * j - 1), 1, carry)
        return stage(2 * j + 1, 1, 2 * j, 0, carry)

    carry = lax.fori_loop(0, qi, pair, carry)
    last = jnp.where(qi == 0, 1, 2 * qi - 1)
    heads = [[], []]
    for unit, c in zip(units, carry):
        _, acc = softmax_pv(last, 1, unit, c)
        heads[unit[0]].append(acc[:HEAD_DIM] / acc[HEAD_DIM:HEAD_DIM + 1])
    o = jnp.concatenate([jnp.concatenate(h, axis=1) for h in heads], axis=0)
    o_ref[...] = o.T.astype(o_ref.dtype)


def _prompt_attention(qat, ka, vt, *, nb, seq, tq):
    tk = tq // 2
    ka3 = ka.reshape(nb, seq, N_HEADS * LANES)
    out = pl.pallas_call(
        functools.partial(_attn_kernel, tq=tq, tk=tk),
        out_shape=jax.ShapeDtypeStruct((nb, seq, ATT_WIDTH), BF16),
        grid=(nb, N_PAIRS, seq // tq),
        in_specs=[pl.BlockSpec((None, 2 * LANES, tq), lambda b, p, i: (b, p, i)),
                  pl.BlockSpec((None, seq, 2 * LANES), lambda b, p, i: (b, 0, p)),
                  pl.BlockSpec((None, LANES, seq), lambda b, p, i: (b, p, 0))],
        out_specs=pl.BlockSpec((None, tq, LANES), lambda b, p, i: (b, i, p)),
        scratch_shapes=([pltpu.VMEM((2 * tq // ATTN_QCHUNK, tk, ATTN_QCHUNK), F32)] * 2
                        + [pltpu.VMEM((2 * tq // ATTN_QCHUNK, 1, ATTN_QCHUNK), F32)] * 2),
        compiler_params=pltpu.CompilerParams(dimension_semantics=("arbitrary", "arbitrary", "arbitrary"),
                                             vmem_limit_bytes=VMEM_LIMIT),
        name="prompt_attention",
    )(qat, ka3, vt)
    return out.reshape(nb * seq, ATT_WIDTH)


class _SampleAttention:
    SCORE_ROWS = N_HEADS * SAMPLE_T
    NT_DIMS = (((1,), (1,)), ((), ()))

    def __init__(self, q_scr, m_scr, l_scr, acc_scr, carry_scr):
        self.q_scr, self.m_scr, self.l_scr, self.acc_scr, self.carry_scr = q_scr, m_scr, l_scr, acc_scr, carry_scr

    @staticmethod
    def scratch_shapes():
        rows = _SampleAttention.SCORE_ROWS
        return [pltpu.VMEM((rows, ATT_WIDTH), BF16), pltpu.VMEM((rows, 1), F32), pltpu.VMEM((rows, 1), F32),
                pltpu.VMEM((rows, ATT_WIDTH), F32), pltpu.VMEM((N_HEADS, LANES), F32)]

    @staticmethod
    def per_head_rows(a):
        return jnp.concatenate([jnp.broadcast_to(a[h:h + 1, :], (SAMPLE_T, a.shape[1])) for h in range(N_HEADS)],
                               axis=0)

    def update(self, s, pv):
        m_prev = self.m_scr[...]
        m_new = jnp.maximum(m_prev, jnp.max(s, axis=-1, keepdims=True))
        alpha = jnp.exp(m_prev - m_new)
        p = jnp.exp(s - m_new)
        self.l_scr[...] = alpha * self.l_scr[...] + jnp.sum(p, axis=-1, keepdims=True)
        self.acc_scr[...] = alpha * self.acc_scr[...] + pv(p.astype(BF16))
        self.m_scr[...] = m_new

    def start(self, q_ref, kn_ref, vn_ref, cn_ref):
        rows, q_scr, nt_dims, per_head_rows = self.SCORE_ROWS, self.q_scr, self.NT_DIMS, self.per_head_rows
        m_scr, l_scr, acc_scr, carry_scr, update = self.m_scr, self.l_scr, self.acc_scr, self.carry_scr, self.update
        q8 = q_ref[...].astype(F32)
        qt = jnp.concatenate([q8] * N_HEADS, axis=0)
        r_i = lax.broadcasted_iota(jnp.int32, (rows, ATT_WIDTH), 0)
        c_i = lax.broadcasted_iota(jnp.int32, (rows, ATT_WIDTH), 1)
        q_scr[...] = jnp.where(r_i // SAMPLE_T == c_i // HEAD_DIM, qt, 0.0).astype(BF16)
        m_scr[...] = jnp.full_like(m_scr, NEG_INF)
        l_scr[...] = jnp.zeros_like(l_scr)
        acc_scr[...] = jnp.zeros_like(acc_scr)
        carry_scr[...] = jnp.zeros_like(carry_scr)
        pad = jnp.zeros((PAGE_SIZE - SAMPLE_T, ATT_WIDTH), F32)
        k_new = jnp.concatenate([kn_ref[...], pad], axis=0).astype(BF16)
        v_new = jnp.concatenate([vn_ref[...], pad], axis=0).astype(BF16)
        c_new = jnp.concatenate([cn_ref[...], jnp.zeros((PAGE_SIZE - SAMPLE_T, LANES), F32)], axis=0)
        s = lax.dot_general(q_scr[...], k_new, nt_dims, preferred_element_type=F32)
        s = s - per_head_rows(c_new.T[:N_HEADS, :])
        r_s = lax.broadcasted_iota(jnp.int32, (rows, PAGE_SIZE), 0)
        c_s = lax.broadcasted_iota(jnp.int32, (rows, PAGE_SIZE), 1)
        update(jnp.where(c_s <= r_s % SAMPLE_T, s, NEG_INF),
               lambda p: jnp.dot(p, v_new, preferred_element_type=F32))

    def page_scores(self, k_pages, lf_pages, tri_ref):
        pieces = []
        for lf in lf_pages:
            hi = lf.astype(BF16).astype(F32)
            mid = (lf - hi).astype(BF16).astype(F32)
            lo = (lf - hi - mid).astype(BF16).astype(F32)
            pieces += [hi, mid, lo]
        sums = jnp.dot(jnp.concatenate(pieces, axis=0).astype(BF16), tri_ref[...], preferred_element_type=F32)
        carry = self.carry_scr[...]
        bias = []
        for i in range(len(lf_pages)):
            part = [sums[(3 * i + j) * N_HEADS:(3 * i + j + 1) * N_HEADS, :] for j in range(AUG)]
            local = part[0] + part[1] + part[2]
            bias.append(local[:, :PAGE_SIZE] + carry)
            carry = carry + local[:, PAGE_SIZE:]
        self.carry_scr[...] = carry
        kt = jnp.concatenate([k.astype(BF16) for k in k_pages], axis=1)
        s = jnp.dot(self.q_scr[...], kt, preferred_element_type=F32)
        return s + self.per_head_rows(jnp.concatenate(bias, axis=1))

    def absorb_pages(self, s, v_pages):
        vt = jnp.concatenate([v.astype(BF16) for v in v_pages], axis=1)
        self.update(s, lambda p: lax.dot_general(p, vt, self.NT_DIMS, preferred_element_type=F32))

    def finish(self, o_ref):
        rows = self.SCORE_ROWS
        o = self.acc_scr[...] / self.l_scr[...]
        r_i = lax.broadcasted_iota(jnp.int32, (rows, ATT_WIDTH), 0)
        c_i = lax.broadcasted_iota(jnp.int32, (rows, ATT_WIDTH), 1)
        o = jnp.where(r_i // SAMPLE_T == c_i // HEAD_DIM, o, 0.0).astype(BF16)
        t_i = lax.broadcasted_iota(jnp.int32, (SAMPLE_T, rows), 0)
        r_j = lax.broadcasted_iota(jnp.int32, (SAMPLE_T, rows), 1)
        sel = jnp.where(r_j % SAMPLE_T == t_i, 1.0, 0.0).astype(BF16)
        o_ref[...] = jnp.dot(sel, o, preferred_element_type=F32).astype(o_ref.dtype)


def _merge_tail(rows, conv, g1, x_ref, o_ref, ga_ref, gb_ref, dwb_ref, lng_ref, lnb_ref, wpa_ref, wpb_ref, bpb_ref,
                wo_ref, out_ref, ya=None):
    yb = conv + dwb_ref[...]
    mu = jnp.mean(yb, axis=-1, keepdims=True)
    var = jnp.mean(jnp.square(yb - mu), axis=-1, keepdims=True)
    yb = (yb - mu) * lax.rsqrt(var + EPS) * lng_ref[...] + lnb_ref[...]
    yb = jnp.dot(_silu(yb).astype(BF16), wpb_ref[...], preferred_element_type=F32) + bpb_ref[...]
    if ya is None:
        ya = jnp.dot(o_ref[rows, :], wpa_ref[...], preferred_element_type=F32)
    m = ga_ref[rows, :].astype(F32) * ya + gb_ref[rows, :].astype(F32) * yb
    out_ref[rows, :] = x_ref[rows, :] + g1 * jnp.dot(m.astype(BF16), wo_ref[...], preferred_element_type=F32)


def _merge_prompt_kernel(x_ref, o_ref, ucur_ref, uprev_ref, ga_ref, gb_ref, g1_ref, dww_ref, dwb_ref, lng_ref,
                         lnb_ref, wpa_ref, wpb_ref, bpb_ref, wo_ref, out_ref, ubuf, *, nseq, row_block):
    tm = x_ref.shape[0]
    first = pl.program_id(1) == 0
    prev = uprev_ref[...]
    ubuf[:, 0:CONV_HALO, :] = jnp.where(first, jnp.zeros_like(prev), prev)
    ubuf[:, CONV_HALO:, :] = ucur_ref[...]
    g1 = _mod_rows(g1_ref, nseq, False)
    chunk = 64
    ya_all = jnp.dot(o_ref[...], wpa_ref[...], preferred_element_type=F32)
    for b0 in range(0, tm, row_block):
        slabs = []
        for c in range(N_SLABS):
            pieces = []
            for r0 in range(b0, b0 + row_block, chunk):
                acc = jnp.zeros((chunk, LANES), F32)
                for j in range(CONV_K):
                    off = r0 + CONV_HALO - (CONV_K - 1) + j
                    acc = acc + dww_ref[c, j:j + 1, :] * ubuf[c, off:off + chunk, :]
                pieces.append(acc)
            slabs.append(jnp.concatenate(pieces, axis=0))
        _merge_tail(slice(b0, b0 + row_block), jnp.concatenate(slabs, axis=1), g1, x_ref, o_ref, ga_ref, gb_ref,
                    dwb_ref, lng_ref, lnb_ref, wpa_ref, wpb_ref, bpb_ref, wo_ref, out_ref,
                    ya=ya_all[b0:b0 + row_block])


def _merge_sample_kernel(x_ref, o_ref, uwin_ref, ga_ref, gb_ref, g1_ref, dww_ref, dwb_ref, lng_ref,
                         lnb_ref, wpa_ref, wpb_ref, bpb_ref, wo_ref, out_ref, *, nseq):
    slabs = []
    for c in range(N_SLABS):
        acc = jnp.zeros((nseq, SAMPLE_T, LANES), F32)
        for j in range(CONV_K):
            acc = acc + dww_ref[c, j:j + 1, :] * uwin_ref[c, :, j:j + SAMPLE_T, :]
        slabs.append(acc.reshape(nseq * SAMPLE_T, LANES))
    _merge_tail(slice(None), jnp.concatenate(slabs, axis=1), _mod_rows(g1_ref, nseq, True), x_ref, o_ref, ga_ref,
                gb_ref, dwb_ref, lng_ref, lnb_ref, wpa_ref, wpb_ref, bpb_ref, wo_ref, out_ref)


def _merge(x, o, u_args, ga, gb, mod, weights, *, nb, tm, sample, nseq):
    n = x.shape[0]
    nt = n // (nb * tm)
    row = lambda b, i: (b * nt + i, 0)
    const2 = lambda b, i: (0, 0)
    const3 = lambda b, i: (0, 0, 0)
    once = dict(pipeline_mode=pl.Buffered(1))
    mod_spec = _mod_spec(mod, 2)
    if sample:
        (uwin,) = u_args
        u_specs = [pl.BlockSpec(uwin.shape, lambda b, i: (0, 0, 0, 0))]
        kernel, scratch = functools.partial(_merge_sample_kernel, nseq=nseq), []
    else:
        (u,) = u_args
        u_args = (u, u)
        blocks_per_tile = tm // CONV_HALO
        u_specs = [pl.BlockSpec((N_SLABS, tm, LANES), lambda b, i: (0, b * nt + i, 0)),
                   pl.BlockSpec((N_SLABS, CONV_HALO, LANES),
                                lambda b, i: (0, jnp.maximum((b * nt + i) * blocks_per_tile - 1, 0), 0))]
        kernel = functools.partial(_merge_prompt_kernel, nseq=nseq, row_block=tm // 2)
        scratch = [pltpu.VMEM((N_SLABS, CONV_HALO + tm, LANES), F32)]
    dww, dwb, lng, lnb, wpa, wpb, bpb, wo = weights
    in_specs = ([pl.BlockSpec((tm, D_MODEL), row), pl.BlockSpec((tm, ATT_WIDTH), row)] + u_specs
                + [pl.BlockSpec((tm, D_MODEL), row), pl.BlockSpec((tm, D_MODEL), row), mod_spec,
                   pl.BlockSpec(dww.shape, const3),
                   pl.BlockSpec((1, CONV_WIDTH), const2), pl.BlockSpec((1, CONV_WIDTH), const2),
                   pl.BlockSpec((1, CONV_WIDTH), const2),
                   pl.BlockSpec((ATT_WIDTH, D_MODEL), const2, **once),
                   pl.BlockSpec((CONV_WIDTH, D_MODEL), const2, **once),
                   pl.BlockSpec((1, D_MODEL), const2),
                   pl.BlockSpec((D_MODEL, D_MODEL), const2, **once)])
    return pl.pallas_call(
        kernel,
        out_shape=jax.ShapeDtypeStruct((n, D_MODEL), F32),
        grid=(nb, nt),
        in_specs=in_specs,
        out_specs=pl.BlockSpec((tm, D_MODEL), row),
        scratch_shapes=scratch,
        compiler_params=pltpu.CompilerParams(dimension_semantics=("arbitrary", "arbitrary"),
                                             vmem_limit_bytes=VMEM_LIMIT),
        name="merge_sample" if sample else "merge_prompt",
    )(x, o, *u_args, ga, gb, mod, dww, dwb, lng, lnb, wpa, wpb, bpb, wo)


def _ffn_steps(x_ref, sh_ref, sc_ref, g2_ref, rg_ref, fg_ref, win_ref, wout_ref, out_ref, *, bounds, sample, nseq):
    x = x_ref[...]
    ms = jnp.mean(x * x, axis=-1, keepdims=True)
    h = x * lax.rsqrt(ms + EPS) * rg_ref[...]
    hb = (h * (1.0 + _mod_rows(sc_ref, nseq, sample)) + _mod_rows(sh_ref, nseq, sample)).astype(BF16)
    acc = jnp.zeros(x.shape, F32)
    for lo, hi in zip(bounds[:-1], bounds[1:]):
        gate = jnp.dot(hb, win_ref[:, lo:hi], preferred_element_type=F32)
        up = jnp.dot(hb, win_ref[:, FFN_HIDDEN + lo:FFN_HIDDEN + hi], preferred_element_type=F32)
        yield
        act = (_silu(gate) * up).astype(BF16)
        acc = acc + jnp.dot(act, wout_ref[lo:hi, :], preferred_element_type=F32)
        if hi == bounds[-1]:
            x2 = x + _mod_rows(g2_ref, nseq, sample) * acc
            ms2 = jnp.mean(x2 * x2, axis=-1, keepdims=True)
            out_ref[...] = x2 * lax.rsqrt(ms2 + EPS) * fg_ref[...]
        yield


def _hidden_bounds(n_chunks):
    tiles = FFN_HIDDEN // MXU_TILE
    assert tiles * MXU_TILE == FFN_HIDDEN
    return [MXU_TILE * ((tiles * k) // n_chunks) for k in range(n_chunks + 1)]


def _ffn_kernel(*refs, sample, nseq):
    for _ in _ffn_steps(*refs, bounds=_hidden_bounds(2), sample=sample, nseq=nseq):
        pass


def _ffn_attn_kernel(pt_ref, x_ref, sh_ref, sc_ref, g2_ref, rg_ref, fg_ref, win_ref, wout_ref,
                     q_ref, kn_ref, vn_ref, cn_ref, tri_ref, ck_hbm, cv_hbm, clf_hbm,
                     out_ref, o_ref, kbuf, vbuf, lbuf, sem, *state, nseq, n_pages):
    npg = PAGES_PER_STEP
    n_chunks = n_pages // npg
    seq = pl.program_id(0) * pl.num_programs(1) + pl.program_id(1)
    attn = _SampleAttention(*state)

    def copies(sequence, c):
        slot, out = c % 2, []
        for j in range(npg):
            page = pt_ref[sequence * n_pages + (n_pages - 1) - (c * npg + j)]
            out += [pltpu.make_async_copy(ck_hbm.at[page], kbuf.at[slot, j], sem.at[0, slot]),
                    pltpu.make_async_copy(cv_hbm.at[page], vbuf.at[slot, j], sem.at[1, slot]),
                    pltpu.make_async_copy(clf_hbm.at[page], lbuf.at[slot, j], sem.at[2, slot])]
        return out

    def start(sequence, c):
        for cp in copies(sequence, c):
            cp.start()

    @pl.when(seq == 0)
    def _():
        start(seq, 0)

    ffn = _ffn_steps(x_ref, sh_ref, sc_ref, g2_ref, rg_ref, fg_ref, win_ref, wout_ref, out_ref,
                     bounds=_hidden_bounds(n_chunks), sample=False, nseq=nseq)
    for c in range(n_chunks):
        for cp in copies(seq, c):
            cp.wait()
        if c + 1 < n_chunks:
            start(seq, c + 1)
        else:
            @pl.when(seq + 1 < nseq)
            def _():
                start(seq + 1, 0)
        if c == 0:
            attn.start(q_ref, kn_ref, vn_ref, cn_ref)
        slot = c % 2
        next(ffn)
        s = attn.page_scores([kbuf[slot, j] for j in range(npg)], [lbuf[slot, j] for j in range(npg)], tri_ref)
        next(ffn)
        attn.absorb_pages(s, [vbuf[slot, j] for j in range(npg)])
    attn.finish(o_ref)


def _ffn(x, mod, rms_g, final_g, w_in, w_out, *, nb, tm, sample, nseq):
    n = x.shape[0]
    nt = n // (nb * tm)
    row = lambda b, i: (b * nt + i, 0)
    const2 = lambda b, i: (0, 0)
    once = dict(pipeline_mode=pl.Buffered(1))
    return pl.pallas_call(
        functools.partial(_ffn_kernel, sample=sample, nseq=nseq),
        out_shape=jax.ShapeDtypeStruct((n, D_MODEL), F32),
        grid=(nb, nt),
        in_specs=[pl.BlockSpec((tm, D_MODEL), row), _mod_spec(mod, 3), _mod_spec(mod, 4), _mod_spec(mod, 5),
                  pl.BlockSpec((1, D_MODEL), const2), pl.BlockSpec((1, D_MODEL), const2),
                  pl.BlockSpec((D_MODEL, 2 * FFN_HIDDEN), const2, **once),
                  pl.BlockSpec((FFN_HIDDEN, D_MODEL), const2, **once)],
        out_specs=pl.BlockSpec((tm, D_MODEL), row),
        compiler_params=pltpu.CompilerParams(dimension_semantics=("arbitrary", "arbitrary"),
                                             vmem_limit_bytes=VMEM_LIMIT),
        name="ffn_sample" if sample else "ffn_prompt",
    )(x, mod, mod, mod, rms_g, final_g, w_in, w_out)


def _ffn_with_sample_attention(x, mod, rms_g, final_g, w_in, w_out, page_table, q, k_new, v_new, c_new, tri_page,
                               cache_k, cache_v, cache_logf, *, nb, tm):
    n = x.shape[0]
    nt = n // (nb * tm)
    nseq, n_pages = page_table.shape
    assert nseq == nb * nt and n_pages % PAGES_PER_STEP == 0
    row = lambda b, i, pt: (b * nt + i, 0)
    const2 = lambda b, i, pt: (0, 0)
    once = dict(pipeline_mode=pl.Buffered(1))
    mod_spec = lambda kind: pl.BlockSpec((None, mod.shape[1], D_MODEL), lambda b, i, pt: (kind, 0, 0))
    seq_spec = lambda width: pl.BlockSpec((None, SAMPLE_T, width), lambda b, i, pt: (b * nt + i, 0, 0))
    hbm = pl.BlockSpec(memory_space=pl.ANY)
    slots = 2
    grid_spec = pltpu.PrefetchScalarGridSpec(
        num_scalar_prefetch=1,
        grid=(nb, nt),
        in_specs=[pl.BlockSpec((tm, D_MODEL), row), mod_spec(3), mod_spec(4), mod_spec(5),
                  pl.BlockSpec((1, D_MODEL), const2), pl.BlockSpec((1, D_MODEL), const2),
                  pl.BlockSpec((D_MODEL, 2 * FFN_HIDDEN), const2, **once),
                  pl.BlockSpec((FFN_HIDDEN, D_MODEL), const2, **once),
                  seq_spec(ATT_WIDTH), seq_spec(ATT_WIDTH), seq_spec(ATT_WIDTH), seq_spec(LANES),
                  pl.BlockSpec((PAGE_SIZE, 2 * PAGE_SIZE), const2), hbm, hbm, hbm],
        out_specs=[pl.BlockSpec((tm, D_MODEL), row), seq_spec(ATT_WIDTH)],
        scratch_shapes=[pltpu.VMEM((slots, PAGES_PER_STEP, ATT_WIDTH, PAGE_SIZE), F32),
                        pltpu.VMEM((slots, PAGES_PER_STEP, ATT_WIDTH, PAGE_SIZE), F32),
                        pltpu.VMEM((slots, PAGES_PER_STEP, N_HEADS, PAGE_SIZE), F32),
                        pltpu.SemaphoreType.DMA((3, slots))] + _SampleAttention.scratch_shapes())
    return pl.pallas_call(
        functools.partial(_ffn_attn_kernel, nseq=nseq, n_pages=n_pages),
        out_shape=[jax.ShapeDtypeStruct((n, D_MODEL), F32),
                   jax.ShapeDtypeStruct((nseq, SAMPLE_T, ATT_WIDTH), BF16)],
        grid_spec=grid_spec,
        compiler_params=pltpu.CompilerParams(dimension_semantics=("arbitrary", "arbitrary"),
                                             vmem_limit_bytes=VMEM_LIMIT),
        name="ffn_prompt_sample_attention",
    )(page_table.reshape(-1), x, mod, mod, mod, rms_g, final_g, w_in, w_out, q, k_new, v_new, c_new, tri_page,
      cache_k, cache_v, cache_logf)


def _bias_placement():
    half = N_HEADS * LANES
    place = np.zeros((LANES, 2 * half), np.float32)
    ones = np.zeros((1, 2 * half), np.float32)
    for h in range(N_HEADS):
        base = h * LANES + (HEAD_DIM if h % 2 == 0 else 0)
        for piece in range(AUG):
            place[piece * N_HEADS + h, base + piece] = 1.0
            ones[0, base + AUG + piece] = 1.0
            ones[0, half + base + piece] = 1.0
            place[piece * N_HEADS + h, half + base + AUG + piece] = -1.0
    return jnp.asarray(place, BF16), jnp.asarray(ones, F32)


def _lower_tri(n, block):
    t = np.arange(n)[:, None]
    s = np.arange(n)[None, :]
    return jnp.asarray(((s <= t) & (t // block == s // block)).astype(np.float32), BF16)


def _later_keys(n):
    j = np.arange(n)[:, None]
    s = np.arange(n)[None, :]
    return jnp.asarray(np.concatenate([(j > s).astype(np.float32), np.ones((n, n), np.float32)], axis=1), BF16)


def kernel(x_prompt, x_sample, c_prompt, c_sample, cache_k, cache_v, cache_logf, state_conv, page_table, rms1_g, rms2_g, w_ada, b_ada, w_in, b_in, dw_w, dw_b, ln_g, ln_b, w_pa, w_pb, b_pb, w_o, w_ffn_in, w_ffn_out, final_g):
    nb, seq, _ = x_prompt.shape
    nseq, dec_t, _ = x_sample.shape
    depth = w_in.shape[0]
    assert depth == 1 and dec_t <= SAMPLE_T
    n_prompt = nb * seq
    tm = 512
    n_sample = nseq * SAMPLE_T

    wt, b = w_in[0].T, b_in[0]
    g_off = 3 * ATT_WIDTH + N_HEADS
    w_all = jnp.concatenate([wt[:g_off], jnp.zeros((LANES - N_HEADS, D_MODEL), F32), wt[g_off:]], axis=0).astype(BF16)
    b_all = jnp.concatenate([b[:g_off], jnp.zeros((LANES - N_HEADS,), F32), b[g_off:]])[None, :]
    dww = jnp.pad(dw_w[0], ((0, CONV_HALO - CONV_K), (0, 0))).reshape(CONV_HALO, N_SLABS, LANES).transpose(1, 0, 2)
    merge_w = (dww, dw_b[0][None, :], ln_g[0][None, :], ln_b[0][None, :], w_pa[0].astype(BF16),
               w_pb[0].astype(BF16), b_pb[0][None, :], w_o[0].astype(BF16))
    wf_in, wf_out = w_ffn_in[0].astype(BF16), w_ffn_out[0].astype(BF16)
    g1w, g2w, gfw = rms1_g[0][None, :], rms2_g[0][None, :], final_g[None, :]
    place, ones_row = _bias_placement()

    n_cond = nb + nseq
    c_all = jnp.pad(jnp.concatenate([c_sample, c_prompt], axis=0), ((0, -n_cond % 8), (0, 0)))
    mod = _modulation(c_all, w_ada[0], b_ada[0][None, :])

    xs = jnp.pad(x_sample, ((0, 0), (0, SAMPLE_T - dec_t), (0, 0))).reshape(n_sample, D_MODEL)
    (q_s, k_s, v_s, lf_s, c_s, u_s, ga_s, gb_s) = _inproj(
        xs, mod, g1w, w_all, b_all, _lower_tri(n_sample, SAMPLE_T), place, ones_row,
        nb=1, tm=n_sample, sample=True, nseq=nseq)

    xp = x_prompt.reshape(n_prompt, D_MODEL)
    (qa, ka, kt_p, vt_p, vb_p, lft_p, u_p, ga_p, gb_p) = _inproj(
        xp, mod, g1w, w_all, b_all, _lower_tri(tm, tm), place, ones_row, nb=nb, tm=tm, sample=False, nseq=nseq)
    o_p = _prompt_attention(qa, ka, vb_p, nb=nb, seq=seq, tq=2 * tm)
    x1_p = _merge(xp, o_p, (u_p,), ga_p, gb_p, mod, merge_w, nb=nb, tm=tm, sample=False, nseq=nseq)
    n_phys = cache_k.shape[1]
    page_t = lambda c: c[0].transpose(0, 2, 3, 1).reshape(n_phys, ATT_WIDTH, PAGE_SIZE)
    y_p, o_s = _ffn_with_sample_attention(
        x1_p, mod, g2w, gfw, wf_in, wf_out,
        page_table, q_s.reshape(nseq, SAMPLE_T, ATT_WIDTH), k_s.reshape(nseq, SAMPLE_T, ATT_WIDTH),
        v_s.reshape(nseq, SAMPLE_T, ATT_WIDTH), c_s.reshape(nseq, SAMPLE_T, LANES), _later_keys(PAGE_SIZE),
        page_t(cache_k), page_t(cache_v), cache_logf[0].transpose(0, 2, 1), nb=nb, tm=tm)

    state_slabs = state_conv[0].reshape(nseq, CONV_K - 1, N_SLABS, LANES).transpose(2, 0, 1, 3)
    u_slabs = u_s.reshape(N_SLABS, nseq, SAMPLE_T, LANES)
    uwin = jnp.concatenate(
        [state_slabs, u_slabs, jnp.zeros((N_SLABS, nseq, SAMPLE_WIN - (CONV_K - 1) - SAMPLE_T, LANES), F32)], axis=2)
    x1_s = _merge(xs, o_s.reshape(n_sample, ATT_WIDTH), (uwin,), ga_s, gb_s, mod, merge_w,
                  nb=1, tm=n_sample, sample=True, nseq=nseq)
    y_s = _ffn(x1_s, mod, g2w, gfw, wf_in, wf_out, nb=1, tm=n_sample, sample=True, nseq=nseq)

    n_pg = seq // PAGE_SIZE
    tail = CONV_K - 1
    u_tail = u_p.reshape(N_SLABS, nb, seq, LANES)[:, :, seq - tail:]
    u_tail = u_tail.transpose(1, 2, 0, 3).reshape(nb, tail, CONV_WIDTH)
    us_rows = u_s.reshape(N_SLABS, nseq, SAMPLE_T, LANES)[:, :, :dec_t]
    us_rows = us_rows.transpose(1, 2, 0, 3).reshape(nseq, dec_t, CONV_WIDTH)
    unpad = lambda a, width: a.reshape(nseq, SAMPLE_T, *width)[:, :dec_t]
    return (y_p.reshape(nb, seq, D_MODEL),
            unpad(y_s, (D_MODEL,)),
            kt_p.reshape(1, nb, n_pg, N_HEADS, HEAD_DIM, PAGE_SIZE).transpose(0, 1, 2, 5, 3, 4),
            vt_p.reshape(1, nb, n_pg, N_HEADS, HEAD_DIM, PAGE_SIZE).transpose(0, 1, 2, 5, 3, 4),
            lft_p.reshape(1, nb, n_pg, N_HEADS, PAGE_SIZE).transpose(0, 1, 2, 4, 3),
            u_tail[None],
            unpad(k_s, (N_HEADS, HEAD_DIM))[None],
            unpad(v_s, (N_HEADS, HEAD_DIM))[None],
            unpad(lf_s, (N_HEADS,))[None],
            jnp.concatenate([state_conv[0][:, dec_t:], us_rows], axis=1)[None])
```

```python
import functools

import numpy as np
import jax
import jax.numpy as jnp
from jax import lax
from jax.experimental import pallas as pl
from jax.experimental.pallas import tpu as pltpu

F32 = jnp.float32
BF16 = jnp.bfloat16

D_MODEL = 1024
N_HEADS = 8
HEAD_DIM = 64
ATT_WIDTH = N_HEADS * HEAD_DIM
CONV_WIDTH = 512
CONV_K = 31
FFN_HIDDEN = 2816
PAGE_SIZE = 128
EPS = 1e-6
NEG_INF = -1e30
SCALE = HEAD_DIM ** -0.5
LOG2E = 1.4426950408889634

LANES = 128
MXU_TILE = 256
N_PAIRS = N_HEADS // 2
N_SLABS = CONV_WIDTH // LANES
AUG = 3
SUM_ROWS = 16
ATTN_QCHUNK = 256
SAMPLE_T = 8
CONV_HALO = 32
SAMPLE_WIN = 40
PAGES_PER_STEP = 16
VMEM_LIMIT = 56 * 1024 * 1024

C_Q, C_K, C_V, C_F, C_GLA, C_GLB, C_GA, C_GB, C_END = 0, 512, 1024, 1536, 1664, 2176, 2688, 3712, 4736


def _sigmoid(x):
    return 1.0 / (1.0 + jnp.exp(-x))


def _silu(x):
    return x * _sigmoid(x)


def _split3_packed(a, lane):
    a = jnp.where(lane < N_HEADS, a, 0.0)
    hi = a.astype(BF16).astype(F32)
    r1 = a - hi
    mid = r1.astype(BF16).astype(F32)
    lo = (r1 - mid).astype(BF16).astype(F32)
    packed = hi + pltpu.roll(mid, N_HEADS, axis=1) + pltpu.roll(lo, 2 * N_HEADS, axis=1)
    return packed.astype(BF16)


def _unpack3(p, lane):
    s = p + pltpu.roll(p, LANES - N_HEADS, axis=1) + pltpu.roll(p, LANES - 2 * N_HEADS, axis=1)
    return jnp.where(lane < N_HEADS, s, 0.0)


def _mod_kernel(c_ref, w_ref, b_ref, o_ref):
    s = _silu(c_ref[...]).astype(BF16)
    o_ref[...] = jnp.dot(s, w_ref[...].astype(BF16), preferred_element_type=F32) + b_ref[...]


def _modulation(c_all, w_ada, b_ada):
    rows = c_all.shape[0]
    n_mod = w_ada.shape[1] // D_MODEL
    return pl.pallas_call(
        _mod_kernel,
        out_shape=jax.ShapeDtypeStruct((n_mod, rows, D_MODEL), F32),
        grid=(n_mod,),
        in_specs=[pl.BlockSpec((rows, D_MODEL), lambda j: (0, 0)),
                  pl.BlockSpec((D_MODEL, D_MODEL), lambda j: (0, j)),
                  pl.BlockSpec((1, D_MODEL), lambda j: (0, j))],
        out_specs=pl.BlockSpec((None, rows, D_MODEL), lambda j: (j, 0, 0)),
        compiler_params=pltpu.CompilerParams(dimension_semantics=("arbitrary",), vmem_limit_bytes=VMEM_LIMIT),
        name="modulation",
    )(c_all, w_ada, b_ada)


def _mod_rows(ref, nseq, sample):
    if sample:
        m = ref[0:nseq, :]
        return jnp.broadcast_to(m[:, None, :], (nseq, SAMPLE_T, D_MODEL)).reshape(nseq * SAMPLE_T, D_MODEL)
    return ref[pl.ds(nseq + pl.program_id(0), 1), :]


def _mod_spec(mod, kind):
    return pl.BlockSpec((None, mod.shape[1], D_MODEL), lambda b, i: (kind, 0, 0))


def _inproj_kernel(x_ref, sh_ref, sc_ref, g_ref, w_ref, b_ref, tri_ref, place_ref, ones_ref, *refs, sample, nseq):
    if sample:
        qp_ref, k_ref, v_ref, lf_ref, c_ref, u_ref, ga_ref, gb_ref, carry_ref = refs
    else:
        qat_ref, ka_ref, kt_ref, vt_ref, vb_ref, lft_ref, u_ref, ga_ref, gb_ref, carry_ref = refs
    tm = x_ref.shape[0]

    @pl.when(pl.program_id(1) == 0)
    def _():
        carry_ref[...] = jnp.zeros_like(carry_ref)

    x = x_ref[...]
    ms = jnp.mean(x * x, axis=-1, keepdims=True)
    h = x * lax.rsqrt(ms + EPS) * g_ref[...]
    h = h * (1.0 + _mod_rows(sc_ref, nseq, sample)) + _mod_rows(sh_ref, nseq, sample)
    hb = h.astype(BF16)

    def seg(lo, hi):
        z = lax.dot_general(hb, w_ref[lo:hi, :], (((1,), (1,)), ((), ())), preferred_element_type=F32)
        return z + b_ref[:, lo:hi]

    lane = lax.broadcasted_iota(jnp.int32, (tm, LANES), 1)
    zf = seg(C_F, C_GLA)
    lf = jnp.minimum(zf, 0.0) - jnp.log1p(jnp.exp(-jnp.abs(zf)))
    lf = jnp.where(lane < N_HEADS, lf, 0.0)
    csum = _unpack3(jnp.dot(tri_ref[...], _split3_packed(lf, lane), preferred_element_type=F32), lane)
    csum = csum + carry_ref[...]
    carry_ref[...] = csum[tm - 1:tm, :]

    zq = seg(C_Q, C_K)
    zk = seg(C_K, C_V)
    zv = seg(C_V, C_F)
    if sample:
        qp_ref[...] = (zq * SCALE).astype(BF16)
        k_ref[...] = zk
        v_ref[...] = zv
        lf_ref[...] = lf[:, :N_HEADS]
        c_ref[...] = csum
    else:
        for pg in range(tm // PAGE_SIZE):
            rows = slice(pg * PAGE_SIZE, (pg + 1) * PAGE_SIZE)
            kt_ref[pg] = zk[rows, :].T
            vt_page = zv[rows, :].T
            vt_ref[pg] = vt_page
            vb_ref[:, rows] = vt_page.astype(BF16)
            lft_ref[pg] = lf[rows, :].T[:N_HEADS, :]
        zq = zq * (SCALE * LOG2E)
        aug = jnp.dot(_split3_packed(csum * LOG2E, lane), place_ref[...], preferred_element_type=F32) + ones_ref[...]
        for h in range(N_HEADS):
            own_half = (lane < HEAD_DIM) if h % 2 == 0 else (lane >= HEAD_DIM)
            pair, feat = slice((h // 2) * LANES, (h // 2 + 1) * LANES), slice(h * LANES, (h + 1) * LANES)
            qat_ref[feat, :] = jnp.where(own_half, zq[:, pair], aug[:, feat]).T.astype(BF16)
            k_feat = jnp.where(own_half, zk[:, pair], aug[:, N_HEADS * LANES + h * LANES:N_HEADS * LANES + (h + 1) * LANES])
            ka_ref[:, feat] = k_feat.astype(BF16)

    u = seg(C_GLA, C_GLB) * _sigmoid(seg(C_GLB, C_GA))
    for c in range(N_SLABS):
        u_ref[c] = u[:, c * LANES:(c + 1) * LANES]
    ga_ref[...] = _sigmoid(seg(C_GA, C_GB)).astype(BF16)
    gb_ref[...] = _sigmoid(seg(C_GB, C_END)).astype(BF16)


def _inproj(x, mod, g1, w_all, b_all, tri, place, ones_row, *, nb, tm, sample, nseq):
    n = x.shape[0]
    nt = n // (nb * tm)
    row = lambda b, i: (b * nt + i, 0)
    const2 = lambda b, i: (0, 0)
    once = dict(pipeline_mode=pl.Buffered(1))
    in_specs = [pl.BlockSpec((tm, D_MODEL), row), _mod_spec(mod, 0), _mod_spec(mod, 1),
                pl.BlockSpec((1, D_MODEL), const2),
                pl.BlockSpec((C_END, D_MODEL), const2, **once),
                pl.BlockSpec((1, C_END), const2),
                pl.BlockSpec((tm, tm), const2, **once),
                pl.BlockSpec(place.shape, const2, **once),
                pl.BlockSpec(ones_row.shape, const2)]
    shared_shape = [jax.ShapeDtypeStruct((N_SLABS, n, LANES), F32),
                    jax.ShapeDtypeStruct((n, D_MODEL), BF16),
                    jax.ShapeDtypeStruct((n, D_MODEL), BF16)]
    shared_specs = [pl.BlockSpec((N_SLABS, tm, LANES), lambda b, i: (0, b * nt + i, 0)),
                    pl.BlockSpec((tm, D_MODEL), row), pl.BlockSpec((tm, D_MODEL), row)]
    if sample:
        out_shape = [jax.ShapeDtypeStruct((n, ATT_WIDTH), BF16),
                     jax.ShapeDtypeStruct((n, ATT_WIDTH), F32),
                     jax.ShapeDtypeStruct((n, ATT_WIDTH), F32),
                     jax.ShapeDtypeStruct((n, N_HEADS), F32),
                     jax.ShapeDtypeStruct((n, LANES), F32)]
        out_specs = [pl.BlockSpec((tm, ATT_WIDTH), row), pl.BlockSpec((tm, ATT_WIDTH), row),
                     pl.BlockSpec((tm, ATT_WIDTH), row), pl.BlockSpec((tm, N_HEADS), row),
                     pl.BlockSpec((tm, LANES), row)]
    else:
        n_pg, pg_tile = n // PAGE_SIZE, tm // PAGE_SIZE
        page = lambda b, i: (b * nt + i, 0, 0)
        n_feat = N_HEADS * LANES
        out_shape = [jax.ShapeDtypeStruct((nb, n_feat, n // nb), BF16),
                     jax.ShapeDtypeStruct((n, n_feat), BF16),
                     jax.ShapeDtypeStruct((n_pg, ATT_WIDTH, PAGE_SIZE), F32),
                     jax.ShapeDtypeStruct((n_pg, ATT_WIDTH, PAGE_SIZE), F32),
                     jax.ShapeDtypeStruct((nb, ATT_WIDTH, n // nb), BF16),
                     jax.ShapeDtypeStruct((n_pg, N_HEADS, PAGE_SIZE), F32)]
        out_specs = [pl.BlockSpec((None, n_feat, tm), lambda b, i: (b, 0, i)), pl.BlockSpec((tm, n_feat), row),
                     pl.BlockSpec((pg_tile, ATT_WIDTH, PAGE_SIZE), page),
                     pl.BlockSpec((pg_tile, ATT_WIDTH, PAGE_SIZE), page),
                     pl.BlockSpec((None, ATT_WIDTH, tm), lambda b, i: (b, 0, i)),
                     pl.BlockSpec((pg_tile, N_HEADS, PAGE_SIZE), page)]
    out_shape += shared_shape
    out_specs += shared_specs
    return pl.pallas_call(
        functools.partial(_inproj_kernel, sample=sample, nseq=nseq),
        out_shape=out_shape,
        grid=(nb, nt),
        in_specs=in_specs,
        out_specs=out_specs,
        scratch_shapes=[pltpu.VMEM((1, LANES), F32)],
        compiler_params=pltpu.CompilerParams(dimension_semantics=("arbitrary", "arbitrary"),
                                             vmem_limit_bytes=VMEM_LIMIT),
        name="inproj_sample" if sample else "inproj_prompt",
    )(x, mod, mod, g1, w_all, b_all, tri, place, ones_row)


def _attn_kernel(qat_ref, ka_ref, vt_ref, o_ref, s_even, s_odd, mx_even, mx_odd, *, tq, tk):
    qi = pl.program_id(2)
    s_bufs = (s_even, s_odd)
    mx_bufs = (mx_even, mx_odd)
    q_heads = [qat_ref[hh * LANES:(hh + 1) * LANES, :] for hh in range(2)]
    key = lax.broadcasted_iota(jnp.int32, (tk, ATTN_QCHUNK), 0)
    qry = lax.broadcasted_iota(jnp.int32, (tk, ATTN_QCHUNK), 1)

    units = [(hh, slice(c, c + ATTN_QCHUNK), hh * (tq // ATTN_QCHUNK) + c // ATTN_QCHUNK)
             for hh in range(2) for c in range(0, tq, ATTN_QCHUNK)]

    def scores(ki, parity, unit, diagonal):
        hh, cols, u = unit
        if diagonal and parity * tk >= cols.stop:
            s_bufs[parity][u] = jnp.full((tk, ATTN_QCHUNK), NEG_INF, F32)
            mx_bufs[parity][u] = jnp.full((1, ATTN_QCHUNK), NEG_INF, F32)
            return
        ka = ka_ref[pl.ds(pl.multiple_of(ki * tk, tk), tk), hh * LANES:(hh + 1) * LANES]
        st = jnp.dot(ka, q_heads[hh][:, cols], preferred_element_type=F32)
        if diagonal and (parity + 1) * tk > cols.start + 1:
            st = jnp.where(key + parity * tk <= qry + cols.start, st, NEG_INF)
        s_bufs[parity][u] = st
        mx_bufs[parity][u] = jnp.max(st, axis=0, keepdims=True)

    ones_rows = jnp.ones((SUM_ROWS, tk), BF16)

    def softmax_pv(ki, parity, unit, carry):
        hh, _, u = unit
        m_prev, acc = carry
        m_new = jnp.maximum(m_prev, mx_bufs[parity][u])
        alpha = jnp.exp2(m_prev - m_new)
        pt = jnp.exp2(s_bufs[parity][u] - m_new).astype(BF16)
        vt = vt_ref[hh * HEAD_DIM:(hh + 1) * HEAD_DIM, pl.ds(pl.multiple_of(ki * tk, tk), tk)]
        vt = jnp.concatenate([vt, ones_rows], axis=0)
        return m_new, acc * alpha + jnp.dot(vt, pt, preferred_element_type=F32)

    def stage(k_next, p_next, k_cur, p_cur, carry, diagonal=False):
        out = []
        for unit, c in zip(units, carry):
            scores(k_next, p_next, unit, diagonal)
            out.append(softmax_pv(k_cur, p_cur, unit, c))
        return tuple(out)

    unit0 = (jnp.full((1, ATTN_QCHUNK), NEG_INF, F32), jnp.zeros((HEAD_DIM + SUM_ROWS, ATTN_QCHUNK), F32))
    for unit in units:
        scores(2 * qi, 0, unit, True)
    carry = stage(2 * qi + 1, 1, 2 * qi, 0, (unit0,) * len(units), diagonal=True)

    def pair(j, carry):
        carry = stage(2 * j, 0, jnp.where(j == 0, 2 * qi + 1, 2 * j - 1), 1, carry)
        return stage(2 * j + 1, 1, 2 * j, 0, carry)

    carry = lax.fori_loop(0, qi, pair, carry)
    last = jnp.where(qi == 0, 1, 2 * qi - 1)
    heads = [[], []]
    for unit, c in zip(units, carry):
        _, acc = softmax_pv(last, 1, unit, c)
        heads[unit[0]].append(acc[:HEAD_DIM] / acc[HEAD_DIM:HEAD_DIM + 1])
    o = jnp.concatenate([jnp.concatenate(h, axis=1) for h in heads], axis=0)
    o_ref[...] = o.T.astype(o_ref.dtype)


def _prompt_attention(qat, ka, vt, *, nb, seq, tq):
    tk = tq // 2
    ka3 = ka.reshape(nb, seq, N_HEADS * LANES)
    out = pl.pallas_call(
        functools.partial(_attn_kernel, tq=tq, tk=tk),
        out_shape=jax.ShapeDtypeStruct((nb, seq, ATT_WIDTH), BF16),
        grid=(nb, N_PAIRS, seq // tq),
        in_specs=[pl.BlockSpec((None, 2 * LANES, tq), lambda b, p, i: (b, p, i)),
                  pl.BlockSpec((None, seq, 2 * LANES), lambda b, p, i: (b, 0, p)),
                  pl.BlockSpec((None, LANES, seq), lambda b, p, i: (b, p, 0))],
        out_specs=pl.BlockSpec((None, tq, LANES), lambda b, p, i: (b, i, p)),
        scratch_shapes=([pltpu.VMEM((2 * tq // ATTN_QCHUNK, tk, ATTN_QCHUNK), F32)] * 2
                        + [pltpu.VMEM((2 * tq // ATTN_QCHUNK, 1, ATTN_QCHUNK), F32)] * 2),
        compiler_params=pltpu.CompilerParams(dimension_semantics=("arbitrary", "arbitrary", "arbitrary"),
                                             vmem_limit_bytes=VMEM_LIMIT),
        name="prompt_attention",
    )(qat, ka3, vt)
    return out.reshape(nb * seq, ATT_WIDTH)


class _SampleAttention:
    SCORE_ROWS = N_HEADS * SAMPLE_T
    NT_DIMS = (((1,), (1,)), ((), ()))

    def __init__(self, q_scr, m_scr, l_scr, acc_scr, carry_scr):
        self.q_scr, self.m_scr, self.l_scr, self.acc_scr, self.carry_scr = q_scr, m_scr, l_scr, acc_scr, carry_scr

    @staticmethod
    def scratch_shapes():
        rows = _SampleAttention.SCORE_ROWS
        return [pltpu.VMEM((rows, ATT_WIDTH), BF16), pltpu.VMEM((rows, 1), F32), pltpu.VMEM((rows, 1), F32),
                pltpu.VMEM((rows, ATT_WIDTH), F32), pltpu.VMEM((N_HEADS, LANES), F32)]

    @staticmethod
    def per_head_rows(a):
        return jnp.concatenate([jnp.broadcast_to(a[h:h + 1, :], (SAMPLE_T, a.shape[1])) for h in range(N_HEADS)],
                               axis=0)

    def update(self, s, pv):
        m_prev = self.m_scr[...]
        m_new = jnp.maximum(m_prev, jnp.max(s, axis=-1, keepdims=True))
        alpha = jnp.exp(m_prev - m_new)
        p = jnp.exp(s - m_new)
        self.l_scr[...] = alpha * self.l_scr[...] + jnp.sum(p, axis=-1, keepdims=True)
        self.acc_scr[...] = alpha * self.acc_scr[...] + pv(p.astype(BF16))
        self.m_scr[...] = m_new

    def start(self, q_ref, kn_ref, vn_ref, cn_ref):
        rows, q_scr, nt_dims, per_head_rows = self.SCORE_ROWS, self.q_scr, self.NT_DIMS, self.per_head_rows
        m_scr, l_scr, acc_scr, carry_scr, update = self.m_scr, self.l_scr, self.acc_scr, self.carry_scr, self.update
        q8 = q_ref[...].astype(F32)
        qt = jnp.concatenate([q8] * N_HEADS, axis=0)
        r_i = lax.broadcasted_iota(jnp.int32, (rows, ATT_WIDTH), 0)
        c_i = lax.broadcasted_iota(jnp.int32, (rows, ATT_WIDTH), 1)
        q_scr[...] = jnp.where(r_i // SAMPLE_T == c_i // HEAD_DIM, qt, 0.0).astype(BF16)
        m_scr[...] = jnp.full_like(m_scr, NEG_INF)
        l_scr[...] = jnp.zeros_like(l_scr)
        acc_scr[...] = jnp.zeros_like(acc_scr)
        carry_scr[...] = jnp.zeros_like(carry_scr)
        pad = jnp.zeros((PAGE_SIZE - SAMPLE_T, ATT_WIDTH), F32)
        k_new = jnp.concatenate([kn_ref[...], pad], axis=0).astype(BF16)
        v_new = jnp.concatenate([vn_ref[...], pad], axis=0).astype(BF16)
        c_new = jnp.concatenate([cn_ref[...], jnp.zeros((PAGE_SIZE - SAMPLE_T, LANES), F32)], axis=0)
        s = lax.dot_general(q_scr[...], k_new, nt_dims, preferred_element_type=F32)
        s = s - per_head_rows(c_new.T[:N_HEADS, :])
        r_s = lax.broadcasted_iota(jnp.int32, (rows, PAGE_SIZE), 0)
        c_s = lax.broadcasted_iota(jnp.int32, (rows, PAGE_SIZE), 1)
        update(jnp.where(c_s <= r_s % SAMPLE_T, s, NEG_INF),
               lambda p: jnp.dot(p, v_new, preferred_element_type=F32))

    def page_scores(self, k_pages, lf_pages, tri_ref):
        pieces = []
        for lf in lf_pages:
            hi = lf.astype(BF16).astype(F32)
            mid = (lf - hi).astype(BF16).astype(F32)
            lo = (lf - hi - mid).astype(BF16).astype(F32)
            pieces += [hi, mid, lo]
        sums = jnp.dot(jnp.concatenate(pieces, axis=0).astype(BF16), tri_ref[...], preferred_element_type=F32)
        carry = self.carry_scr[...]
        bias = []
        for i in range(len(lf_pages)):
            part = [sums[(3 * i + j) * N_HEADS:(3 * i + j + 1) * N_HEADS, :] for j in range(AUG)]
            local = part[0] + part[1] + part[2]
            bias.append(local[:, :PAGE_SIZE] + carry)
            carry = carry + local[:, PAGE_SIZE:]
        self.carry_scr[...] = carry
        kt = jnp.concatenate([k.astype(BF16) for k in k_pages], axis=1)
        s = jnp.dot(self.q_scr[...], kt, preferred_element_type=F32)
        return s + self.per_head_rows(jnp.concatenate(bias, axis=1))

    def absorb_pages(self, s, v_pages):
        vt = jnp.concatenate([v.astype(BF16) for v in v_pages], axis=1)
        self.update(s, lambda p: lax.dot_general(p, vt, self.NT_DIMS, preferred_element_type=F32))

    def finish(self, o_ref):
        rows = self.SCORE_ROWS
        o = self.acc_scr[...] / self.l_scr[...]
        r_i = lax.broadcasted_iota(jnp.int32, (rows, ATT_WIDTH), 0)
        c_i = lax.broadcasted_iota(jnp.int32, (rows, ATT_WIDTH), 1)
        o = jnp.where(r_i // SAMPLE_T == c_i // HEAD_DIM, o, 0.0).astype(BF16)
        t_i = lax.broadcasted_iota(jnp.int32, (SAMPLE_T, rows), 0)
        r_j = lax.broadcasted_iota(jnp.int32, (SAMPLE_T, rows), 1)
        sel = jnp.where(r_j % SAMPLE_T == t_i, 1.0, 0.0).astype(BF16)
        o_ref[...] = jnp.dot(sel, o, preferred_element_type=F32).astype(o_ref.dtype)


def _merge_tail(rows, conv, g1, x_ref, o_ref, ga_ref, gb_ref, dwb_ref, lng_ref, lnb_ref, wpa_ref, wpb_ref, bpb_ref,
                wo_ref, out_ref, ya=None):
    yb = conv + dwb_ref[...]
    mu = jnp.mean(yb, axis=-1, keepdims=True)
    var = jnp.mean(jnp.square(yb - mu), axis=-1, keepdims=True)
    yb = (yb - mu) * lax.rsqrt(var + EPS) * lng_ref[...] + lnb_ref[...]
    yb = jnp.dot(_silu(yb).astype(BF16), wpb_ref[...], preferred_element_type=F32) + bpb_ref[...]
    if ya is None:
        ya = jnp.dot(o_ref[rows, :], wpa_ref[...], preferred_element_type=F32)
    m = ga_ref[rows, :].astype(F32) * ya + gb_ref[rows, :].astype(F32) * yb
    out_ref[rows, :] = x_ref[rows, :] + g1 * jnp.dot(m.astype(BF16), wo_ref[...], preferred_element_type=F32)


def _merge_prompt_kernel(x_ref, o_ref, ucur_ref, uprev_ref, ga_ref, gb_ref, g1_ref, dww_ref, dwb_ref, lng_ref,
                         lnb_ref, wpa_ref, wpb_ref, bpb_ref, wo_ref, out_ref, ubuf, *, nseq, row_block):
    tm = x_ref.shape[0]
    first = pl.program_id(1) == 0
    prev = uprev_ref[...]
    ubuf[:, 0:CONV_HALO, :] = jnp.where(first, jnp.zeros_like(prev), prev)
    ubuf[:, CONV_HALO:, :] = ucur_ref[...]
    g1 = _mod_rows(g1_ref, nseq, False)
    chunk = 64
    ya_all = jnp.dot(o_ref[...], wpa_ref[...], preferred_element_type=F32)
    for b0 in range(0, tm, row_block):
        slabs = []
        for c in range(N_SLABS):
            pieces = []
            for r0 in range(b0, b0 + row_block, chunk):
                acc = jnp.zeros((chunk, LANES), F32)
                for j in range(CONV_K):
                    off = r0 + CONV_HALO - (CONV_K - 1) + j
                    acc = acc + dww_ref[c, j:j + 1, :] * ubuf[c, off:off + chunk, :]
                pieces.append(acc)
            slabs.append(jnp.concatenate(pieces, axis=0))
        _merge_tail(slice(b0, b0 + row_block), jnp.concatenate(slabs, axis=1), g1, x_ref, o_ref, ga_ref, gb_ref,
                    dwb_ref, lng_ref, lnb_ref, wpa_ref, wpb_ref, bpb_ref, wo_ref, out_ref,
                    ya=ya_all[b0:b0 + row_block])


def _merge_sample_kernel(x_ref, o_ref, uwin_ref, ga_ref, gb_ref, g1_ref, dww_ref, dwb_ref, lng_ref,
                         lnb_ref, wpa_ref, wpb_ref, bpb_ref, wo_ref, out_ref, *, nseq):
    slabs = []
    for c in range(N_SLABS):
        acc = jnp.zeros((nseq, SAMPLE_T, LANES), F32)
        for j in range(CONV_K):
            acc = acc + dww_ref[c, j:j + 1, :] * uwin_ref[c, :, j:j + SAMPLE_T, :]
        slabs.append(acc.reshape(nseq * SAMPLE_T, LANES))
    _merge_tail(slice(None), jnp.concatenate(slabs, axis=1), _mod_rows(g1_ref, nseq, True), x_ref, o_ref, ga_ref,
                gb_ref, dwb_ref, lng_ref, lnb_ref, wpa_ref, wpb_ref, bpb_ref, wo_ref, out_ref)


def _merge(x, o, u_args, ga, gb, mod, weights, *, nb, tm, sample, nseq):
    n = x.shape[0]
    nt = n // (nb * tm)
    row = lambda b, i: (b * nt + i, 0)
    const2 = lambda b, i: (0, 0)
    const3 = lambda b, i: (0, 0, 0)
    once = dict(pipeline_mode=pl.Buffered(1))
    mod_spec = _mod_spec(mod, 2)
    if sample:
        (uwin,) = u_args
        u_specs = [pl.BlockSpec(uwin.shape, lambda b, i: (0, 0, 0, 0))]
        kernel, scratch = functools.partial(_merge_sample_kernel, nseq=nseq), []
    else:
        (u,) = u_args
        u_args = (u, u)
        blocks_per_tile = tm // CONV_HALO
        u_specs = [pl.BlockSpec((N_SLABS, tm, LANES), lambda b, i: (0, b * nt + i, 0)),
                   pl.BlockSpec((N_SLABS, CONV_HALO, LANES),
                                lambda b, i: (0, jnp.maximum((b * nt + i) * blocks_per_tile - 1, 0), 0))]
        kernel = functools.partial(_merge_prompt_kernel, nseq=nseq, row_block=tm // 2)
        scratch = [pltpu.VMEM((N_SLABS, CONV_HALO + tm, LANES), F32)]
    dww, dwb, lng, lnb, wpa, wpb, bpb, wo = weights
    in_specs = ([pl.BlockSpec((tm, D_MODEL), row), pl.BlockSpec((tm, ATT_WIDTH), row)] + u_specs
                + [pl.BlockSpec((tm, D_MODEL), row), pl.BlockSpec((tm, D_MODEL), row), mod_spec,
                   pl.BlockSpec(dww.shape, const3),
                   pl.BlockSpec((1, CONV_WIDTH), const2), pl.BlockSpec((1, CONV_WIDTH), const2),
                   pl.BlockSpec((1, CONV_WIDTH), const2),
                   pl.BlockSpec((ATT_WIDTH, D_MODEL), const2, **once),
                   pl.BlockSpec((CONV_WIDTH, D_MODEL), const2, **once),
                   pl.BlockSpec((1, D_MODEL), const2),
                   pl.BlockSpec((D_MODEL, D_MODEL), const2, **once)])
    return pl.pallas_call(
        kernel,
        out_shape=jax.ShapeDtypeStruct((n, D_MODEL), F32),
        grid=(nb, nt),
        in_specs=in_specs,
        out_specs=pl.BlockSpec((tm, D_MODEL), row),
        scratch_shapes=scratch,
        compiler_params=pltpu.CompilerParams(dimension_semantics=("arbitrary", "arbitrary"),
                                             vmem_limit_bytes=VMEM_LIMIT),
        name="merge_sample" if sample else "merge_prompt",
    )(x, o, *u_args, ga, gb, mod, dww, dwb, lng, lnb, wpa, wpb, bpb, wo)


def _ffn_steps(x_ref, sh_ref, sc_ref, g2_ref, rg_ref, fg_ref, win_ref, wout_ref, out_ref, *, bounds, sample, nseq):
    x = x_ref[...]
    ms = jnp.mean(x * x, axis=-1, keepdims=True)
    h = x * lax.rsqrt(ms + EPS) * rg_ref[...]
    hb = (h * (1.0 + _mod_rows(sc_ref, nseq, sample)) + _mod_rows(sh_ref, nseq, sample)).astype(BF16)
    acc = jnp.zeros(x.shape, F32)
    for lo, hi in zip(bounds[:-1], bounds[1:]):
        gate = jnp.dot(hb, win_ref[:, lo:hi], preferred_element_type=F32)
        up = jnp.dot(hb, win_ref[:, FFN_HIDDEN + lo:FFN_HIDDEN + hi], preferred_element_type=F32)
        yield
        act = (_silu(gate) * up).astype(BF16)
        acc = acc + jnp.dot(act, wout_ref[lo:hi, :], preferred_element_type=F32)
        if hi == bounds[-1]:
            x2 = x + _mod_rows(g2_ref, nseq, sample) * acc
            ms2 = jnp.mean(x2 * x2, axis=-1, keepdims=True)
            out_ref[...] = x2 * lax.rsqrt(ms2 + EPS) * fg_ref[...]
        yield


def _hidden_bounds(n_chunks):
    tiles = FFN_HIDDEN // MXU_TILE
    assert tiles * MXU_TILE == FFN_HIDDEN
    return [MXU_TILE * ((tiles * k) // n_chunks) for k in range(n_chunks + 1)]


def _ffn_kernel(*refs, sample, nseq):
    for _ in _ffn_steps(*refs, bounds=_hidden_bounds(2), sample=sample, nseq=nseq):
        pass


def _ffn_attn_kernel(pt_ref, x_ref, sh_ref, sc_ref, g2_ref, rg_ref, fg_ref, win_ref, wout_ref,
                     q_ref, kn_ref, vn_ref, cn_ref, tri_ref, ck_hbm, cv_hbm, clf_hbm,
                     out_ref, o_ref, kbuf, vbuf, lbuf, sem, *state, nseq, n_pages):
    npg = PAGES_PER_STEP
    n_chunks = n_pages // npg
    seq = pl.program_id(0) * pl.num_programs(1) + pl.program_id(1)
    attn = _SampleAttention(*state)

    def copies(sequence, c):
        slot, out = c % 2, []
        for j in range(npg):
            page = pt_ref[sequence * n_pages + (n_pages - 1) - (c * npg + j)]
            out += [pltpu.make_async_copy(ck_hbm.at[page], kbuf.at[slot, j], sem.at[0, slot]),
                    pltpu.make_async_copy(cv_hbm.at[page], vbuf.at[slot, j], sem.at[1, slot]),
                    pltpu.make_async_copy(clf_hbm.at[page], lbuf.at[slot, j], sem.at[2, slot])]
        return out

    def start(sequence, c):
        for cp in copies(sequence, c):
            cp.start()

    @pl.when(seq == 0)
    def _():
        start(seq, 0)

    ffn = _ffn_steps(x_ref, sh_ref, sc_ref, g2_ref, rg_ref, fg_ref, win_ref, wout_ref, out_ref,
                     bounds=_hidden_bounds(n_chunks), sample=False, nseq=nseq)
    for c in range(n_chunks):
        for cp in copies(seq, c):
            cp.wait()
        if c + 1 < n_chunks:
            start(seq, c + 1)
        else:
            @pl.when(seq + 1 < nseq)
            def _():
                start(seq + 1, 0)
        if c == 0:
            attn.start(q_ref, kn_ref, vn_ref, cn_ref)
        slot = c % 2
        next(ffn)
        s = attn.page_scores([kbuf[slot, j] for j in range(npg)], [lbuf[slot, j] for j in range(npg)], tri_ref)
        next(ffn)
        attn.absorb_pages(s, [vbuf[slot, j] for j in range(npg)])
    attn.finish(o_ref)


def _ffn(x, mod, rms_g, final_g, w_in, w_out, *, nb, tm, sample, nseq):
    n = x.shape[0]
    nt = n // (nb * tm)
    row = lambda b, i: (b * nt + i, 0)
    const2 = lambda b, i: (0, 0)
    once = dict(pipeline_mode=pl.Buffered(1))
    return pl.pallas_call(
        functools.partial(_ffn_kernel, sample=sample, nseq=nseq),
        out_shape=jax.ShapeDtypeStruct((n, D_MODEL), F32),
        grid=(nb, nt),
        in_specs=[pl.BlockSpec((tm, D_MODEL), row), _mod_spec(mod, 3), _mod_spec(mod, 4), _mod_spec(mod, 5),
                  pl.BlockSpec((1, D_MODEL), const2), pl.BlockSpec((1, D_MODEL), const2),
                  pl.BlockSpec((D_MODEL, 2 * FFN_HIDDEN), const2, **once),
                  pl.BlockSpec((FFN_HIDDEN, D_MODEL), const2, **once)],
        out_specs=pl.BlockSpec((tm, D_MODEL), row),
        compiler_params=pltpu.CompilerParams(dimension_semantics=("arbitrary", "arbitrary"),
                                             vmem_limit_bytes=VMEM_LIMIT),
        name="ffn_sample" if sample else "ffn_prompt",
    )(x, mod, mod, mod, rms_g, final_g, w_in, w_out)


def _ffn_with_sample_attention(x, mod, rms_g, final_g, w_in, w_out, page_table, q, k_new, v_new, c_new, tri_page,
                               cache_k, cache_v, cache_logf, *, nb, tm):
    n = x.shape[0]
    nt = n // (nb * tm)
    nseq, n_pages = page_table.shape
    assert nseq == nb * nt and n_pages % PAGES_PER_STEP == 0
    row = lambda b, i, pt: (b * nt + i, 0)
    const2 = lambda b, i, pt: (0, 0)
    once = dict(pipeline_mode=pl.Buffered(1))
    mod_spec = lambda kind: pl.BlockSpec((None, mod.shape[1], D_MODEL), lambda b, i, pt: (kind, 0, 0))
    seq_spec = lambda width: pl.BlockSpec((None, SAMPLE_T, width), lambda b, i, pt: (b * nt + i, 0, 0))
    hbm = pl.BlockSpec(memory_space=pl.ANY)
    slots = 2
    grid_spec = pltpu.PrefetchScalarGridSpec(
        num_scalar_prefetch=1,
        grid=(nb, nt),
        in_specs=[pl.BlockSpec((tm, D_MODEL), row), mod_spec(3), mod_spec(4), mod_spec(5),
                  pl.BlockSpec((1, D_MODEL), const2), pl.BlockSpec((1, D_MODEL), const2),
                  pl.BlockSpec((D_MODEL, 2 * FFN_HIDDEN), const2, **once),
                  pl.BlockSpec((FFN_HIDDEN, D_MODEL), const2, **once),
                  seq_spec(ATT_WIDTH), seq_spec(ATT_WIDTH), seq_spec(ATT_WIDTH), seq_spec(LANES),
                  pl.BlockSpec((PAGE_SIZE, 2 * PAGE_SIZE), const2), hbm, hbm, hbm],
        out_specs=[pl.BlockSpec((tm, D_MODEL), row), seq_spec(ATT_WIDTH)],
        scratch_shapes=[pltpu.VMEM((slots, PAGES_PER_STEP, ATT_WIDTH, PAGE_SIZE), F32),
                        pltpu.VMEM((slots, PAGES_PER_STEP, ATT_WIDTH, PAGE_SIZE), F32),
                        pltpu.VMEM((slots, PAGES_PER_STEP, N_HEADS, PAGE_SIZE), F32),
                        pltpu.SemaphoreType.DMA((3, slots))] + _SampleAttention.scratch_shapes())
    return pl.pallas_call(
        functools.partial(_ffn_attn_kernel, nseq=nseq, n_pages=n_pages),
        out_shape=[jax.ShapeDtypeStruct((n, D_MODEL), F32),
                   jax.ShapeDtypeStruct((nseq, SAMPLE_T, ATT_WIDTH), BF16)],
        grid_spec=grid_spec,
        compiler_params=pltpu.CompilerParams(dimension_semantics=("arbitrary", "arbitrary"),
                                             vmem_limit_bytes=VMEM_LIMIT),
        name="ffn_prompt_sample_attention",
    )(page_table.reshape(-1), x, mod, mod, mod, rms_g, final_g, w_in, w_out, q, k_new, v_new, c_new, tri_page,
      cache_k, cache_v, cache_logf)


def _bias_placement():
    half = N_HEADS * LANES
    place = np.zeros((LANES, 2 * half), np.float32)
    ones = np.zeros((1, 2 * half), np.float32)
    for h in range(N_HEADS):
        base = h * LANES + (HEAD_DIM if h % 2 == 0 else 0)
        for piece in range(AUG):
            place[piece * N_HEADS + h, base + piece] = 1.0
            ones[0, base + AUG + piece] = 1.0
            ones[0, half + base + piece] = 1.0
            place[piece * N_HEADS + h, half + base + AUG + piece] = -1.0
    return jnp.asarray(place, BF16), jnp.asarray(ones, F32)


def _lower_tri(n, block):
    t = np.arange(n)[:, None]
    s = np.arange(n)[None, :]
    return jnp.asarray(((s <= t) & (t // block == s // block)).astype(np.float32), BF16)


def _later_keys(n):
    j = np.arange(n)[:, None]
    s = np.arange(n)[None, :]
    return jnp.asarray(np.concatenate([(j > s).astype(np.float32), np.ones((n, n), np.float32)], axis=1), BF16)


def kernel(x_prompt, x_sample, c_prompt, c_sample, cache_k, cache_v, cache_logf, state_conv, page_table, rms1_g, rms2_g, w_ada, b_ada, w_in, b_in, dw_w, dw_b, ln_g, ln_b, w_pa, w_pb, b_pb, w_o, w_ffn_in, w_ffn_out, final_g):
    nb, seq, _ = x_prompt.shape
    nseq, dec_t, _ = x_sample.shape
    depth = w_in.shape[0]
    assert depth == 1 and dec_t <= SAMPLE_T
    n_prompt = nb * seq
    tm = 512
    n_sample = nseq * SAMPLE_T

    wt, b = w_in[0].T, b_in[0]
    g_off = 3 * ATT_WIDTH + N_HEADS
    w_all = jnp.concatenate([wt[:g_off], jnp.zeros((LANES - N_HEADS, D_MODEL), F32), wt[g_off:]], axis=0).astype(BF16)
    b_all = jnp.concatenate([b[:g_off], jnp.zeros((LANES - N_HEADS,), F32), b[g_off:]])[None, :]
    dww = jnp.pad(dw_w[0], ((0, CONV_HALO - CONV_K), (0, 0))).reshape(CONV_HALO, N_SLABS, LANES).transpose(1, 0, 2)
    merge_w = (dww, dw_b[0][None, :], ln_g[0][None, :], ln_b[0][None, :], w_pa[0].astype(BF16),
               w_pb[0].astype(BF16), b_pb[0][None, :], w_o[0].astype(BF16))
    wf_in, wf_out = w_ffn_in[0].astype(BF16), w_ffn_out[0].astype(BF16)
    g1w, g2w, gfw = rms1_g[0][None, :], rms2_g[0][None, :], final_g[None, :]
    place, ones_row = _bias_placement()

    n_cond = nb + nseq
    c_all = jnp.pad(jnp.concatenate([c_sample, c_prompt], axis=0), ((0, -n_cond % 8), (0, 0)))
    mod = _modulation(c_all, w_ada[0], b_ada[0][None, :])

    xs = jnp.pad(x_sample, ((0, 0), (0, SAMPLE_T - dec_t), (0, 0))).reshape(n_sample, D_MODEL)
    (q_s, k_s, v_s, lf_s, c_s, u_s, ga_s, gb_s) = _inproj(
        xs, mod, g1w, w_all, b_all, _lower_tri(n_sample, SAMPLE_T), place, ones_row,
        nb=1, tm=n_sample, sample=True, nseq=nseq)

    xp = x_prompt.reshape(n_prompt, D_MODEL)
    (qa, ka, kt_p, vt_p, vb_p, lft_p, u_p, ga_p, gb_p) = _inproj(
        xp, mod, g1w, w_all, b_all, _lower_tri(tm, tm), place, ones_row, nb=nb, tm=tm, sample=False, nseq=nseq)
    o_p = _prompt_attention(qa, ka, vb_p, nb=nb, seq=seq, tq=2 * tm)
    x1_p = _merge(xp, o_p, (u_p,), ga_p, gb_p, mod, merge_w, nb=nb, tm=tm, sample=False, nseq=nseq)
    n_phys = cache_k.shape[1]
    page_t = lambda c: c[0].transpose(0, 2, 3, 1).reshape(n_phys, ATT_WIDTH, PAGE_SIZE)
    y_p, o_s = _ffn_with_sample_attention(
        x1_p, mod, g2w, gfw, wf_in, wf_out,
        page_table, q_s.reshape(nseq, SAMPLE_T, ATT_WIDTH), k_s.reshape(nseq, SAMPLE_T, ATT_WIDTH),
        v_s.reshape(nseq, SAMPLE_T, ATT_WIDTH), c_s.reshape(nseq, SAMPLE_T, LANES), _later_keys(PAGE_SIZE),
        page_t(cache_k), page_t(cache_v), cache_logf[0].transpose(0, 2, 1), nb=nb, tm=tm)

    state_slabs = state_conv[0].reshape(nseq, CONV_K - 1, N_SLABS, LANES).transpose(2, 0, 1, 3)
    u_slabs = u_s.reshape(N_SLABS, nseq, SAMPLE_T, LANES)
    uwin = jnp.concatenate(
        [state_slabs, u_slabs, jnp.zeros((N_SLABS, nseq, SAMPLE_WIN - (CONV_K - 1) - SAMPLE_T, LANES), F32)], axis=2)
    x1_s = _merge(xs, o_s.reshape(n_sample, ATT_WIDTH), (uwin,), ga_s, gb_s, mod, merge_w,
                  nb=1, tm=n_sample, sample=True, nseq=nseq)
    y_s = _ffn(x1_s, mod, g2w, gfw, wf_in, wf_out, nb=1, tm=n_sample, sample=True, nseq=nseq)

    n_pg = seq // PAGE_SIZE
    tail = CONV_K - 1
    u_tail = u_p.reshape(N_SLABS, nb, seq, LANES)[:, :, seq - tail:]
    u_tail = u_tail.transpose(1, 2, 0, 3).reshape(nb, tail, CONV_WIDTH)
    us_rows = u_s.reshape(N_SLABS, nseq, SAMPLE_T, LANES)[:, :, :dec_t]
    us_rows = us_rows.transpose(1, 2, 0, 3).reshape(nseq, dec_t, CONV_WIDTH)
    unpad = lambda a, width: a.reshape(nseq, SAMPLE_T, *width)[:, :dec_t]
    return (y_p.reshape(nb, seq, D_MODEL),
            unpad(y_s, (D_MODEL,)),
            kt_p.reshape(1, nb, n_pg, N_HEADS, HEAD_DIM, PAGE_SIZE).transpose(0, 1, 2, 5, 3, 4),
            vt_p.reshape(1, nb, n_pg, N_HEADS, HEAD_DIM, PAGE_SIZE).transpose(0, 1, 2, 5, 3, 4),
            lft_p.reshape(1, nb, n_pg, N_HEADS, PAGE_SIZE).transpose(0, 1, 2, 4, 3),
            u_tail[None],
            unpad(k_s, (N_HEADS, HEAD_DIM))[None],
            unpad(v_s, (N_HEADS, HEAD_DIM))[None],
            unpad(lf_s, (N_HEADS,))[None],
            jnp.concatenate([state_conv[0][:, dec_t:], us_rows], axis=1)[None])
```

```python
import functools

import numpy as np
import jax
import jax.numpy as jnp
from jax import lax
from jax.experimental import pallas as pl
from jax.experimental.pallas import tpu as pltpu

F32 = jnp.float32
BF16 = jnp.bfloat16

D_MODEL = 1024
N_HEADS = 8
HEAD_DIM = 64
ATT_WIDTH = N_HEADS * HEAD_DIM
CONV_WIDTH = 512
CONV_K = 31
FFN_HIDDEN = 2816
PAGE_SIZE = 128
EPS = 1e-6
NEG_INF = -1e30
SCALE = HEAD_DIM ** -0.5
LOG2E = 1.4426950408889634

LANES = 128
MXU_TILE = 256
N_PAIRS = N_HEADS // 2
N_SLABS = CONV_WIDTH // LANES
AUG = 3
SUM_ROWS = 16
ATTN_QCHUNK = 256
SAMPLE_T = 8
CONV_HALO = 32
SAMPLE_WIN = 40
PAGES_PER_STEP = 16
VMEM_LIMIT = 56 * 1024 * 1024

C_Q, C_K, C_V, C_F, C_GLA, C_GLB, C_GA, C_GB, C_END = 0, 512, 1024, 1536, 1664, 2176, 2688, 3712, 4736


def _sigmoid(x):
    return 1.0 / (1.0 + jnp.exp(-x))


def _silu(x):
    return x * _sigmoid(x)


def _split3_packed(a, lane):
    a = jnp.where(lane < N_HEADS, a, 0.0)
    hi = a.astype(BF16).astype(F32)
    r1 = a - hi
    mid = r1.astype(BF16).astype(F32)
    lo = (r1 - mid).astype(BF16).astype(F32)
    packed = hi + pltpu.roll(mid, N_HEADS, axis=1) + pltpu.roll(lo, 2 * N_HEADS, axis=1)
    return packed.astype(BF16)


def _unpack3(p, lane):
    s = p + pltpu.roll(p, LANES - N_HEADS, axis=1) + pltpu.roll(p, LANES - 2 * N_HEADS, axis=1)
    return jnp.where(lane < N_HEADS, s, 0.0)


def _mod_kernel(c_ref, w_ref, b_ref, o_ref):
    s = _silu(c_ref[...]).astype(BF16)
    o_ref[...] = jnp.dot(s, w_ref[...].astype(BF16), preferred_element_type=F32) + b_ref[...]


def _modulation(c_all, w_ada, b_ada):
    rows = c_all.shape[0]
    n_mod = w_ada.shape[1] // D_MODEL
    return pl.pallas_call(
        _mod_kernel,
        out_shape=jax.ShapeDtypeStruct((n_mod, rows, D_MODEL), F32),
        grid=(n_mod,),
        in_specs=[pl.BlockSpec((rows, D_MODEL), lambda j: (0, 0)),
                  pl.BlockSpec((D_MODEL, D_MODEL), lambda j: (0, j)),
                  pl.BlockSpec((1, D_MODEL), lambda j: (0, j))],
        out_specs=pl.BlockSpec((None, rows, D_MODEL), lambda j: (j, 0, 0)),
        compiler_params=pltpu.CompilerParams(dimension_semantics=("arbitrary",), vmem_limit_bytes=VMEM_LIMIT),
        name="modulation",
    )(c_all, w_ada, b_ada)


def _mod_rows(ref, nseq, sample):
    if sample:
        m = ref[0:nseq, :]
        return jnp.broadcast_to(m[:, None, :], (nseq, SAMPLE_T, D_MODEL)).reshape(nseq * SAMPLE_T, D_MODEL)
    return ref[pl.ds(nseq + pl.program_id(0), 1), :]


def _mod_spec(mod, kind):
    return pl.BlockSpec((None, mod.shape[1], D_MODEL), lambda b, i: (kind, 0, 0))


def _inproj_kernel(x_ref, sh_ref, sc_ref, g_ref, w_ref, b_ref, tri_ref, place_ref, ones_ref, *refs, sample, nseq):
    if sample:
        qp_ref, k_ref, v_ref, lf_ref, c_ref, u_ref, ga_ref, gb_ref, carry_ref = refs
    else:
        qat_ref, ka_ref, kt_ref, vt_ref, vb_ref, lft_ref, u_ref, ga_ref, gb_ref, carry_ref = refs
    tm = x_ref.shape[0]

    @pl.when(pl.program_id(1) == 0)
    def _():
        carry_ref[...] = jnp.zeros_like(carry_ref)

    x = x_ref[...]
    ms = jnp.mean(x * x, axis=-1, keepdims=True)
    h = x * lax.rsqrt(ms + EPS) * g_ref[...]
    h = h * (1.0 + _mod_rows(sc_ref, nseq, sample)) + _mod_rows(sh_ref, nseq, sample)
    hb = h.astype(BF16)

    def seg(lo, hi):
        z = lax.dot_general(hb, w_ref[lo:hi, :], (((1,), (1,)), ((), ())), preferred_element_type=F32)
        return z + b_ref[:, lo:hi]

    lane = lax.broadcasted_iota(jnp.int32, (tm, LANES), 1)
    zf = seg(C_F, C_GLA)
    lf = jnp.minimum(zf, 0.0) - jnp.log1p(jnp.exp(-jnp.abs(zf)))
    lf = jnp.where(lane < N_HEADS, lf, 0.0)
    csum = _unpack3(jnp.dot(tri_ref[...], _split3_packed(lf, lane), preferred_element_type=F32), lane)
    csum = csum + carry_ref[...]
    carry_ref[...] = csum[tm - 1:tm, :]

    zq = seg(C_Q, C_K)
    zk = seg(C_K, C_V)
    zv = seg(C_V, C_F)
    if sample:
        qp_ref[...] = (zq * SCALE).astype(BF16)
        k_ref[...] = zk
        v_ref[...] = zv
        lf_ref[...] = lf[:, :N_HEADS]
        c_ref[...] = csum
    else:
        for pg in range(tm // PAGE_SIZE):
            rows = slice(pg * PAGE_SIZE, (pg + 1) * PAGE_SIZE)
            kt_ref[pg] = zk[rows, :].T
            vt_page = zv[rows, :].T
            vt_ref[pg] = vt_page
            vb_ref[:, rows] = vt_page.astype(BF16)
            lft_ref[pg] = lf[rows, :].T[:N_HEADS, :]
        zq = zq * (SCALE * LOG2E)
        aug = jnp.dot(_split3_packed(csum * LOG2E, lane), place_ref[...], preferred_element_type=F32) + ones_ref[...]
        for h in range(N_HEADS):
            own_half = (lane < HEAD_DIM) if h % 2 == 0 else (lane >= HEAD_DIM)
            pair, feat = slice((h // 2) * LANES, (h // 2 + 1) * LANES), slice(h * LANES, (h + 1) * LANES)
            qat_ref[feat, :] = jnp.where(own_half, zq[:, pair], aug[:, feat]).T.astype(BF16)
            k_feat = jnp.where(own_half, zk[:, pair], aug[:, N_HEADS * LANES + h * LANES:N_HEADS * LANES + (h + 1) * LANES])
            ka_ref[:, feat] = k_feat.astype(BF16)

    u = seg(C_GLA, C_GLB) * _sigmoid(seg(C_GLB, C_GA))
    for c in range(N_SLABS):
        u_ref[c] = u[:, c * LANES:(c + 1) * LANES]
    ga_ref[...] = _sigmoid(seg(C_GA, C_GB)).astype(BF16)
    gb_ref[...] = _sigmoid(seg(C_GB, C_END)).astype(BF16)


def _inproj(x, mod, g1, w_all, b_all, tri, place, ones_row, *, nb, tm, sample, nseq):
    n = x.shape[0]
    nt = n // (nb * tm)
    row = lambda b, i: (b * nt + i, 0)
    const2 = lambda b, i: (0, 0)
    once = dict(pipeline_mode=pl.Buffered(1))
    in_specs = [pl.BlockSpec((tm, D_MODEL), row), _mod_spec(mod, 0), _mod_spec(mod, 1),
                pl.BlockSpec((1, D_MODEL), const2),
                pl.BlockSpec((C_END, D_MODEL), const2, **once),
                pl.BlockSpec((1, C_END), const2),
                pl.BlockSpec((tm, tm), const2, **once),
                pl.BlockSpec(place.shape, const2, **once),
                pl.BlockSpec(ones_row.shape, const2)]
    shared_shape = [jax.ShapeDtypeStruct((N_SLABS, n, LANES), F32),
                    jax.ShapeDtypeStruct((n, D_MODEL), BF16),
                    jax.ShapeDtypeStruct((n, D_MODEL), BF16)]
    shared_specs = [pl.BlockSpec((N_SLABS, tm, LANES), lambda b, i: (0, b * nt + i, 0)),
                    pl.BlockSpec((tm, D_MODEL), row), pl.BlockSpec((tm, D_MODEL), row)]
    if sample:
        out_shape = [jax.ShapeDtypeStruct((n, ATT_WIDTH), BF16),
                     jax.ShapeDtypeStruct((n, ATT_WIDTH), F32),
                     jax.ShapeDtypeStruct((n, ATT_WIDTH), F32),
                     jax.ShapeDtypeStruct((n, N_HEADS), F32),
                     jax.ShapeDtypeStruct((n, LANES), F32)]
        out_specs = [pl.BlockSpec((tm, ATT_WIDTH), row), pl.BlockSpec((tm, ATT_WIDTH), row),
                     pl.BlockSpec((tm, ATT_WIDTH), row), pl.BlockSpec((tm, N_HEADS), row),
                     pl.BlockSpec((tm, LANES), row)]
    else:
        n_pg, pg_tile = n // PAGE_SIZE, tm // PAGE_SIZE
        page = lambda b, i: (b * nt + i, 0, 0)
        n_feat = N_HEADS * LANES
        out_shape = [jax.ShapeDtypeStruct((nb, n_feat, n // nb), BF16),
                     jax.ShapeDtypeStruct((n, n_feat), BF16),
                     jax.ShapeDtypeStruct((n_pg, ATT_WIDTH, PAGE_SIZE), F32),
                     jax.ShapeDtypeStruct((n_pg, ATT_WIDTH, PAGE_SIZE), F32),
                     jax.ShapeDtypeStruct((nb, ATT_WIDTH, n // nb), BF16),
                     jax.ShapeDtypeStruct((n_pg, N_HEADS, PAGE_SIZE), F32)]
        out_specs = [pl.BlockSpec((None, n_feat, tm), lambda b, i: (b, 0, i)), pl.BlockSpec((tm, n_feat), row),
                     pl.BlockSpec((pg_tile, ATT_WIDTH, PAGE_SIZE), page),
                     pl.BlockSpec((pg_tile, ATT_WIDTH, PAGE_SIZE), page),
                     pl.BlockSpec((None, ATT_WIDTH, tm), lambda b, i: (b, 0, i)),
                     pl.BlockSpec((pg_tile, N_HEADS, PAGE_SIZE), page)]
    out_shape += shared_shape
    out_specs += shared_specs
    return pl.pallas_call(
        functools.partial(_inproj_kernel, sample=sample, nseq=nseq),
        out_shape=out_shape,
        grid=(nb, nt),
        in_specs=in_specs,
        out_specs=out_specs,
        scratch_shapes=[pltpu.VMEM((1, LANES), F32)],
        compiler_params=pltpu.CompilerParams(dimension_semantics=("arbitrary", "arbitrary"),
                                             vmem_limit_bytes=VMEM_LIMIT),
        name="inproj_sample" if sample else "inproj_prompt",
    )(x, mod, mod, g1, w_all, b_all, tri, place, ones_row)


def _attn_kernel(qat_ref, ka_ref, vt_ref, o_ref, s_even, s_odd, mx_even, mx_odd, *, tq, tk):
    qi = pl.program_id(2)
    s_bufs = (s_even, s_odd)
    mx_bufs = (mx_even, mx_odd)
    q_heads = [qat_ref[hh * LANES:(hh + 1) * LANES, :] for hh in range(2)]
    key = lax.broadcasted_iota(jnp.int32, (tk, ATTN_QCHUNK), 0)
    qry = lax.broadcasted_iota(jnp.int32, (tk, ATTN_QCHUNK), 1)

    units = [(hh, slice(c, c + ATTN_QCHUNK), hh * (tq // ATTN_QCHUNK) + c // ATTN_QCHUNK)
             for hh in range(2) for c in range(0, tq, ATTN_QCHUNK)]

    def scores(ki, parity, unit, diagonal):
        hh, cols, u = unit
        ka = ka_ref[pl.ds(pl.multiple_of(ki * tk, tk), tk), hh * LANES:(hh + 1) * LANES]
        st = jnp.dot(ka, q_heads[hh][:, cols], preferred_element_type=F32)
        if diagonal and (parity + 1) * tk > cols.start + 1:
            st = jnp.where(key + parity * tk <= qry + cols.start, st, NEG_INF)
        s_bufs[parity][u] = st
        mx_bufs[parity][u] = jnp.max(st, axis=0, keepdims=True)

    ones_rows = jnp.ones((SUM_ROWS, tk), BF16)

    def softmax_pv(ki, parity, unit, carry):
        hh, _, u = unit
        m_prev, acc = carry
        m_new = jnp.maximum(m_prev, mx_bufs[parity][u])
        alpha = jnp.exp2(m_prev - m_new)
        pt = jnp.exp2(s_bufs[parity][u] - m_new).astype(BF16)
        vt = vt_ref[hh * HEAD_DIM:(hh + 1) * HEAD_DIM, pl.ds(pl.multiple_of(ki * tk, tk), tk)]
        vt = jnp.concatenate([vt, ones_rows], axis=0)
        return m_new, acc * alpha + jnp.dot(vt, pt, preferred_element_type=F32)

    def visible(parity, unit):
        return parity * tk < unit[1].stop

    def stage(k_next, p_next, k_cur, p_cur, carry, next_diagonal=False, cur_diagonal=False):
        out = []
        for unit, c in zip(units, carry):
            if not next_diagonal or visible(p_next, unit):
                scores(k_next, p_next, unit, next_diagonal)
            skip = cur_diagonal and not visible(p_cur, unit)
            out.append(c if skip else softmax_pv(k_cur, p_cur, unit, c))
        return tuple(out)

    def finish(carry):
        heads = [[], []]
        for unit, c in zip(units, carry):
            _, acc = softmax_pv(2 * qi + 1, 1, unit, c) if visible(1, unit) else c
            heads[unit[0]].append(acc[:HEAD_DIM] / acc[HEAD_DIM:HEAD_DIM + 1])
        o = jnp.concatenate([jnp.concatenate(h, axis=1) for h in heads], axis=0)
        o_ref[...] = o.T.astype(o_ref.dtype)

    unit0 = (jnp.full((1, ATTN_QCHUNK), NEG_INF, F32), jnp.zeros((HEAD_DIM + SUM_ROWS, ATTN_QCHUNK), F32))
    carry0 = (unit0,) * len(units)

    @pl.when(qi == 0)
    def _():
        for unit in units:
            scores(0, 0, unit, True)
        finish(stage(1, 1, 0, 0, carry0, next_diagonal=True, cur_diagonal=True))

    @pl.when(qi > 0)
    def _():
        for unit in units:
            scores(0, 0, unit, False)

        def pair(j, carry):
            carry = stage(2 * j + 1, 1, 2 * j, 0, carry)
            return stage(2 * j + 2, 0, 2 * j + 1, 1, carry)

        carry = lax.fori_loop(0, qi - 1, pair, carry0)
        carry = stage(2 * qi - 1, 1, 2 * qi - 2, 0, carry)
        carry = stage(2 * qi, 0, 2 * qi - 1, 1, carry, next_diagonal=True)
        finish(stage(2 * qi + 1, 1, 2 * qi, 0, carry, next_diagonal=True, cur_diagonal=True))


def _prompt_attention(qat, ka, vt, *, nb, seq, tq):
    tk = tq // 2
    ka3 = ka.reshape(nb, seq, N_HEADS * LANES)
    out = pl.pallas_call(
        functools.partial(_attn_kernel, tq=tq, tk=tk),
        out_shape=jax.ShapeDtypeStruct((nb, seq, ATT_WIDTH), BF16),
        grid=(nb, N_PAIRS, seq // tq),
        in_specs=[pl.BlockSpec((None, 2 * LANES, tq), lambda b, p, i: (b, p, i)),
                  pl.BlockSpec((None, seq, 2 * LANES), lambda b, p, i: (b, 0, p)),
                  pl.BlockSpec((None, LANES, seq), lambda b, p, i: (b, p, 0))],
        out_specs=pl.BlockSpec((None, tq, LANES), lambda b, p, i: (b, i, p)),
        scratch_shapes=([pltpu.VMEM((2 * tq // ATTN_QCHUNK, tk, ATTN_QCHUNK), F32)] * 2
                        + [pltpu.VMEM((2 * tq // ATTN_QCHUNK, 1, ATTN_QCHUNK), F32)] * 2),
        compiler_params=pltpu.CompilerParams(dimension_semantics=("arbitrary", "arbitrary", "arbitrary"),
                                             vmem_limit_bytes=VMEM_LIMIT),
        name="prompt_attention",
    )(qat, ka3, vt)
    return out.reshape(nb * seq, ATT_WIDTH)


class _SampleAttention:
    SCORE_ROWS = N_HEADS * SAMPLE_T
    NT_DIMS = (((1,), (1,)), ((), ()))

    def __init__(self, q_scr, m_scr, l_scr, acc_scr, carry_scr):
        self.q_scr, self.m_scr, self.l_scr, self.acc_scr, self.carry_scr = q_scr, m_scr, l_scr, acc_scr, carry_scr

    @staticmethod
    def scratch_shapes():
        rows = _SampleAttention.SCORE_ROWS
        return [pltpu.VMEM((rows, ATT_WIDTH), BF16), pltpu.VMEM((rows, 1), F32), pltpu.VMEM((rows, 1), F32),
                pltpu.VMEM((rows, ATT_WIDTH), F32), pltpu.VMEM((N_HEADS, LANES), F32)]

    @staticmethod
    def per_head_rows(a):
        return jnp.concatenate([jnp.broadcast_to(a[h:h + 1, :], (SAMPLE_T, a.shape[1])) for h in range(N_HEADS)],
                               axis=0)

    def update(self, s, pv):
        m_prev = self.m_scr[...]
        m_new = jnp.maximum(m_prev, jnp.max(s, axis=-1, keepdims=True))
        alpha = jnp.exp(m_prev - m_new)
        p = jnp.exp(s - m_new)
        self.l_scr[...] = alpha * self.l_scr[...] + jnp.sum(p, axis=-1, keepdims=True)
        self.acc_scr[...] = alpha * self.acc_scr[...] + pv(p.astype(BF16))
        self.m_scr[...] = m_new

    def start(self, q_ref, kn_ref, vn_ref, cn_ref):
        rows, q_scr, nt_dims, per_head_rows = self.SCORE_ROWS, self.q_scr, self.NT_DIMS, self.per_head_rows
        m_scr, l_scr, acc_scr, carry_scr, update = self.m_scr, self.l_scr, self.acc_scr, self.carry_scr, self.update
        q8 = q_ref[...].astype(F32)
        qt = jnp.concatenate([q8] * N_HEADS, axis=0)
        r_i = lax.broadcasted_iota(jnp.int32, (rows, ATT_WIDTH), 0)
        c_i = lax.broadcasted_iota(jnp.int32, (rows, ATT_WIDTH), 1)
        q_scr[...] = jnp.where(r_i // SAMPLE_T == c_i // HEAD_DIM, qt, 0.0).astype(BF16)
        m_scr[...] = jnp.full_like(m_scr, NEG_INF)
        l_scr[...] = jnp.zeros_like(l_scr)
        acc_scr[...] = jnp.zeros_like(acc_scr)
        carry_scr[...] = jnp.zeros_like(carry_scr)
        pad = jnp.zeros((PAGE_SIZE - SAMPLE_T, ATT_WIDTH), F32)
        k_new = jnp.concatenate([kn_ref[...], pad], axis=0).astype(BF16)
        v_new = jnp.concatenate([vn_ref[...], pad], axis=0).astype(BF16)
        c_new = jnp.concatenate([cn_ref[...], jnp.zeros((PAGE_SIZE - SAMPLE_T, LANES), F32)], axis=0)
        s = lax.dot_general(q_scr[...], k_new, nt_dims, preferred_element_type=F32)
        s = s - per_head_rows(c_new.T[:N_HEADS, :])
        r_s = lax.broadcasted_iota(jnp.int32, (rows, PAGE_SIZE), 0)
        c_s = lax.broadcasted_iota(jnp.int32, (rows, PAGE_SIZE), 1)
        update(jnp.where(c_s <= r_s % SAMPLE_T, s, NEG_INF),
               lambda p: jnp.dot(p, v_new, preferred_element_type=F32))

    def page_scores(self, k_pages, lf_pages, tri_ref):
        pieces = []
        for lf in lf_pages:
            hi = lf.astype(BF16).astype(F32)
            mid = (lf - hi).astype(BF16).astype(F32)
            lo = (lf - hi - mid).astype(BF16).astype(F32)
            pieces += [hi, mid, lo]
        sums = jnp.dot(jnp.concatenate(pieces, axis=0).astype(BF16), tri_ref[...], preferred_element_type=F32)
        carry = self.carry_scr[...]
        bias = []
        for i in range(len(lf_pages)):
            part = [sums[(3 * i + j) * N_HEADS:(3 * i + j + 1) * N_HEADS, :] for j in range(AUG)]
            local = part[0] + part[1] + part[2]
            bias.append(local[:, :PAGE_SIZE] + carry)
            carry = carry + local[:, PAGE_SIZE:]
        self.carry_scr[...] = carry
        kt = jnp.concatenate([k.astype(BF16) for k in k_pages], axis=1)
        s = jnp.dot(self.q_scr[...], kt, preferred_element_type=F32)
        return s + self.per_head_rows(jnp.concatenate(bias, axis=1))

    def absorb_pages(self, s, v_pages):
        vt = jnp.concatenate([v.astype(BF16) for v in v_pages], axis=1)
        self.update(s, lambda p: lax.dot_general(p, vt, self.NT_DIMS, preferred_element_type=F32))

    def finish(self, o_ref):
        rows = self.SCORE_ROWS
        o = self.acc_scr[...] / self.l_scr[...]
        r_i = lax.broadcasted_iota(jnp.int32, (rows, ATT_WIDTH), 0)
        c_i = lax.broadcasted_iota(jnp.int32, (rows, ATT_WIDTH), 1)
        o = jnp.where(r_i // SAMPLE_T == c_i // HEAD_DIM, o, 0.0).astype(BF16)
        t_i = lax.broadcasted_iota(jnp.int32, (SAMPLE_T, rows), 0)
        r_j = lax.broadcasted_iota(jnp.int32, (SAMPLE_T, rows), 1)
        sel = jnp.where(r_j % SAMPLE_T == t_i, 1.0, 0.0).astype(BF16)
        o_ref[...] = jnp.dot(sel, o, preferred_element_type=F32).astype(o_ref.dtype)


def _merge_tail(rows, conv, g1, x_ref, o_ref, ga_ref, gb_ref, dwb_ref, lng_ref, lnb_ref, wpa_ref, wpb_ref, bpb_ref,
                wo_ref, out_ref, ya=None):
    yb = conv + dwb_ref[...]
    mu = jnp.mean(yb, axis=-1, keepdims=True)
    var = jnp.mean(jnp.square(yb - mu), axis=-1, keepdims=True)
    yb = (yb - mu) * lax.rsqrt(var + EPS) * lng_ref[...] + lnb_ref[...]
    yb = jnp.dot(_silu(yb).astype(BF16), wpb_ref[...], preferred_element_type=F32) + bpb_ref[...]
    if ya is None:
        ya = jnp.dot(o_ref[rows, :], wpa_ref[...], preferred_element_type=F32)
    m = ga_ref[rows, :].astype(F32) * ya + gb_ref[rows, :].astype(F32) * yb
    out_ref[rows, :] = x_ref[rows, :] + g1 * jnp.dot(m.astype(BF16), wo_ref[...], preferred_element_type=F32)


def _merge_prompt_kernel(x_ref, o_ref, ucur_ref, uprev_ref, ga_ref, gb_ref, g1_ref, dww_ref, dwb_ref, lng_ref,
                         lnb_ref, wpa_ref, wpb_ref, bpb_ref, wo_ref, out_ref, ubuf, *, nseq, row_block):
    tm = x_ref.shape[0]
    first = pl.program_id(1) == 0
    prev = uprev_ref[...]
    ubuf[:, 0:CONV_HALO, :] = jnp.where(first, jnp.zeros_like(prev), prev)
    ubuf[:, CONV_HALO:, :] = ucur_ref[...]
    g1 = _mod_rows(g1_ref, nseq, False)
    chunk = 64
    ya_all = jnp.dot(o_ref[...], wpa_ref[...], preferred_element_type=F32)
    for b0 in range(0, tm, row_block):
        slabs = []
        for c in range(N_SLABS):
            pieces = []
            for r0 in range(b0, b0 + row_block, chunk):
                acc = jnp.zeros((chunk, LANES), F32)
                for j in range(CONV_K):
                    off = r0 + CONV_HALO - (CONV_K - 1) + j
                    acc = acc + dww_ref[c, j:j + 1, :] * ubuf[c, off:off + chunk, :]
                pieces.append(acc)
            slabs.append(jnp.concatenate(pieces, axis=0))
        _merge_tail(slice(b0, b0 + row_block), jnp.concatenate(slabs, axis=1), g1, x_ref, o_ref, ga_ref, gb_ref,
                    dwb_ref, lng_ref, lnb_ref, wpa_ref, wpb_ref, bpb_ref, wo_ref, out_ref,
                    ya=ya_all[b0:b0 + row_block])


def _merge_sample_kernel(x_ref, o_ref, uwin_ref, ga_ref, gb_ref, g1_ref, dww_ref, dwb_ref, lng_ref,
                         lnb_ref, wpa_ref, wpb_ref, bpb_ref, wo_ref, out_ref, *, nseq):
    slabs = []
    for c in range(N_SLABS):
        acc = jnp.zeros((nseq, SAMPLE_T, LANES), F32)
        for j in range(CONV_K):
            acc = acc + dww_ref[c, j:j + 1, :] * uwin_ref[c, :, j:j + SAMPLE_T, :]
        slabs.append(acc.reshape(nseq * SAMPLE_T, LANES))
    _merge_tail(slice(None), jnp.concatenate(slabs, axis=1), _mod_rows(g1_ref, nseq, True), x_ref, o_ref, ga_ref,
                gb_ref, dwb_ref, lng_ref, lnb_ref, wpa_ref, wpb_ref, bpb_ref, wo_ref, out_ref)


def _merge(x, o, u_args, ga, gb, mod, weights, *, nb, tm, sample, nseq):
    n = x.shape[0]
    nt = n // (nb * tm)
    row = lambda b, i: (b * nt + i, 0)
    const2 = lambda b, i: (0, 0)
    const3 = lambda b, i: (0, 0, 0)
    once = dict(pipeline_mode=pl.Buffered(1))
    mod_spec = _mod_spec(mod, 2)
    if sample:
        (uwin,) = u_args
        u_specs = [pl.BlockSpec(uwin.shape, lambda b, i: (0, 0, 0, 0))]
        kernel, scratch = functools.partial(_merge_sample_kernel, nseq=nseq), []
    else:
        (u,) = u_args
        u_args = (u, u)
        blocks_per_tile = tm // CONV_HALO
        u_specs = [pl.BlockSpec((N_SLABS, tm, LANES), lambda b, i: (0, b * nt + i, 0)),
                   pl.BlockSpec((N_SLABS, CONV_HALO, LANES),
                                lambda b, i: (0, jnp.maximum((b * nt + i) * blocks_per_tile - 1, 0), 0))]
        kernel = functools.partial(_merge_prompt_kernel, nseq=nseq, row_block=tm // 2)
        scratch = [pltpu.VMEM((N_SLABS, CONV_HALO + tm, LANES), F32)]
    dww, dwb, lng, lnb, wpa, wpb, bpb, wo = weights
    in_specs = ([pl.BlockSpec((tm, D_MODEL), row), pl.BlockSpec((tm, ATT_WIDTH), row)] + u_specs
                + [pl.BlockSpec((tm, D_MODEL), row), pl.BlockSpec((tm, D_MODEL), row), mod_spec,
                   pl.BlockSpec(dww.shape, const3),
                   pl.BlockSpec((1, CONV_WIDTH), const2), pl.BlockSpec((1, CONV_WIDTH), const2),
                   pl.BlockSpec((1, CONV_WIDTH), const2),
                   pl.BlockSpec((ATT_WIDTH, D_MODEL), const2, **once),
                   pl.BlockSpec((CONV_WIDTH, D_MODEL), const2, **once),
                   pl.BlockSpec((1, D_MODEL), const2),
                   pl.BlockSpec((D_MODEL, D_MODEL), const2, **once)])
    return pl.pallas_call(
        kernel,
        out_shape=jax.ShapeDtypeStruct((n, D_MODEL), F32),
        grid=(nb, nt),
        in_specs=in_specs,
        out_specs=pl.BlockSpec((tm, D_MODEL), row),
        scratch_shapes=scratch,
        compiler_params=pltpu.CompilerParams(dimension_semantics=("arbitrary", "arbitrary"),
                                             vmem_limit_bytes=VMEM_LIMIT),
        name="merge_sample" if sample else "merge_prompt",
    )(x, o, *u_args, ga, gb, mod, dww, dwb, lng, lnb, wpa, wpb, bpb, wo)


def _ffn_steps(x_ref, sh_ref, sc_ref, g2_ref, rg_ref, fg_ref, win_ref, wout_ref, out_ref, *, bounds, sample, nseq):
    x = x_ref[...]
    ms = jnp.mean(x * x, axis=-1, keepdims=True)
    h = x * lax.rsqrt(ms + EPS) * rg_ref[...]
    hb = (h * (1.0 + _mod_rows(sc_ref, nseq, sample)) + _mod_rows(sh_ref, nseq, sample)).astype(BF16)
    acc = jnp.zeros(x.shape, F32)
    for lo, hi in zip(bounds[:-1], bounds[1:]):
        gate = jnp.dot(hb, win_ref[:, lo:hi], preferred_element_type=F32)
        up = jnp.dot(hb, win_ref[:, FFN_HIDDEN + lo:FFN_HIDDEN + hi], preferred_element_type=F32)
        yield
        act = (_silu(gate) * up).astype(BF16)
        acc = acc + jnp.dot(act, wout_ref[lo:hi, :], preferred_element_type=F32)
        if hi == bounds[-1]:
            x2 = x + _mod_rows(g2_ref, nseq, sample) * acc
            ms2 = jnp.mean(x2 * x2, axis=-1, keepdims=True)
            out_ref[...] = x2 * lax.rsqrt(ms2 + EPS) * fg_ref[...]
        yield


def _hidden_bounds(n_chunks):
    tiles = FFN_HIDDEN // MXU_TILE
    assert tiles * MXU_TILE == FFN_HIDDEN
    return [MXU_TILE * ((tiles * k) // n_chunks) for k in range(n_chunks + 1)]


def _ffn_kernel(*refs, sample, nseq):
    for _ in _ffn_steps(*refs, bounds=_hidden_bounds(2), sample=sample, nseq=nseq):
        pass


def _ffn_attn_kernel(pt_ref, x_ref, sh_ref, sc_ref, g2_ref, rg_ref, fg_ref, win_ref, wout_ref,
                     q_ref, kn_ref, vn_ref, cn_ref, tri_ref, ck_hbm, cv_hbm, clf_hbm,
                     out_ref, o_ref, kbuf, vbuf, lbuf, sem, *state, nseq, n_pages):
    npg = PAGES_PER_STEP
    n_chunks = n_pages // npg
    seq = pl.program_id(0) * pl.num_programs(1) + pl.program_id(1)
    attn = _SampleAttention(*state)

    def copies(sequence, c):
        slot, out = c % 2, []
        for j in range(npg):
            page = pt_ref[sequence * n_pages + (n_pages - 1) - (c * npg + j)]
            out += [pltpu.make_async_copy(ck_hbm.at[page], kbuf.at[slot, j], sem.at[0, slot]),
                    pltpu.make_async_copy(cv_hbm.at[page], vbuf.at[slot, j], sem.at[1, slot]),
                    pltpu.make_async_copy(clf_hbm.at[page], lbuf.at[slot, j], sem.at[2, slot])]
        return out

    def start(sequence, c):
        for cp in copies(sequence, c):
            cp.start()

    @pl.when(seq == 0)
    def _():
        start(seq, 0)

    ffn = _ffn_steps(x_ref, sh_ref, sc_ref, g2_ref, rg_ref, fg_ref, win_ref, wout_ref, out_ref,
                     bounds=_hidden_bounds(n_chunks), sample=False, nseq=nseq)
    for c in range(n_chunks):
        for cp in copies(seq, c):
            cp.wait()
        if c + 1 < n_chunks:
            start(seq, c + 1)
        else:
            @pl.when(seq + 1 < nseq)
            def _():
                start(seq + 1, 0)
        if c == 0:
            attn.start(q_ref, kn_ref, vn_ref, cn_ref)
        slot = c % 2
        next(ffn)
        s = attn.page_scores([kbuf[slot, j] for j in range(npg)], [lbuf[slot, j] for j in range(npg)], tri_ref)
        next(ffn)
        attn.absorb_pages(s, [vbuf[slot, j] for j in range(npg)])
    attn.finish(o_ref)


def _ffn(x, mod, rms_g, final_g, w_in, w_out, *, nb, tm, sample, nseq):
    n = x.shape[0]
    nt = n // (nb * tm)
    row = lambda b, i: (b * nt + i, 0)
    const2 = lambda b, i: (0, 0)
    once = dict(pipeline_mode=pl.Buffered(1))
    return pl.pallas_call(
        functools.partial(_ffn_kernel, sample=sample, nseq=nseq),
        out_shape=jax.ShapeDtypeStruct((n, D_MODEL), F32),
        grid=(nb, nt),
        in_specs=[pl.BlockSpec((tm, D_MODEL), row), _mod_spec(mod, 3), _mod_spec(mod, 4), _mod_spec(mod, 5),
                  pl.BlockSpec((1, D_MODEL), const2), pl.BlockSpec((1, D_MODEL), const2),
                  pl.BlockSpec((D_MODEL, 2 * FFN_HIDDEN), const2, **once),
                  pl.BlockSpec((FFN_HIDDEN, D_MODEL), const2, **once)],
        out_specs=pl.BlockSpec((tm, D_MODEL), row),
        compiler_params=pltpu.CompilerParams(dimension_semantics=("arbitrary", "arbitrary"),
                                             vmem_limit_bytes=VMEM_LIMIT),
        name="ffn_sample" if sample else "ffn_prompt",
    )(x, mod, mod, mod, rms_g, final_g, w_in, w_out)


def _ffn_with_sample_attention(x, mod, rms_g, final_g, w_in, w_out, page_table, q, k_new, v_new, c_new, tri_page,
                               cache_k, cache_v, cache_logf, *, nb, tm):
    n = x.shape[0]
    nt = n // (nb * tm)
    nseq, n_pages = page_table.shape
    assert nseq == nb * nt and n_pages % PAGES_PER_STEP == 0
    row = lambda b, i, pt: (b * nt + i, 0)
    const2 = lambda b, i, pt: (0, 0)
    once = dict(pipeline_mode=pl.Buffered(1))
    mod_spec = lambda kind: pl.BlockSpec((None, mod.shape[1], D_MODEL), lambda b, i, pt: (kind, 0, 0))
    seq_spec = lambda width: pl.BlockSpec((None, SAMPLE_T, width), lambda b, i, pt: (b * nt + i, 0, 0))
    hbm = pl.BlockSpec(memory_space=pl.ANY)
    slots = 2
    grid_spec = pltpu.PrefetchScalarGridSpec(
        num_scalar_prefetch=1,
        grid=(nb, nt),
        in_specs=[pl.BlockSpec((tm, D_MODEL), row), mod_spec(3), mod_spec(4), mod_spec(5),
                  pl.BlockSpec((1, D_MODEL), const2), pl.BlockSpec((1, D_MODEL), const2),
                  pl.BlockSpec((D_MODEL, 2 * FFN_HIDDEN), const2, **once),
                  pl.BlockSpec((FFN_HIDDEN, D_MODEL), const2, **once),
                  seq_spec(ATT_WIDTH), seq_spec(ATT_WIDTH), seq_spec(ATT_WIDTH), seq_spec(LANES),
                  pl.BlockSpec((PAGE_SIZE, 2 * PAGE_SIZE), const2), hbm, hbm, hbm],
        out_specs=[pl.BlockSpec((tm, D_MODEL), row), seq_spec(ATT_WIDTH)],
        scratch_shapes=[pltpu.VMEM((slots, PAGES_PER_STEP, ATT_WIDTH, PAGE_SIZE), F32),
                        pltpu.VMEM((slots, PAGES_PER_STEP, ATT_WIDTH, PAGE_SIZE), F32),
                        pltpu.VMEM((slots, PAGES_PER_STEP, N_HEADS, PAGE_SIZE), F32),
                        pltpu.SemaphoreType.DMA((3, slots))] + _SampleAttention.scratch_shapes())
    return pl.pallas_call(
        functools.partial(_ffn_attn_kernel, nseq=nseq, n_pages=n_pages),
        out_shape=[jax.ShapeDtypeStruct((n, D_MODEL), F32),
                   jax.ShapeDtypeStruct((nseq, SAMPLE_T, ATT_WIDTH), BF16)],
        grid_spec=grid_spec,
        compiler_params=pltpu.CompilerParams(dimension_semantics=("arbitrary", "arbitrary"),
                                             vmem_limit_bytes=VMEM_LIMIT),
        name="ffn_prompt_sample_attention",
    )(page_table.reshape(-1), x, mod, mod, mod, rms_g, final_g, w_in, w_out, q, k_new, v_new, c_new, tri_page,
      cache_k, cache_v, cache_logf)


def _bias_placement():
    half = N_HEADS * LANES
    place = np.zeros((LANES, 2 * half), np.float32)
    ones = np.zeros((1, 2 * half), np.float32)
    for h in range(N_HEADS):
        base = h * LANES + (HEAD_DIM if h % 2 == 0 else 0)
        for piece in range(AUG):
            place[piece * N_HEADS + h, base + piece] = 1.0
            ones[0, base + AUG + piece] = 1.0
            ones[0, half + base + piece] = 1.0
            place[piece * N_HEADS + h, half + base + AUG + piece] = -1.0
    return jnp.asarray(place, BF16), jnp.asarray(ones, F32)


def _lower_tri(n, block):
    t = np.arange(n)[:, None]
    s = np.arange(n)[None, :]
    return jnp.asarray(((s <= t) & (t // block == s // block)).astype(np.float32), BF16)


def _later_keys(n):
    j = np.arange(n)[:, None]
    s = np.arange(n)[None, :]
    return jnp.asarray(np.concatenate([(j > s).astype(np.float32), np.ones((n, n), np.float32)], axis=1), BF16)


def kernel(x_prompt, x_sample, c_prompt, c_sample, cache_k, cache_v, cache_logf, state_conv, page_table, rms1_g, rms2_g, w_ada, b_ada, w_in, b_in, dw_w, dw_b, ln_g, ln_b, w_pa, w_pb, b_pb, w_o, w_ffn_in, w_ffn_out, final_g):
    nb, seq, _ = x_prompt.shape
    nseq, dec_t, _ = x_sample.shape
    depth = w_in.shape[0]
    assert depth == 1 and dec_t <= SAMPLE_T
    n_prompt = nb * seq
    tm = 512
    n_sample = nseq * SAMPLE_T

    wt, b = w_in[0].T, b_in[0]
    g_off = 3 * ATT_WIDTH + N_HEADS
    w_all = jnp.concatenate([wt[:g_off], jnp.zeros((LANES - N_HEADS, D_MODEL), F32), wt[g_off:]], axis=0).astype(BF16)
    b_all = jnp.concatenate([b[:g_off], jnp.zeros((LANES - N_HEADS,), F32), b[g_off:]])[None, :]
    dww = jnp.pad(dw_w[0], ((0, CONV_HALO - CONV_K), (0, 0))).reshape(CONV_HALO, N_SLABS, LANES).transpose(1, 0, 2)
    merge_w = (dww, dw_b[0][None, :], ln_g[0][None, :], ln_b[0][None, :], w_pa[0].astype(BF16),
               w_pb[0].astype(BF16), b_pb[0][None, :], w_o[0].astype(BF16))
    wf_in, wf_out = w_ffn_in[0].astype(BF16), w_ffn_out[0].astype(BF16)
    g1w, g2w, gfw = rms1_g[0][None, :], rms2_g[0][None, :], final_g[None, :]
    place, ones_row = _bias_placement()

    n_cond = nb + nseq
    c_all = jnp.pad(jnp.concatenate([c_sample, c_prompt], axis=0), ((0, -n_cond % 8), (0, 0)))
    mod = _modulation(c_all, w_ada[0], b_ada[0][None, :])

    xs = jnp.pad(x_sample, ((0, 0), (0, SAMPLE_T - dec_t), (0, 0))).reshape(n_sample, D_MODEL)
    (q_s, k_s, v_s, lf_s, c_s, u_s, ga_s, gb_s) = _inproj(
        xs, mod, g1w, w_all, b_all, _lower_tri(n_sample, SAMPLE_T), place, ones_row,
        nb=1, tm=n_sample, sample=True, nseq=nseq)

    xp = x_prompt.reshape(n_prompt, D_MODEL)
    (qa, ka, kt_p, vt_p, vb_p, lft_p, u_p, ga_p, gb_p) = _inproj(
        xp, mod, g1w, w_all, b_all, _lower_tri(tm, tm), place, ones_row, nb=nb, tm=tm, sample=False, nseq=nseq)
    o_p = _prompt_attention(qa, ka, vb_p, nb=nb, seq=seq, tq=2 * tm)
    x1_p = _merge(xp, o_p, (u_p,), ga_p, gb_p, mod, merge_w, nb=nb, tm=tm, sample=False, nseq=nseq)
    n_phys = cache_k.shape[1]
    page_t = lambda c: c[0].transpose(0, 2, 3, 1).reshape(n_phys, ATT_WIDTH, PAGE_SIZE)
    y_p, o_s = _ffn_with_sample_attention(
        x1_p, mod, g2w, gfw, wf_in, wf_out,
        page_table, q_s.reshape(nseq, SAMPLE_T, ATT_WIDTH), k_s.reshape(nseq, SAMPLE_T, ATT_WIDTH),
        v_s.reshape(nseq, SAMPLE_T, ATT_WIDTH), c_s.reshape(nseq, SAMPLE_T, LANES), _later_keys(PAGE_SIZE),
        page_t(cache_k), page_t(cache_v), cache_logf[0].transpose(0, 2, 1), nb=nb, tm=tm)

    state_slabs = state_conv[0].reshape(nseq, CONV_K - 1, N_SLABS, LANES).transpose(2, 0, 1, 3)
    u_slabs = u_s.reshape(N_SLABS, nseq, SAMPLE_T, LANES)
    uwin = jnp.concatenate(
        [state_slabs, u_slabs, jnp.zeros((N_SLABS, nseq, SAMPLE_WIN - (CONV_K - 1) - SAMPLE_T, LANES), F32)], axis=2)
    x1_s = _merge(xs, o_s.reshape(n_sample, ATT_WIDTH), (uwin,), ga_s, gb_s, mod, merge_w,
                  nb=1, tm=n_sample, sample=True, nseq=nseq)
    y_s = _ffn(x1_s, mod, g2w, gfw, wf_in, wf_out, nb=1, tm=n_sample, sample=True, nseq=nseq)

    n_pg = seq // PAGE_SIZE
    tail = CONV_K - 1
    u_tail = u_p.reshape(N_SLABS, nb, seq, LANES)[:, :, seq - tail:]
    u_tail = u_tail.transpose(1, 2, 0, 3).reshape(nb, tail, CONV_WIDTH)
    us_rows = u_s.reshape(N_SLABS, nseq, SAMPLE_T, LANES)[:, :, :dec_t]
    us_rows = us_rows.transpose(1, 2, 0, 3).reshape(nseq, dec_t, CONV_WIDTH)
    unpad = lambda a, width: a.reshape(nseq, SAMPLE_T, *width)[:, :dec_t]
    return (y_p.reshape(nb, seq, D_MODEL),
            unpad(y_s, (D_MODEL,)),
            kt_p.reshape(1, nb, n_pg, N_HEADS, HEAD_DIM, PAGE_SIZE).transpose(0, 1, 2, 5, 3, 4),
            vt_p.reshape(1, nb, n_pg, N_HEADS, HEAD_DIM, PAGE_SIZE).transpose(0, 1, 2, 5, 3, 4),
            lft_p.reshape(1, nb, n_pg, N_HEADS, PAGE_SIZE).transpose(0, 1, 2, 4, 3),
            u_tail[None],
            unpad(k_s, (N_HEADS, HEAD_DIM))[None],
            unpad(v_s, (N_HEADS, HEAD_DIM))[None],
            unpad(lf_s, (N_HEADS,))[None],
            jnp.concatenate([state_conv[0][:, dec_t:], us_rows], axis=1)[None])
```
